```python
import math
import jax, jax.numpy as jnp
from jax import lax
import numpy as np

D_MODEL = 1024
BATCH = 16
SEQ = 2048
DEPTH = 1
DEC_BATCH = 128
DEC_SEQ = 8
PAST_LEN = 8192
PAGE_SIZE = 128

NSA_HEADS = 8
NSA_KV_HEADS = 2
NSA_HEAD_DIM = 64
NSA_GROUP = NSA_HEADS // NSA_KV_HEADS
NSA_WIDTH = NSA_HEADS * NSA_HEAD_DIM
CMP_BLOCK = 32
CMP_STRIDE = 16
CMP_HIDDEN = 2 * NSA_HEAD_DIM
SEL_BLOCK = 64
SEL_TOPK = 16
WINDOW = 512
FORCE_SCORE = 1.0e4
GLA_HEADS = 4
GLA_DK = 64
GLA_DV = 128
GLA_WIDTH = GLA_HEADS * GLA_DV
GLA_GATE_RANK = 16
GLA_GATE_TAU = 16.0
GLA_CHUNK = 16
MIX_WIDTH = NSA_WIDTH + GLA_WIDTH
FFN_HIDDEN = -(-8 * D_MODEL // (3 * 256)) * 256
ROPE_THETA = 10000.0
EPS = 1e-6
Q_BLOCK = 128
IN_COLS = (NSA_WIDTH, 6 * NSA_KV_HEADS * NSA_HEAD_DIM, 3 * NSA_HEADS,
           GLA_HEADS * GLA_DK, GLA_HEADS * GLA_DK, GLA_WIDTH, GLA_GATE_RANK, GLA_WIDTH)
IN_WIDTH = sum(IN_COLS)

kernel_name = 'nsa_gla_hybrid_step'


def _split_in(p):
    parts, off = [], 0
    for width in IN_COLS:
        parts.append(p[..., off:off + width])
        off += width
    return parts


def _rms_norm(x, g):
    xf = x.astype(jnp.float32)
    y = xf * lax.rsqrt(jnp.mean(xf * xf, axis=-1, keepdims=True) + EPS)
    return (y * g.astype(jnp.float32)).astype(x.dtype)


def _rope(x, pos):
    half = x.shape[-1] // 2
    inv = ROPE_THETA ** (-jnp.arange(half, dtype=jnp.float32) / half)
    ang = pos.astype(jnp.float32)[:, None] * inv[None, :]
    cos, sin = jnp.cos(ang)[:, None, :], jnp.sin(ang)[:, None, :]
    xf = x.astype(jnp.float32)
    x1, x2 = xf[..., :half], xf[..., half:]
    return jnp.concatenate([x1 * cos - x2 * sin, x2 * cos + x1 * sin], axis=-1).astype(x.dtype)


def _masked_softmax(s, mask):
    s = jnp.where(mask, s.astype(jnp.float32), -jnp.inf)
    m = jnp.max(s, axis=-1, keepdims=True)
    m = jnp.where(jnp.isfinite(m), m, 0.0)
    e = jnp.exp(s - m)
    return e / jnp.maximum(jnp.sum(e, axis=-1, keepdims=True), 1e-30)


def _compress(rows, pos_emb, w1, w2):
    b, length, h, d = rows.shape
    r = CMP_BLOCK // CMP_STRIDE
    n_chunks = length // CMP_STRIDE
    n_cmp = n_chunks - r + 1
    chunks = rows[:, :n_chunks * CMP_STRIDE].reshape(b, n_chunks, CMP_STRIDE, h, d)
    w1r = w1.reshape(r, CMP_STRIDE, d, CMP_HIDDEN)
    hid = jnp.einsum('ld,lde->e', pos_emb, w1)
    for j in range(r):
        hid = hid + jnp.einsum('bnshd,sde->bnhe', chunks[:, j:j + n_cmp], w1r[j])
    return jnp.einsum('bnhe,ed->bnhd', jax.nn.silu(hid), w2)


def _overlap(n_cmp, n_sel):
    start = jnp.arange(n_cmp)[:, None] * CMP_STRIDE
    j0 = jnp.arange(n_sel)[None, :] * SEL_BLOCK
    return ((start < j0 + SEL_BLOCK) & (start + CMP_BLOCK > j0)).astype(jnp.float32)


def _cmp_attend(q, kc, vc, qpos):
    b, t, _, d = q.shape
    qg = q.reshape(b, t, NSA_KV_HEADS, NSA_GROUP, d)
    s = jnp.einsum('btkgd,bnkd->bkgtn', qg, kc) * (d ** -0.5)
    end = jnp.arange(kc.shape[1]) * CMP_STRIDE + CMP_BLOCK - 1
    p = _masked_softmax(s, end[None, :] <= qpos[:, None])
    o = jnp.einsum('bkgtn,bnkd->btkgd', p.astype(vc.dtype), vc)
    return o.reshape(b, t, NSA_HEADS, d), p


def _select(p, qpos, n_sel):
    imp = jnp.einsum('bkgtn,nj->bktj', p, _overlap(p.shape[-1], n_sel))
    blk = jnp.arange(n_sel)[None, :]
    visible = blk * SEL_BLOCK <= qpos[:, None]
    forced = (blk == 0) | (blk == (qpos // SEL_BLOCK)[:, None])
    imp = jnp.where(forced, FORCE_SCORE, jnp.where(visible, imp, -jnp.inf))
    vals, idx = lax.top_k(imp, min(SEL_TOPK, n_sel))
    return idx, jnp.isfinite(vals)


def _sel_attend(q, kg, vg, idx, valid, qpos):
    hk, nq, nk, sb, d = kg.shape
    s = jnp.einsum('qkgd,kqnsd->kgqns', q, kg) * (d ** -0.5)
    kpos = idx[..., None] * SEL_BLOCK + jnp.arange(SEL_BLOCK)
    mask = valid[..., None] & (kpos <= qpos[None, :, None, None])
    p = _masked_softmax(s.reshape(hk, NSA_GROUP, nq, nk * sb), mask.reshape(hk, 1, nq, nk * sb))
    return jnp.einsum('kgqm,kqmd->qkgd', p.astype(vg.dtype), vg.reshape(hk, nq, nk * sb, d))


def _win_attend(q, k, v, qpos, kpos):
    d = q.shape[-1]
    s = jnp.einsum('bqkgd,bmkd->bkgqm', q, k) * (d ** -0.5)
    rel = qpos[:, None] - kpos[None, :]
    mask = (rel >= 0) & (rel < WINDOW) & (kpos[None, :] >= 0)
    p = _masked_softmax(s, mask)
    return jnp.einsum('bkgqm,bmkd->bqkgd', p.astype(v.dtype), v)


def _nsa_project(q_raw, kv_raw, gate_raw, pos, q_gain, k_gain):
    b, t, _ = q_raw.shape
    q = _rms_norm(q_raw.reshape(b, t, NSA_HEADS, NSA_HEAD_DIM), q_gain)
    kv = kv_raw.reshape(b, t, 6, NSA_KV_HEADS, NSA_HEAD_DIM)
    k_slc = _rope(_rms_norm(kv[:, :, 2], k_gain[1]), pos)
    k_win = _rope(_rms_norm(kv[:, :, 4], k_gain[2]), pos)
    rows = jnp.stack([kv[:, :, 0], kv[:, :, 1], k_slc, kv[:, :, 3]], axis=2)
    win_rows = jnp.stack([k_win, kv[:, :, 5]], axis=2)
    gates = jax.nn.sigmoid(gate_raw.astype(jnp.float32)).reshape(b, t, NSA_HEADS, 3)
    return q, _rope(q, pos), rows, win_rows, gates


def _combine(o_c, o_s, o_w, gates):
    b, t = o_c.shape[:2]
    g = gates.astype(o_c.dtype)
    o = g[..., 0:1] * o_c + g[..., 1:2] * o_s + g[..., 2:3] * o_w
    return o.reshape(b, t, NSA_WIDTH)


def _nsa_prompt(q, q_rot, rows, win_rows, gates, pos, k_gain, cmp_k, cmp_v):
    b, t = q.shape[:2]
    kc = _rms_norm(_compress(rows[:, :, 0], *cmp_k), k_gain[0])
    vc = _compress(rows[:, :, 1], *cmp_v)
    o_c, p = _cmp_attend(q, kc, vc, pos)
    n_sel = -(-t // SEL_BLOCK)
    idx, valid = _select(p, pos, n_sel)
    qb = min(Q_BLOCK, t)
    nqb = t // qb
    pad = n_sel * SEL_BLOCK - t
    k_blocks = jnp.pad(rows[:, :, 2], ((0, 0), (0, pad), (0, 0), (0, 0))).reshape(b, n_sel, SEL_BLOCK, NSA_KV_HEADS, NSA_HEAD_DIM)
    v_blocks = jnp.pad(rows[:, :, 3], ((0, 0), (0, pad), (0, 0), (0, 0))).reshape(b, n_sel, SEL_BLOCK, NSA_KV_HEADS, NSA_HEAD_DIM)
    n_k = idx.shape[-1]
    q_b = q_rot.reshape(b * nqb, qb, NSA_KV_HEADS, NSA_GROUP, NSA_HEAD_DIM)
    idx_b = idx.reshape(b, NSA_KV_HEADS, nqb, qb, n_k).transpose(0, 2, 1, 3, 4).reshape(b * nqb, NSA_KV_HEADS, qb, n_k)
    val_b = valid.reshape(b, NSA_KV_HEADS, nqb, qb, n_k).transpose(0, 2, 1, 3, 4).reshape(b * nqb, NSA_KV_HEADS, qb, n_k)
    seq_id = jnp.repeat(jnp.arange(b), nqb)
    pos_b = jnp.tile(pos.reshape(nqb, qb), (b, 1))
    heads = jnp.arange(NSA_KV_HEADS)[:, None, None]

    def sel_body(args):
        qi, ii, vi, si, pi = args
        kg = k_blocks[si, ii, :, heads]
        vg = v_blocks[si, ii, :, heads]
        return _sel_attend(qi, kg, vg, ii, vi, pi)

    o_s = lax.map(sel_body, (q_b, idx_b, val_b, seq_id, pos_b)).reshape(b, t, NSA_HEADS, NSA_HEAD_DIM)
    kw = jnp.pad(win_rows[:, :, 0], ((0, 0), (WINDOW, 0), (0, 0), (0, 0)))
    vw = jnp.pad(win_rows[:, :, 1], ((0, 0), (WINDOW, 0), (0, 0), (0, 0)))
    q_w = q_rot.reshape(b, nqb, qb, NSA_KV_HEADS, NSA_GROUP, NSA_HEAD_DIM)

    def win_body(i):
        start = i * qb
        k_i = lax.dynamic_slice_in_dim(kw, start, qb + WINDOW, axis=1)
        v_i = lax.dynamic_slice_in_dim(vw, start, qb + WINDOW, axis=1)
        q_i = lax.dynamic_index_in_dim(q_w, i, axis=1, keepdims=False)
        qpos = start + jnp.arange(qb)
        kpos = start - WINDOW + jnp.arange(qb + WINDOW)
        return _win_attend(q_i, k_i, v_i, qpos, kpos)

    o_w = lax.map(win_body, jnp.arange(nqb))
    o_w = o_w.transpose(1, 0, 2, 3, 4, 5).reshape(b, t, NSA_HEADS, NSA_HEAD_DIM)
    return _combine(o_c, o_s, o_w, gates)


def _nsa_sample(q, q_rot, rows, win_rows, gates, pos, k_gain, cmp_k, cmp_v, cache_nsa_kv, layer, cache_win, page_table):
    b, t = q.shape[:2]
    past_len = page_table.shape[1] * PAGE_SIZE
    past = cache_nsa_kv[layer, page_table, :, 0:2].reshape(b, past_len, 2, NSA_KV_HEADS, NSA_HEAD_DIM)
    k_all = jnp.concatenate([past[:, :, 0], rows[:, :, 0]], axis=1)
    v_all = jnp.concatenate([past[:, :, 1], rows[:, :, 1]], axis=1)
    kc = _rms_norm(_compress(k_all, *cmp_k), k_gain[0])
    vc = _compress(v_all, *cmp_v)
    o_c, p = _cmp_attend(q, kc, vc, pos)
    n_sel = -(-(past_len + t) // SEL_BLOCK)
    nb_past = past_len // SEL_BLOCK
    nb_new = n_sel - nb_past
    idx, valid = _select(p, pos, n_sel)
    pad = nb_new * SEL_BLOCK - t
    new_k = jnp.pad(rows[:, :, 2], ((0, 0), (0, pad), (0, 0), (0, 0))).reshape(b, nb_new, SEL_BLOCK, NSA_KV_HEADS, NSA_HEAD_DIM)
    new_v = jnp.pad(rows[:, :, 3], ((0, 0), (0, pad), (0, 0), (0, 0))).reshape(b, nb_new, SEL_BLOCK, NSA_KV_HEADS, NSA_HEAD_DIM)
    blocks_per_page = PAGE_SIZE // SEL_BLOCK
    heads = jnp.arange(NSA_KV_HEADS)[:, None, None]
    offs = jnp.arange(SEL_BLOCK)

    def sel_body(args):
        qi, ii, vi, pt, nk, nv = args
        jp = jnp.minimum(ii, nb_past - 1)
        page = pt[jp // blocks_per_page][..., None]
        row = (jp % blocks_per_page)[..., None] * SEL_BLOCK + offs
        hh = heads[..., None]
        jn = jnp.clip(ii - nb_past, 0, nb_new - 1)
        is_past = (ii < nb_past)[..., None, None]
        kg = jnp.where(is_past, cache_nsa_kv[layer, page, row, 2, hh], nk[jn, :, heads])
        vg = jnp.where(is_past, cache_nsa_kv[layer, page, row, 3, hh], nv[jn, :, heads])
        return _sel_attend(qi, kg, vg, ii, vi, pos)

    q_g = q_rot.reshape(b, t, NSA_KV_HEADS, NSA_GROUP, NSA_HEAD_DIM)
    o_s = lax.map(sel_body, (q_g, idx, valid, page_table, new_k, new_v)).reshape(b, t, NSA_HEADS, NSA_HEAD_DIM)
    wb = cache_win.shape[1]
    win_all = jnp.concatenate([cache_win, win_rows], axis=1)
    kpos = past_len - wb + jnp.arange(wb + t)
    o_w = _win_attend(q_g, win_all[:, :, 0], win_all[:, :, 1], pos, kpos).reshape(b, t, NSA_HEADS, NSA_HEAD_DIM)
    return _combine(o_c, o_s, o_w, gates), win_all[:, -wb:]


def _gla(q_raw, k_raw, v_raw, a_low, g_raw, s0, w_a2, b_a, norm_gain):
    b, t, _ = q_raw.shape
    f32 = jnp.float32
    q = q_raw.reshape(b, t, GLA_HEADS, GLA_DK).astype(f32) * (GLA_DK ** -0.5)
    k = k_raw.reshape(b, t, GLA_HEADS, GLA_DK).astype(f32)
    v = v_raw.reshape(b, t, GLA_HEADS, GLA_DV).astype(f32)
    la = jax.nn.log_sigmoid((jnp.einsum('btr,re->bte', a_low, w_a2) + b_a).astype(f32))
    la = la.reshape(b, t, GLA_HEADS, GLA_DK) / GLA_GATE_TAU
    c = math.gcd(t, GLA_CHUNK)
    n = t // c

    def to_chunks(z):
        return z.reshape(b, n, c, GLA_HEADS, z.shape[-1]).transpose(1, 0, 3, 2, 4)

    tri = jnp.tril(jnp.ones((c, c), dtype=bool))[:, :, None]

    def body(state, inp):
        qc, kc, vc, lc = inp
        bcum = jnp.cumsum(lc, axis=2)
        o_inter = jnp.einsum('bhtd,bhde->bhte', qc * jnp.exp(bcum), state)
        rel = bcum[:, :, :, None, :] - bcum[:, :, None, :, :]
        decay = jnp.exp(jnp.where(tri, rel, -jnp.inf))
        a = jnp.einsum('bhtd,bhsd,bhtsd->bhts', qc, kc, decay)
        o = o_inter + jnp.einsum('bhts,bhse->bhte', a, vc)
        last = bcum[:, :, -1:, :]
        state = jnp.exp(last[:, :, 0, :])[..., None] * state + jnp.einsum('bhsd,bhse->bhde', kc * jnp.exp(last - bcum), vc)
        return state, o

    s_fin, o = lax.scan(body, s0.astype(f32), (to_chunks(q), to_chunks(k), to_chunks(v), to_chunks(la)))
    o = o.transpose(1, 0, 3, 2, 4).reshape(b, t, GLA_HEADS, GLA_DV)
    o = _rms_norm(o, norm_gain) * jax.nn.silu(g_raw.astype(f32).reshape(b, t, GLA_HEADS, GLA_DV))
    return o.reshape(b, t, GLA_WIDTH).astype(q_raw.dtype), s_fin.astype(s0.dtype)


def _residual_out(x, o_nsa, o_gla, w_out, g_ffn, w_up, w_down):
    x = x + jnp.concatenate([o_nsa, o_gla], axis=-1) @ w_out
    gu = _rms_norm(x, g_ffn) @ w_up
    return x + (jax.nn.silu(gu[..., :FFN_HIDDEN]) * gu[..., FFN_HIDDEN:]) @ w_down


def setup_inputs(seed: int = 0) -> dict:
    key = jax.random.key(seed)
    ks = jax.random.split(key, 24)
    n_pages = PAST_LEN // PAGE_SIZE
    n_used = DEC_BATCH * n_pages
    n_pool = n_used + max(1, n_used // 4)
    win_buf = min(WINDOW, PAST_LEN)

    def nrm(k, shape, scale):
        return jax.random.normal(k, shape, jnp.float32) * scale

    hd = NSA_HEAD_DIM
    return {
        'x_prompt': nrm(ks[0], (BATCH, SEQ, D_MODEL), 1.0),
        'x_sample': nrm(ks[1], (DEC_BATCH, DEC_SEQ, D_MODEL), 1.0),
        'cache_nsa_kv': nrm(ks[2], (DEPTH, n_pool, PAGE_SIZE, 4, NSA_KV_HEADS, hd), 1.0),
        'cache_win_kv': nrm(ks[3], (DEPTH, DEC_BATCH, win_buf, 2, NSA_KV_HEADS, hd), 1.0),
        'state_gla': nrm(ks[4], (DEPTH, DEC_BATCH, GLA_HEADS, GLA_DK, GLA_DV), 1.0),
        'page_table': jax.random.permutation(ks[5], n_pool)[:n_used].reshape(DEC_BATCH, n_pages).astype(jnp.int32),
        'w_norm_mix': 1.0 + nrm(ks[6], (DEPTH, D_MODEL), 0.01),
        'w_in': nrm(ks[7], (DEPTH, D_MODEL, IN_WIDTH), D_MODEL ** -0.5),
        'nsa_q_gain': 1.0 + nrm(ks[8], (DEPTH, hd), 0.01),
        'nsa_k_gain': 1.0 + nrm(ks[9], (DEPTH, 3, hd), 0.01),
        'cmp_pos_k': nrm(ks[10], (DEPTH, CMP_BLOCK, hd), 0.1),
        'cmp_w1_k': nrm(ks[11], (DEPTH, CMP_BLOCK, hd, CMP_HIDDEN), (CMP_BLOCK * hd) ** -0.5),
        'cmp_w2_k': nrm(ks[12], (DEPTH, CMP_HIDDEN, hd), CMP_HIDDEN ** -0.5),
        'cmp_pos_v': nrm(ks[13], (DEPTH, CMP_BLOCK, hd), 0.1),
        'cmp_w1_v': nrm(ks[14], (DEPTH, CMP_BLOCK, hd, CMP_HIDDEN), (CMP_BLOCK * hd) ** -0.5),
        'cmp_w2_v': nrm(ks[15], (DEPTH, CMP_HIDDEN, hd), CMP_HIDDEN ** -0.5),
        'gla_w_a2': nrm(ks[16], (DEPTH, GLA_GATE_RANK, GLA_HEADS * GLA_DK), GLA_GATE_RANK ** -0.5),
        'gla_b_a': nrm(ks[17], (DEPTH, GLA_HEADS * GLA_DK), 0.1),
        'gla_norm_gain': 1.0 + nrm(ks[18], (DEPTH, GLA_DV), 0.01),
        'w_out': nrm(ks[19], (DEPTH, MIX_WIDTH, D_MODEL), MIX_WIDTH ** -0.5),
        'w_norm_ffn': 1.0 + nrm(ks[20], (DEPTH, D_MODEL), 0.01),
        'w_ffn_up': nrm(ks[21], (DEPTH, D_MODEL, 2 * FFN_HIDDEN), D_MODEL ** -0.5),
        'w_ffn_down': nrm(ks[22], (DEPTH, FFN_HIDDEN, D_MODEL), FFN_HIDDEN ** -0.5),
    }


def reference(x_prompt, x_sample, cache_nsa_kv, cache_win_kv, state_gla, page_table,
              w_norm_mix, w_in, nsa_q_gain, nsa_k_gain, cmp_pos_k, cmp_w1_k, cmp_w2_k,
              cmp_pos_v, cmp_w1_v, cmp_w2_v, gla_w_a2, gla_b_a, gla_norm_gain, w_out,
              w_norm_ffn, w_ffn_up, w_ffn_down):
    b_p, t_p = x_prompt.shape[:2]
    past_len = page_table.shape[1] * PAGE_SIZE
    pos_p = jnp.arange(t_p)
    pos_s = past_len + jnp.arange(x_sample.shape[1])
    xp, xs = x_prompt, x_sample
    nsa_p, win_p, gla_p, nsa_s, win_s, gla_s = [], [], [], [], [], []
    for l in range(DEPTH):
        cmp_k = (cmp_pos_k[l], cmp_w1_k[l], cmp_w2_k[l])
        cmp_v = (cmp_pos_v[l], cmp_w1_v[l], cmp_w2_v[l])
        parts = _split_in(_rms_norm(xp, w_norm_mix[l]) @ w_in[l])
        q, q_rot, rows, win_rows, gates = _nsa_project(parts[0], parts[1], parts[2], pos_p, nsa_q_gain[l], nsa_k_gain[l])
        o_nsa = _nsa_prompt(q, q_rot, rows, win_rows, gates, pos_p, nsa_k_gain[l], cmp_k, cmp_v)
        s0 = jnp.zeros((b_p, GLA_HEADS, GLA_DK, GLA_DV), state_gla.dtype)
        o_gla, s_fin = _gla(parts[3], parts[4], parts[5], parts[6], parts[7], s0, gla_w_a2[l], gla_b_a[l], gla_norm_gain[l])
        xp = _residual_out(xp, o_nsa, o_gla, w_out[l], w_norm_ffn[l], w_ffn_up[l], w_ffn_down[l])
        nsa_p.append(rows)
        win_p.append(win_rows[:, -min(WINDOW, t_p):])
        gla_p.append(s_fin)
        parts = _split_in(_rms_norm(xs, w_norm_mix[l]) @ w_in[l])
        q, q_rot, rows, win_rows, gates = _nsa_project(parts[0], parts[1], parts[2], pos_s, nsa_q_gain[l], nsa_k_gain[l])
        o_nsa, new_win = _nsa_sample(q, q_rot, rows, win_rows, gates, pos_s, nsa_k_gain[l], cmp_k, cmp_v,
                                     cache_nsa_kv, l, cache_win_kv[l], page_table)
        o_gla, s_fin = _gla(parts[3], parts[4], parts[5], parts[6], parts[7], state_gla[l], gla_w_a2[l], gla_b_a[l], gla_norm_gain[l])
        xs = _residual_out(xs, o_nsa, o_gla, w_out[l], w_norm_ffn[l], w_ffn_up[l], w_ffn_down[l])
        nsa_s.append(rows)
        win_s.append(new_win)
        gla_s.append(s_fin)
    return (xp, xs, jnp.stack(nsa_p), jnp.stack(win_p), jnp.stack(gla_p), jnp.stack(nsa_s), jnp.stack(win_s), jnp.stack(gla_s))
```

```python
import functools
import math

import numpy as np
import jax
import jax.numpy as jnp
from jax import lax
from jax.experimental import pallas as pl
from jax.experimental.pallas import tpu as pltpu

F32 = jnp.float32
BF16 = jnp.bfloat16

D_MODEL = 1024
PAGE_SIZE = 128
NSA_HEADS = 8
NSA_KV_HEADS = 2
NSA_GROUP = NSA_HEADS // NSA_KV_HEADS
HEAD_DIM = 64
NSA_WIDTH = NSA_HEADS * HEAD_DIM
KV_WIDTH = NSA_KV_HEADS * HEAD_DIM
CMP_BLOCK = 32
CMP_STRIDE = 16
CMP_HIDDEN = 2 * HEAD_DIM
SEL_BLOCK = 64
SEL_TOPK = 16
WINDOW = 512
FORCE_SCORE = 1.0e4
GLA_HEADS = 4
GLA_DK = 64
GLA_DV = 128
GLA_WIDTH = GLA_HEADS * GLA_DV
GLA_QK_WIDTH = GLA_HEADS * GLA_DK
GLA_GATE_RANK = 16
GLA_GATE_TAU = 16.0
FFN_HIDDEN = -(-8 * D_MODEL // (3 * 256)) * 256
ROPE_THETA = 10000.0
EPS = 1e-6
NEG = -1.0e30

LANES = 128
MISC_WIDTH = LANES
GATE_COLS = 3 * NSA_HEADS
PROJ_WIDTH = NSA_WIDTH + 6 * KV_WIDTH + 2 * GLA_QK_WIDTH + 2 * GLA_WIDTH + MISC_WIDTH
ROW_TILE = 512
OUT_ROW_TILE = 1024
Q_TILE = 128
SEL_KEY_TILE = 512
GLA_TILE = 128
PAGES_PER_STEP = 16
FFN_CHUNK = 256
VMEM_LIMIT = 56 * 1024 * 1024


def _dot(a, b):
    return jnp.dot(a, b, preferred_element_type=F32)


def _dot_nt(a, b):
    return lax.dot_general(a, b, (((1,), (1,)), ((), ())), preferred_element_type=F32)


def _split(x):
    hi = x.astype(BF16)
    lo = (x - hi.astype(F32)).astype(BF16)
    return hi, lo


def _dot2(x, w_bf16):
    hi, lo = _split(x)
    return _dot(hi, w_bf16) + _dot(lo, w_bf16)


def _silu(x):
    return x * (1.0 / (1.0 + jnp.exp(-x)))


def _head_norm(v, ones_bd, gain):
    ss = _dot((v * v).astype(BF16), ones_bd)
    return v * lax.rsqrt(ss * (1.0 / HEAD_DIM) + EPS) * gain


def _rope(v, cos, sin_signed):
    width = v.shape[-1]
    reps = width // LANES
    if reps > 1:
        cos = jnp.concatenate([cos] * reps, axis=1)
        sin_signed = jnp.concatenate([sin_signed] * reps, axis=1)
    lane = lax.broadcasted_iota(jnp.int32, v.shape, 1)
    first_half = (lane & (HEAD_DIM - 1)) < (HEAD_DIM // 2)
    rot = jnp.where(first_half, pltpu.roll(v, width - HEAD_DIM // 2, 1), pltpu.roll(v, HEAD_DIM // 2, 1))
    return v * cos + rot * sin_signed


def _softmax_rows(s):
    m = jnp.max(s, axis=-1, keepdims=True)
    m = jnp.where(m == -jnp.inf, 0.0, m)
    e = jnp.exp(s - m)
    return e / jnp.maximum(jnp.sum(e, axis=-1, keepdims=True), 1e-30)


def _proj_body(x_ref, gmix_ref, w_ref, cos_ref, sin_ref, qg_ref, kg1_ref, kg2_ref, ones_ref,
               qn_ref, qr_ref, rows_ref, win_ref, kvb_ref, gq_ref, gk_ref, gv_ref, gg_ref, misc_ref):
    x = x_ref[...]
    ms = jnp.mean(x * x, axis=-1, keepdims=True)
    h = (x * lax.rsqrt(ms + EPS) * gmix_ref[...]).astype(BF16)
    cos = cos_ref[...]
    sin = sin_ref[...]
    ones512 = ones_ref[...]
    ones128 = ones512[:LANES, :LANES]
    scale = HEAD_DIM ** -0.5

    off = 0
    q = _dot(h, w_ref[:, off:off + NSA_WIDTH])
    off += NSA_WIDTH
    qn = _head_norm(q, ones512, qg_ref[...])
    qn_ref[...] = (qn * scale).astype(BF16)
    qr_ref[...] = (_rope(qn, cos, sin) * scale).astype(BF16)

    kv = _dot(h, w_ref[:, off:off + 6 * KV_WIDTH])
    off += 6 * KV_WIDTH
    k_slc = _rope(_head_norm(kv[:, 2 * KV_WIDTH:3 * KV_WIDTH], ones128, kg1_ref[...]), cos, sin)
    k_win = _rope(_head_norm(kv[:, 4 * KV_WIDTH:5 * KV_WIDTH], ones128, kg2_ref[...]), cos, sin)
    v_slc = kv[:, 3 * KV_WIDTH:4 * KV_WIDTH]
    v_win = kv[:, 5 * KV_WIDTH:6 * KV_WIDTH]
    rows_ref[:, 0:2 * KV_WIDTH] = kv[:, 0:2 * KV_WIDTH]
    rows_ref[:, 2 * KV_WIDTH:3 * KV_WIDTH] = k_slc
    rows_ref[:, 3 * KV_WIDTH:4 * KV_WIDTH] = v_slc
    win_ref[:, 0:KV_WIDTH] = k_win
    win_ref[:, KV_WIDTH:2 * KV_WIDTH] = v_win
    kvb_ref[:, 0:KV_WIDTH] = k_slc.astype(BF16)
    kvb_ref[:, KV_WIDTH:2 * KV_WIDTH] = v_slc.astype(BF16)
    kvb_ref[:, 2 * KV_WIDTH:3 * KV_WIDTH] = k_win.astype(BF16)
    kvb_ref[:, 3 * KV_WIDTH:4 * KV_WIDTH] = v_win.astype(BF16)

    gq_ref[...] = _dot(h, w_ref[:, off:off + GLA_QK_WIDTH])
    off += GLA_QK_WIDTH
    gk_ref[...] = _dot(h, w_ref[:, off:off + GLA_QK_WIDTH])
    off += GLA_QK_WIDTH
    gv_ref[...] = _dot(h, w_ref[:, off:off + GLA_WIDTH])
    off += GLA_WIDTH
    gg_ref[...] = _dot(h, w_ref[:, off:off + GLA_WIDTH])
    off += GLA_WIDTH
    misc_ref[...] = _dot(h, w_ref[:, off:off + MISC_WIDTH])


def _proj(x2d, gmix, w_perm, cos_t, sin_t, qg, kg1, kg2, ones512, table_blocks):
    n = x2d.shape[0]
    tm = ROW_TILE
    row = lambda i: (i, 0)
    const = lambda i: (0, 0)
    tab = lambda i: (i % table_blocks, 0)
    widths = [(NSA_WIDTH, BF16), (NSA_WIDTH, BF16), (4 * KV_WIDTH, F32), (2 * KV_WIDTH, F32), (4 * KV_WIDTH, BF16),
              (GLA_QK_WIDTH, F32), (GLA_QK_WIDTH, F32), (GLA_WIDTH, F32), (GLA_WIDTH, F32), (MISC_WIDTH, F32)]
    return pl.pallas_call(
        _proj_body,
        grid=(n // tm,),
        in_specs=[
            pl.BlockSpec((tm, D_MODEL), row),
            pl.BlockSpec((1, D_MODEL), const),
            pl.BlockSpec((D_MODEL, PROJ_WIDTH), const),
            pl.BlockSpec((tm, LANES), tab),
            pl.BlockSpec((tm, LANES), tab),
            pl.BlockSpec((1, NSA_WIDTH), const),
            pl.BlockSpec((1, KV_WIDTH), const),
            pl.BlockSpec((1, KV_WIDTH), const),
            pl.BlockSpec((NSA_WIDTH, NSA_WIDTH), const),
        ],
        out_specs=[pl.BlockSpec((tm, w), row) for w, _ in widths],
        out_shape=[jax.ShapeDtypeStruct((n, w), dt) for w, dt in widths],
        compiler_params=pltpu.CompilerParams(dimension_semantics=("parallel",), vmem_limit_bytes=VMEM_LIMIT),
        name="proj",
    )(x2d, gmix, w_perm, cos_t, sin_t, qg, kg1, kg2, ones512)


def _cmp1_body(*refs, n_src, n_prefetch):
    refs = refs[n_prefetch:]
    k_src = refs[:n_src]
    v_src = refs[n_src:2 * n_src]
    wk_ref, wv_ref, out_ref = refs[2 * n_src:]
    lead = (0,) * (len(k_src[0].shape) - 2)

    def rows_at(src, s):
        parts = [r[lead + (pl.ds(s, r.shape[-2] // CMP_STRIDE, stride=CMP_STRIDE), slice(None))] for r in src]
        return parts[0] if len(parts) == 1 else jnp.concatenate(parts, axis=0)

    acc_k = None
    acc_v = None
    for sp in range(CMP_STRIDE // 2):
        lk = jnp.concatenate([rows_at(k_src, 2 * sp), rows_at(k_src, 2 * sp + 1)], axis=1).astype(BF16)
        lv = jnp.concatenate([rows_at(v_src, 2 * sp), rows_at(v_src, 2 * sp + 1)], axis=1).astype(BF16)
        dk = _dot(lk, wk_ref[sp])
        dv = _dot(lv, wv_ref[sp])
        acc_k = dk if acc_k is None else acc_k + dk
        acc_v = dv if acc_v is None else acc_v + dv
    hid = 4 * CMP_HIDDEN
    out_ref[0, :, 0:hid] = acc_k
    out_ref[0, :, hid:2 * hid] = acc_v


def _cmp1_weights(w1):
    w = w1.reshape(2, CMP_STRIDE // 2, 2, HEAD_DIM, CMP_HIDDEN)
    eye = jnp.eye(NSA_KV_HEADS, dtype=w1.dtype)
    big = jnp.einsum("jpsde,hg->pshdgje", w, eye)
    return big.reshape(CMP_STRIDE // 2, 4 * HEAD_DIM, 4 * CMP_HIDDEN).astype(BF16)


def _cmp1_prompt(rows3d, wk, wv):
    b, t, _ = rows3d.shape
    chunks = t // CMP_STRIDE
    wspec = pl.BlockSpec(wk.shape, lambda i: (0, 0, 0))
    return pl.pallas_call(
        functools.partial(_cmp1_body, n_src=1, n_prefetch=0),
        grid=(b,),
        in_specs=[pl.BlockSpec((1, t, KV_WIDTH), lambda i: (i, 0, 0)),
                  pl.BlockSpec((1, t, KV_WIDTH), lambda i: (i, 0, 1)), wspec, wspec],
        out_specs=pl.BlockSpec((1, chunks, 8 * CMP_HIDDEN), lambda i: (i, 0, 0)),
        out_shape=jax.ShapeDtypeStruct((b, chunks, 8 * CMP_HIDDEN), F32),
        compiler_params=pltpu.CompilerParams(dimension_semantics=("parallel",), vmem_limit_bytes=VMEM_LIMIT),
        name="cmp1_prompt",
    )(rows3d, rows3d, wk, wv)


def _cmp1_sample(cache3d, page_table, wk, wv):
    bs, n_pages = page_table.shape
    steps = n_pages // PAGES_PER_STEP
    chunks_per_step = PAGES_PER_STEP * PAGE_SIZE // CMP_STRIDE

    def page_spec(i, kind):
        return pl.BlockSpec((None, PAGE_SIZE, KV_WIDTH),
                            lambda b, g, pt, i=i: (pt[b, g * PAGES_PER_STEP + i], 0, kind))

    wspec = pl.BlockSpec(wk.shape, lambda b, g, pt: (0, 0, 0))
    grid_spec = pltpu.PrefetchScalarGridSpec(
        num_scalar_prefetch=1,
        grid=(bs, steps),
        in_specs=[page_spec(i, kind) for kind in range(2) for i in range(PAGES_PER_STEP)] + [wspec, wspec],
        out_specs=pl.BlockSpec((1, chunks_per_step, 8 * CMP_HIDDEN), lambda b, g, pt: (b, g, 0)),
    )
    return pl.pallas_call(
        functools.partial(_cmp1_body, n_src=PAGES_PER_STEP, n_prefetch=1),
        grid_spec=grid_spec,
        out_shape=jax.ShapeDtypeStruct((bs, steps * chunks_per_step, 8 * CMP_HIDDEN), F32),
        compiler_params=pltpu.CompilerParams(dimension_semantics=("parallel", "parallel"),
                                             vmem_limit_bytes=VMEM_LIMIT),
        name="cmp1_sample",
    )(page_table, *([cache3d] * (2 * PAGES_PER_STEP)), wk, wv)


def _cmp2_body(g_ref, posk_ref, w1k_ref, posv_ref, w1v_ref, w2k_ref, w2v_ref, kg0_ref, ones_ref, kc_ref, vc_ref):
    g = g_ref[0]
    chunks = g.shape[0]
    hd = CMP_HIDDEN

    def pos_term(pos_ref, w1_ref):
        p = jnp.broadcast_to(pos_ref[...], (8, pos_ref.shape[1]))
        p_hi, p_lo = _split(p)
        w_hi, w_lo = _split(w1_ref[...])
        return (_dot(p_hi, w_hi) + _dot(p_hi, w_lo) + _dot(p_lo, w_hi))[0:1, :]

    def hidden(base, pos):
        out = []
        for h in range(NSA_KV_HEADS):
            j0 = g[:, base + 2 * h * hd: base + (2 * h + 1) * hd]
            j1 = g[:, base + (2 * h + 1) * hd: base + (2 * h + 2) * hd]
            out.append(j0 + pltpu.roll(j1, chunks - 1, 0) + pos)
        return _silu(jnp.concatenate(out, axis=1)).astype(BF16)

    kc = _dot(hidden(0, pos_term(posk_ref, w1k_ref)), w2k_ref[...])
    kc_ref[0] = _head_norm(kc, ones_ref[...], kg0_ref[...])
    vc_ref[0] = _dot(hidden(4 * hd, pos_term(posv_ref, w1v_ref)), w2v_ref[...])


def _cmp2(g, posk, w1k, posv, w1v, w2k_bd, w2v_bd, kg0, ones128):
    b, chunks, _ = g.shape
    c2 = lambda i: (0, 0)
    seq = lambda i: (i, 0, 0)
    flat = CMP_BLOCK * HEAD_DIM
    return pl.pallas_call(
        _cmp2_body,
        grid=(b,),
        in_specs=[
            pl.BlockSpec((1, chunks, 8 * CMP_HIDDEN), seq),
            pl.BlockSpec((1, flat), c2), pl.BlockSpec((flat, CMP_HIDDEN), c2),
            pl.BlockSpec((1, flat), c2), pl.BlockSpec((flat, CMP_HIDDEN), c2),
            pl.BlockSpec((2 * CMP_HIDDEN, KV_WIDTH), c2), pl.BlockSpec((2 * CMP_HIDDEN, KV_WIDTH), c2),
            pl.BlockSpec((1, KV_WIDTH), c2), pl.BlockSpec((LANES, LANES), c2),
        ],
        out_specs=[pl.BlockSpec((1, chunks, KV_WIDTH), seq)] * 2,
        out_shape=[jax.ShapeDtypeStruct((b, chunks, KV_WIDTH), F32)] * 2,
        compiler_params=pltpu.CompilerParams(dimension_semantics=("parallel",), vmem_limit_bytes=VMEM_LIMIT),
        name="cmp2",
    )(g, posk, w1k, posv, w1v, w2k_bd, w2v_bd, kg0, ones128)


def _stack_heads(q, kvh):
    base = kvh * NSA_GROUP * HEAD_DIM
    return jnp.concatenate([q[:, base + g * HEAD_DIM: base + (g + 1) * HEAD_DIM] for g in range(NSA_GROUP)], axis=0)


def _unstack_heads(o, rows):
    return jnp.concatenate([o[g * rows:(g + 1) * rows] for g in range(NSA_GROUP)], axis=1)


def _group_sum(p, rows):
    out = p[0:rows]
    for g in range(1, NSA_GROUP):
        out = out + p[g * rows:(g + 1) * rows]
    return out


def _select_blocks(imp, qpos, n_blocks):
    blk = lax.broadcasted_iota(jnp.int32, imp.shape, 1)
    visible = blk * SEL_BLOCK <= qpos
    forced = jnp.logical_or(blk == 0, blk == (qpos >> int(math.log2(SEL_BLOCK))))
    score = jnp.where(forced, FORCE_SCORE, jnp.where(visible, imp, -jnp.inf))
    beaten = jnp.zeros(imp.shape, F32)
    for i in range(n_blocks):
        ci = score[:, i:i + 1]
        earlier = jnp.where(blk > i, 1.0, 0.0)
        beaten = beaten + jnp.where(ci > score, 1.0, jnp.where(ci == score, earlier, 0.0))
    return jnp.where(beaten < float(SEL_TOPK), jnp.where(score > -jnp.inf, 1.0, 0.0), 0.0)


def _nsa_prompt_body(qn_ref, qr_ref, kc_ref, vc_ref, kvb_ref, ov_ref, ex_ref, oc_ref, os_ref, ow_ref, *, seq_len):
    tq = Q_TILE
    tk = SEL_KEY_TILE
    q0 = pl.program_id(1) * tq
    n_cmp = seq_len // CMP_STRIDE - 1
    n_sel = seq_len // SEL_BLOCK
    rows4 = NSA_GROUP * tq
    qn = qn_ref[0]
    qr = qr_ref[0]
    kc = kc_ref[0].astype(BF16)
    vc = vc_ref[0].astype(BF16)
    qpos1 = q0 + lax.broadcasted_iota(jnp.int32, (tq, 1), 0)
    qpos4 = jnp.concatenate([qpos1] * NSA_GROUP, axis=0)
    win_keys = WINDOW + tq
    wstart = pl.multiple_of(jnp.maximum(q0 - WINDOW, 0), tq)
    n_tiles = (q0 + tq + tk - 1) // tk

    oc, osel, ow = [], [], []
    for kvh in range(NSA_KV_HEADS):
        lo, hi = kvh * HEAD_DIM, (kvh + 1) * HEAD_DIM
        qc = _stack_heads(qn, kvh)
        s = _dot_nt(qc, kc[:, lo:hi])
        n_idx = lax.broadcasted_iota(jnp.int32, s.shape, 1)
        vis = jnp.logical_and(n_idx * CMP_STRIDE + (CMP_BLOCK - 1) <= qpos4, n_idx < n_cmp)
        p = _softmax_rows(jnp.where(vis, s, -jnp.inf))
        oc.append(_unstack_heads(_dot(p.astype(BF16), vc[:, lo:hi]), tq))
        imp = _dot2(_group_sum(p, tq), ov_ref[...])
        sel = _select_blocks(imp, qpos1, n_sel).astype(BF16)

        qs = _stack_heads(qr, kvh)

        def sel_step(kt, carry, qs=qs, sel=sel, lo=lo, hi=hi):
            m_i, l_i, acc = carry
            k0 = pl.multiple_of(kt * tk, tk)
            k_t = kvb_ref[0, pl.ds(k0, tk), lo:hi]
            v_t = kvb_ref[0, pl.ds(k0, tk), KV_WIDTH + lo:KV_WIDTH + hi]
            sc = _dot_nt(qs, k_t).reshape(NSA_GROUP, tq, tk)
            chosen = _dot(sel, ex_ref[kt])
            kpos = k0 + lax.broadcasted_iota(jnp.int32, (tq, tk), 1)
            keep = jnp.logical_and(chosen > 0.5, kpos <= qpos1)
            sc = jnp.where(keep[None], sc, -jnp.inf).reshape(rows4, tk)
            m_new = jnp.maximum(m_i, jnp.max(sc, axis=-1, keepdims=True))
            alpha = jnp.exp(m_i - m_new)
            pe = jnp.exp(sc - m_new)
            l_new = alpha * l_i + jnp.sum(pe, axis=-1, keepdims=True)
            acc_new = alpha * acc + _dot(pe.astype(BF16), v_t)
            return m_new, l_new, acc_new

        init = (jnp.full((rows4, 1), NEG, F32), jnp.zeros((rows4, 1), F32), jnp.zeros((rows4, HEAD_DIM), F32))
        _, l_f, acc_f = lax.fori_loop(0, n_tiles, sel_step, init)
        osel.append(_unstack_heads(acc_f / jnp.maximum(l_f, 1e-30), tq))

        k_w = kvb_ref[0, pl.ds(wstart, win_keys), 2 * KV_WIDTH + lo:2 * KV_WIDTH + hi]
        v_w = kvb_ref[0, pl.ds(wstart, win_keys), 3 * KV_WIDTH + lo:3 * KV_WIDTH + hi]
        sw = _dot_nt(qs, k_w)
        rel = qpos4 - (wstart + lax.broadcasted_iota(jnp.int32, sw.shape, 1))
        inside = jnp.logical_and(rel >= 0, rel < WINDOW)
        pw = _softmax_rows(jnp.where(inside, sw, -jnp.inf))
        ow.append(_unstack_heads(_dot(pw.astype(BF16), v_w), tq))

    oc_ref[0] = jnp.concatenate(oc, axis=1)
    os_ref[0] = jnp.concatenate(osel, axis=1)
    ow_ref[0] = jnp.concatenate(ow, axis=1)


def _nsa_prompt(qn, qr, kc, vc, kvb, ov, ex):
    b, t, _ = qn.shape
    chunks = kc.shape[1]
    tile = lambda i, j: (i, j, 0)
    seq = lambda i, j: (i, 0, 0)
    out = jax.ShapeDtypeStruct((b, t, NSA_WIDTH), F32)
    return pl.pallas_call(
        functools.partial(_nsa_prompt_body, seq_len=t),
        grid=(b, t // Q_TILE),
        in_specs=[
            pl.BlockSpec((1, Q_TILE, NSA_WIDTH), tile),
            pl.BlockSpec((1, Q_TILE, NSA_WIDTH), tile),
            pl.BlockSpec((1, chunks, KV_WIDTH), seq),
            pl.BlockSpec((1, chunks, KV_WIDTH), seq),
            pl.BlockSpec((1, t, 4 * KV_WIDTH), seq),
            pl.BlockSpec(ov.shape, lambda i, j: (0, 0)),
            pl.BlockSpec(ex.shape, lambda i, j: (0, 0, 0)),
        ],
        out_specs=[pl.BlockSpec((1, Q_TILE, NSA_WIDTH), tile)] * 3,
        out_shape=[out] * 3,
        compiler_params=pltpu.CompilerParams(dimension_semantics=("parallel", "parallel"),
                                             vmem_limit_bytes=VMEM_LIMIT),
        name="nsa_prompt",
    )(qn, qr, kc, vc, kvb, ov, ex)


def _nsa_sample_cmp_body(qn_ref, kc_ref, vc_ref, ov_ref, oc_ref, sel_ref, *, past_len, n_new):
    t = n_new
    n_chunks = kc_ref.shape[1]
    n_cmp = (past_len + t) // CMP_STRIDE - 1
    n_sel = -(-(past_len + t) // SEL_BLOCK)
    qn = qn_ref[0]
    kc = kc_ref[0].astype(BF16)
    vc = vc_ref[0].astype(BF16)
    qpos1 = past_len + lax.broadcasted_iota(jnp.int32, (t, 1), 0)
    qpos4 = jnp.concatenate([qpos1] * NSA_GROUP, axis=0)
    oc, sel = [], []
    for kvh in range(NSA_KV_HEADS):
        lo, hi = kvh * HEAD_DIM, (kvh + 1) * HEAD_DIM
        s = _dot_nt(_stack_heads(qn, kvh), kc[:, lo:hi])
        n_idx = lax.broadcasted_iota(jnp.int32, s.shape, 1)
        vis = jnp.logical_and(n_idx * CMP_STRIDE + (CMP_BLOCK - 1) <= qpos4, n_idx < n_cmp)
        p = _softmax_rows(jnp.where(vis, s, -jnp.inf))
        oc.append(_unstack_heads(_dot(p.astype(BF16), vc[:, lo:hi]), t))
        imp = _dot2(_group_sum(p, t), ov_ref[...])
        sel.append(_select_blocks(imp, qpos1, n_sel))
    oc_ref[0] = jnp.concatenate(oc, axis=1)
    sel_ref[0] = jnp.concatenate(sel, axis=0)
    del n_chunks


def _nsa_sample_cmp(qn, kc, vc, ov, past_len):
    bs, t, _ = qn.shape
    chunks = kc.shape[1]
    seq = lambda i: (i, 0, 0)
    return pl.pallas_call(
        functools.partial(_nsa_sample_cmp_body, past_len=past_len, n_new=t),
        grid=(bs,),
        in_specs=[
            pl.BlockSpec((1, t, NSA_WIDTH), seq),
            pl.BlockSpec((1, chunks, KV_WIDTH), seq),
            pl.BlockSpec((1, chunks, KV_WIDTH), seq),
            pl.BlockSpec(ov.shape, lambda i: (0, 0)),
        ],
        out_specs=[pl.BlockSpec((1, t, NSA_WIDTH), seq), pl.BlockSpec((1, NSA_KV_HEADS * t, ov.shape[1]), seq)],
        out_shape=[jax.ShapeDtypeStruct((bs, t, NSA_WIDTH), F32),
                   jax.ShapeDtypeStruct((bs, NSA_KV_HEADS * t, ov.shape[1]), F32)],
        compiler_params=pltpu.CompilerParams(dimension_semantics=("parallel",), vmem_limit_bytes=VMEM_LIMIT),
        name="nsa_sample_cmp",
    )(qn, kc, vc, ov)


def _nsa_sample_sel_body(*refs, n_new):
    pt_ref = refs[0]
    pages = refs[1:1 + PAGES_PER_STEP]
    qr_ref, selm_ref, kvb_ref, ex_ref, os_ref, m_sc, l_sc, acc_sc = refs[1 + PAGES_PER_STEP:]
    del pt_ref
    t = n_new
    rows4 = NSA_GROUP * t
    step = pl.program_id(1)
    last = pl.num_programs(1) - 1
    qr = qr_ref[0]

    @pl.when(step == 0)
    def _():
        m_sc[...] = jnp.full(m_sc.shape, NEG, F32)
        l_sc[...] = jnp.zeros(l_sc.shape, F32)
        acc_sc[...] = jnp.zeros(acc_sc.shape, F32)

    def update(kvh, sc, v_blk):
        m_i, l_i, acc = m_sc[kvh], l_sc[kvh], acc_sc[kvh]
        m_new = jnp.maximum(m_i, jnp.max(sc, axis=-1, keepdims=True))
        alpha = jnp.exp(m_i - m_new)
        pe = jnp.exp(sc - m_new)
        m_sc[kvh] = m_new
        l_sc[kvh] = alpha * l_i + jnp.sum(pe, axis=-1, keepdims=True)
        acc_sc[kvh] = alpha * acc + _dot(pe.astype(BF16), v_blk)

    for kvh in range(NSA_KV_HEADS):
        lo, hi = kvh * HEAD_DIM, (kvh + 1) * HEAD_DIM
        qs = _stack_heads(qr, kvh)
        k_blk = jnp.concatenate([p[:, lo:hi] for p in pages], axis=0).astype(BF16)
        v_blk = jnp.concatenate([p[:, KV_WIDTH + lo:KV_WIDTH + hi] for p in pages], axis=0).astype(BF16)
        chosen = _dot(selm_ref[0, 0, kvh * rows4:(kvh + 1) * rows4, :].astype(BF16), ex_ref[0])
        sc = jnp.where(chosen > 0.5, _dot_nt(qs, k_blk), -jnp.inf)
        update(kvh, sc, v_blk)

    @pl.when(step == last)
    def _():
        pad = jnp.zeros((LANES - t, HEAD_DIM), F32)
        out = []
        for kvh in range(NSA_KV_HEADS):
            lo, hi = kvh * HEAD_DIM, (kvh + 1) * HEAD_DIM
            qs = _stack_heads(qr, kvh)
            k_new = jnp.concatenate([kvb_ref[0, :, lo:hi].astype(F32), pad], axis=0).astype(BF16)
            v_new = jnp.concatenate([kvb_ref[0, :, KV_WIDTH + lo:KV_WIDTH + hi].astype(F32), pad], axis=0).astype(BF16)
            sc = _dot_nt(qs, k_new)
            key_i = lax.broadcasted_iota(jnp.int32, sc.shape, 1)
            q_t = lax.broadcasted_iota(jnp.int32, sc.shape, 0) & (t - 1)
            update(kvh, jnp.where(key_i <= q_t, sc, -jnp.inf), v_new)
            out.append(_unstack_heads(acc_sc[kvh] / jnp.maximum(l_sc[kvh], 1e-30), t))
        os_ref[0] = jnp.concatenate(out, axis=1)


def _nsa_sample_sel(cache3d, page_table, qr, selm, kvb, ex):
    bs, n_pages = page_table.shape
    t = qr.shape[1]
    steps = n_pages // PAGES_PER_STEP
    rows = NSA_KV_HEADS * NSA_GROUP * t

    def page_spec(i):
        return pl.BlockSpec((None, PAGE_SIZE, 2 * KV_WIDTH),
                            lambda b, g, pt, i=i: (pt[b, g * PAGES_PER_STEP + i], 0, 1))

    seq = lambda b, g, pt: (b, 0, 0)
    grid_spec = pltpu.PrefetchScalarGridSpec(
        num_scalar_prefetch=1,
        grid=(bs, steps),
        in_specs=[page_spec(i) for i in range(PAGES_PER_STEP)] + [
            pl.BlockSpec((1, t, NSA_WIDTH), seq),
            pl.BlockSpec((1, 1, rows, LANES), lambda b, g, pt: (b, g, 0, 0)),
            pl.BlockSpec((1, t, 4 * KV_WIDTH), seq),
            pl.BlockSpec((1,) + ex.shape[1:], lambda b, g, pt: (0, 0, 0)),
        ],
        out_specs=pl.BlockSpec((1, t, NSA_WIDTH), seq),
        scratch_shapes=[pltpu.VMEM((NSA_KV_HEADS, NSA_GROUP * t, 1), F32),
                        pltpu.VMEM((NSA_KV_HEADS, NSA_GROUP * t, 1), F32),
                        pltpu.VMEM((NSA_KV_HEADS, NSA_GROUP * t, HEAD_DIM), F32)],
    )
    return pl.pallas_call(
        functools.partial(_nsa_sample_sel_body, n_new=t),
        grid_spec=grid_spec,
        out_shape=jax.ShapeDtypeStruct((bs, t, NSA_WIDTH), F32),
        compiler_params=pltpu.CompilerParams(dimension_semantics=("parallel", "arbitrary"),
                                             vmem_limit_bytes=VMEM_LIMIT),
        name="nsa_sample_sel",
    )(page_table, *([cache3d] * PAGES_PER_STEP), qr, selm, kvb, ex)


def _nsa_sample_win_body(qr_ref, cw_ref, nw_ref, ow_ref, neww_ref, *, n_new):
    t = n_new
    wb = cw_ref.shape[1]
    qr = qr_ref[0]
    cw = cw_ref[0]
    nw = nw_ref[0]
    neww_ref[0, 0:wb - t, :] = cw[t:wb, :]
    neww_ref[0, wb - t:wb, :] = nw
    keys = jnp.concatenate([cw, nw, jnp.zeros((LANES - t, 2 * KV_WIDTH), F32)], axis=0).astype(BF16)
    out = []
    for kvh in range(NSA_KV_HEADS):
        lo, hi = kvh * HEAD_DIM, (kvh + 1) * HEAD_DIM
        sw = _dot_nt(_stack_heads(qr, kvh), keys[:, lo:hi])
        key_i = lax.broadcasted_iota(jnp.int32, sw.shape, 1)
        q_t = lax.broadcasted_iota(jnp.int32, sw.shape, 0) & (t - 1)
        rel = wb + q_t - key_i
        inside = jnp.logical_and(rel >= 0, rel < WINDOW)
        pw = _softmax_rows(jnp.where(inside, sw, -jnp.inf))
        out.append(_unstack_heads(_dot(pw.astype(BF16), keys[:, KV_WIDTH + lo:KV_WIDTH + hi]), t))
    ow_ref[0] = jnp.concatenate(out, axis=1)


def _nsa_sample_win(qr, cache_win3d, new_win):
    bs, t, _ = qr.shape
    wb = cache_win3d.shape[1]
    seq = lambda i: (i, 0, 0)
    return pl.pallas_call(
        functools.partial(_nsa_sample_win_body, n_new=t),
        grid=(bs,),
        in_specs=[pl.BlockSpec((1, t, NSA_WIDTH), seq),
                  pl.BlockSpec((1, wb, 2 * KV_WIDTH), seq),
                  pl.BlockSpec((1, t, 2 * KV_WIDTH), seq)],
        out_specs=[pl.BlockSpec((1, t, NSA_WIDTH), seq), pl.BlockSpec((1, wb, 2 * KV_WIDTH), seq)],
        out_shape=[jax.ShapeDtypeStruct((bs, t, NSA_WIDTH), F32),
                   jax.ShapeDtypeStruct((bs, wb, 2 * KV_WIDTH), F32)],
        compiler_params=pltpu.CompilerParams(dimension_semantics=("parallel",), vmem_limit_bytes=VMEM_LIMIT),
        name="nsa_sample_win",
    )(qr, cache_win3d, new_win)


def _gla_constants(n_seq):
    tile = GLA_TILE
    seq_len = tile // n_seq
    t = np.arange(tile)[:, None]
    r = np.arange(tile)[None, :]
    same = (t // seq_len) == (r // seq_len)
    ops = [same & (r <= t), same & (r > t)]
    masks = []
    m = 1
    while 2 * m <= seq_len:
        upper = (t % (2 * m)) >= m
        mid = (t // (2 * m)) * (2 * m) + m - 1
        ops.append(upper & (r > mid) & (r <= t))
        ops.append((~upper) & (r > t) & (r <= mid))
        same_blk = (t // (2 * m)) == (r // (2 * m))
        masks.append(same_blk & upper & ((r % (2 * m)) < m))
        m *= 2
    ops = np.concatenate([o.astype(np.float32) for o in ops], axis=0)
    masks = np.stack([k.astype(np.float32) for k in masks], axis=0)
    lane_seq = (np.arange(n_seq * GLA_DK)[None, :] // GLA_DK) == (np.arange(tile)[:, None] // seq_len)
    rep = (np.arange(n_seq * GLA_DK)[None, :] % GLA_DK) == np.arange(GLA_DK)[:, None]
    return (jnp.asarray(ops, BF16), jnp.asarray(masks, F32), jnp.asarray(lane_seq.astype(np.float32), F32),
            jnp.asarray(rep.astype(np.float32), BF16))


def _gla_body(gq_ref, gk_ref, gv_ref, gg_ref, misc_ref, s0_ref, wa_hi_ref, wa_lo_ref, ba_ref, ops_ref, masks_ref,
              lane_seq_ref, rep_ref, gnorm_ref, o_ref, sfin_ref, st_sc, *, n_levels):
    tile = GLA_TILE
    step = pl.program_id(1)

    @pl.when(step == 0)
    def _():
        st_sc[...] = s0_ref[0]

    m_hi, m_lo = _split(misc_ref[0])
    x = _dot(m_hi, wa_hi_ref[...]) + _dot(m_hi, wa_lo_ref[...]) + _dot(m_lo, wa_hi_ref[...]) + ba_ref[...]
    la = (jnp.minimum(x, 0.0) - jnp.log(1.0 + jnp.exp(-jnp.abs(x)))) * (1.0 / GLA_GATE_TAU)
    la_hi, la_lo = _split(la)
    cums = _dot(ops_ref[...], la_hi) + _dot(ops_ref[...], la_lo)
    lane_seq = lane_seq_ref[...]
    rep = rep_ref[...]
    multi = lane_seq.shape[1] > GLA_DK

    def spread(a):
        a = a.astype(BF16)
        return (_dot(a, rep) * lane_seq).astype(BF16) if multi else a

    for h in range(GLA_HEADS):
        dk = slice(h * GLA_DK, (h + 1) * GLA_DK)
        dv = slice(h * GLA_DV, (h + 1) * GLA_DV)
        q = gq_ref[0, :, dk] * (GLA_DK ** -0.5)
        k = gk_ref[0, :, dk]
        v = gv_ref[0, :, dv]
        v_b = v.astype(BF16)
        st = st_sc[h]
        since_start = cums[0:tile, dk]
        until_end = cums[tile:2 * tile, dk]
        o = _dot_nt(spread(q * jnp.exp(since_start)), st.astype(BF16))
        a = jnp.zeros((tile, tile), F32)
        for lv in range(n_levels):
            dq = cums[(2 + 2 * lv) * tile:(3 + 2 * lv) * tile, dk]
            dkk = cums[(3 + 2 * lv) * tile:(4 + 2 * lv) * tile, dk]
            a = a + _dot_nt((q * jnp.exp(dq)).astype(BF16), (k * jnp.exp(dkk)).astype(BF16)) * masks_ref[lv]
        o = o + _dot(a.astype(BF16), v_b) + jnp.sum(q * k, axis=-1, keepdims=True) * v
        la_h = la[:, dk]
        if multi:
            total = jnp.sum(_dot2(la_h, rep) * lane_seq, axis=0, keepdims=True)
        else:
            total = jnp.sum(la_h, axis=0, keepdims=True)
        st_new = st * jnp.exp(total) + _dot(v.T.astype(BF16), spread(k * jnp.exp(until_end)))
        st_sc[h] = st_new
        ms = jnp.mean(o * o, axis=-1, keepdims=True)
        o_ref[0, :, dv] = o * lax.rsqrt(ms + EPS) * gnorm_ref[...] * _silu(gg_ref[0, :, dv])

    @pl.when(step == pl.num_programs(1) - 1)
    def _():
        sfin_ref[0] = st_sc[...]


def _gla(gq, gk, gv, gg, misc, s0_t, wa_hi, wa_lo, ba, gnorm, n_seq):
    groups, rows, _ = gq.shape
    steps = rows // GLA_TILE
    ops, masks, lane_seq, rep = _gla_constants(n_seq)
    n_levels = masks.shape[0]
    tile = lambda g, s: (g, s, 0)
    c2 = lambda g, s: (0, 0)
    c3 = lambda g, s: (0, 0, 0)
    st_block = (1, GLA_HEADS, GLA_DV, n_seq * GLA_DK)
    st_map = lambda g, s: (g, 0, 0, 0)
    return pl.pallas_call(
        functools.partial(_gla_body, n_levels=n_levels),
        grid=(groups, steps),
        in_specs=[
            pl.BlockSpec((1, GLA_TILE, GLA_QK_WIDTH), tile), pl.BlockSpec((1, GLA_TILE, GLA_QK_WIDTH), tile),
            pl.BlockSpec((1, GLA_TILE, GLA_WIDTH), tile), pl.BlockSpec((1, GLA_TILE, GLA_WIDTH), tile),
            pl.BlockSpec((1, GLA_TILE, MISC_WIDTH), tile),
            pl.BlockSpec(st_block, st_map),
            pl.BlockSpec(wa_hi.shape, c2), pl.BlockSpec(wa_lo.shape, c2), pl.BlockSpec(ba.shape, c2),
            pl.BlockSpec(ops.shape, c2), pl.BlockSpec(masks.shape, c3),
            pl.BlockSpec(lane_seq.shape, c2), pl.BlockSpec(rep.shape, c2), pl.BlockSpec(gnorm.shape, c2),
        ],
        out_specs=[pl.BlockSpec((1, GLA_TILE, GLA_WIDTH), tile), pl.BlockSpec(st_block, st_map)],
        out_shape=[jax.ShapeDtypeStruct((groups, rows, GLA_WIDTH), F32),
                   jax.ShapeDtypeStruct((groups,) + st_block[1:], F32)],
        scratch_shapes=[pltpu.VMEM(st_block[1:], F32)],
        compiler_params=pltpu.CompilerParams(dimension_semantics=("parallel", "arbitrary"),
                                             vmem_limit_bytes=VMEM_LIMIT),
        name="gla",
    )(gq, gk, gv, gg, misc, s0_t, wa_hi, wa_lo, ba, ops, masks, lane_seq, rep, gnorm)


def _out_body(x_ref, oc_ref, os_ref, ow_ref, og_ref, misc_ref, gex_ref, wout_ref, gffn_ref, wa_ref, wb_ref, wdown_ref,
              y_ref, hn_sc):
    c = pl.program_id(1)

    @pl.when(c == 0)
    def _():
        gates = 1.0 / (1.0 + jnp.exp(-misc_ref[...]))
        branch = (oc_ref, os_ref, ow_ref)
        o_nsa = None
        for i in range(3):
            g_i = _dot2(gates, gex_ref[i])
            term = g_i * branch[i][...]
            o_nsa = term if o_nsa is None else o_nsa + term
        x1 = (x_ref[...] + _dot(o_nsa.astype(BF16), wout_ref[0:NSA_WIDTH, :])
              + _dot(og_ref[...].astype(BF16), wout_ref[NSA_WIDTH:NSA_WIDTH + GLA_WIDTH, :]))
        ms = jnp.mean(x1 * x1, axis=-1, keepdims=True)
        hn_sc[...] = (x1 * lax.rsqrt(ms + EPS) * gffn_ref[...]).astype(BF16)
        y_ref[...] = x1

    hn = hn_sc[...]
    act = _silu(_dot(hn, wa_ref[...])) * _dot(hn, wb_ref[...])
    y_ref[...] += _dot(act.astype(BF16), wdown_ref[...])


def _out(x2d, oc, osel, ow, og, misc, gex, wout, gffn, wup, wdown):
    n = x2d.shape[0]
    tm = min(OUT_ROW_TILE, n)
    assert n % tm == 0
    n_chunks = FFN_HIDDEN // FFN_CHUNK
    row = lambda i, c: (i, 0)
    c2 = lambda i, c: (0, 0)
    return pl.pallas_call(
        _out_body,
        grid=(n // tm, n_chunks),
        in_specs=[
            pl.BlockSpec((tm, D_MODEL), row),
            pl.BlockSpec((tm, NSA_WIDTH), row), pl.BlockSpec((tm, NSA_WIDTH), row), pl.BlockSpec((tm, NSA_WIDTH), row),
            pl.BlockSpec((tm, GLA_WIDTH), row), pl.BlockSpec((tm, MISC_WIDTH), row),
            pl.BlockSpec(gex.shape, lambda i, c: (0, 0, 0)),
            pl.BlockSpec(wout.shape, c2),
            pl.BlockSpec(gffn.shape, c2),
            pl.BlockSpec((D_MODEL, FFN_CHUNK), lambda i, c: (0, c)),
            pl.BlockSpec((D_MODEL, FFN_CHUNK), lambda i, c: (0, n_chunks + c)),
            pl.BlockSpec((FFN_CHUNK, D_MODEL), lambda i, c: (c, 0)),
        ],
        out_specs=pl.BlockSpec((tm, D_MODEL), row),
        out_shape=jax.ShapeDtypeStruct((n, D_MODEL), F32),
        scratch_shapes=[pltpu.VMEM((tm, D_MODEL), BF16)],
        compiler_params=pltpu.CompilerParams(dimension_semantics=("parallel", "arbitrary"),
                                             vmem_limit_bytes=VMEM_LIMIT),
        name="out",
    )(x2d, oc, osel, ow, og, misc, gex, wout, gffn, wup, wup, wdown)


def _rope_tables(pos):
    half = HEAD_DIM // 2
    inv = ROPE_THETA ** (-jnp.arange(half, dtype=F32) / half)
    ang = pos.astype(F32)[:, None] * inv[None, :]
    cos, sin = jnp.cos(ang), jnp.sin(ang)
    reps = LANES // HEAD_DIM
    return jnp.tile(jnp.concatenate([cos, cos], axis=1), (1, reps)), jnp.tile(jnp.concatenate([-sin, sin], axis=1), (1, reps))


def _overlap_matrix(n_chunks, n_cols):
    start = np.arange(n_chunks)[:, None] * CMP_STRIDE
    j0 = np.arange(n_cols)[None, :] * SEL_BLOCK
    return jnp.asarray(((start < j0 + SEL_BLOCK) & (start + CMP_BLOCK > j0)).astype(np.float32), BF16)


def _expand_matrix(n_tiles, keys):
    key = np.arange(n_tiles)[:, None, None] * keys + np.arange(keys)[None, None, :]
    blk = np.arange(LANES)[None, :, None]
    return jnp.asarray((key // SEL_BLOCK == blk).astype(np.float32), BF16)


def _block_diag_ones(width):
    i = np.arange(width)
    return jnp.asarray((i[:, None] // HEAD_DIM == i[None, :] // HEAD_DIM).astype(np.float32), BF16)


def _gate_expand():
    lane = np.arange(MISC_WIDTH)[None, :, None]
    out = np.arange(NSA_WIDTH)[None, None, :]
    c = np.arange(3)[:, None, None]
    return jnp.asarray(((lane == (out // HEAD_DIM) * 3 + c) & (lane < GATE_COLS)).astype(np.float32), BF16)


def _pad_cols(w, width):
    return jnp.pad(w, ((0, 0), (0, width - w.shape[1])))


def kernel(x_prompt, x_sample, cache_nsa_kv, cache_win_kv, state_gla, page_table, w_norm_mix, w_in, nsa_q_gain,
           nsa_k_gain, cmp_pos_k, cmp_w1_k, cmp_w2_k, cmp_pos_v, cmp_w1_v, cmp_w2_v, gla_w_a2, gla_b_a,
           gla_norm_gain, w_out, w_norm_ffn, w_ffn_up, w_ffn_down):
    b_p, t_p, _ = x_prompt.shape
    b_s, t_s, _ = x_sample.shape
    depth = w_in.shape[0]
    assert depth == 1, "single layer"
    n_pages = page_table.shape[1]
    past_len = n_pages * PAGE_SIZE
    n_pool = cache_nsa_kv.shape[1]
    wb = cache_win_kv.shape[2]
    assert t_p % SEL_KEY_TILE == 0 and t_p >= WINDOW + Q_TILE and (b_p * t_p) % ROW_TILE == 0
    assert (b_s * t_s) % ROW_TILE == 0 and GLA_TILE % t_s == 0 and b_s % (GLA_TILE // t_s) == 0
    assert n_pages % PAGES_PER_STEP == 0 and wb == WINDOW and ROW_TILE % t_s == 0
    l = 0

    q_w, kv_w, gate_w, gq_w, gk_w, gv_w, alow_w, gg_w = jnp.split(
        w_in[l], np.cumsum([NSA_WIDTH, 6 * KV_WIDTH, GATE_COLS, GLA_QK_WIDTH, GLA_QK_WIDTH, GLA_WIDTH,
                            GLA_GATE_RANK])[:].tolist(), axis=1)
    misc_w = _pad_cols(jnp.concatenate([gate_w, alow_w], axis=1), MISC_WIDTH)
    w_perm = jnp.concatenate([q_w, kv_w, gq_w, gk_w, gv_w, gg_w, misc_w], axis=1).astype(BF16)
    gmix = w_norm_mix[l][None, :]
    qg = jnp.tile(nsa_q_gain[l], NSA_HEADS)[None, :]
    kg = [jnp.tile(nsa_k_gain[l, i], NSA_KV_HEADS)[None, :] for i in range(3)]
    ones512 = _block_diag_ones(NSA_WIDTH)
    ones128 = _block_diag_ones(LANES)
    wa = jnp.zeros((MISC_WIDTH, GLA_QK_WIDTH), F32).at[GATE_COLS:GATE_COLS + GLA_GATE_RANK].set(gla_w_a2[l])
    wa_hi, wa_lo = _split(wa)
    ba = gla_b_a[l][None, :]
    gnorm = gla_norm_gain[l][None, :]
    wk1 = _cmp1_weights(cmp_w1_k[l])
    wv1 = _cmp1_weights(cmp_w1_v[l])
    eye2 = jnp.eye(NSA_KV_HEADS, dtype=F32)
    w2k_bd = jnp.einsum("ed,hg->hegd", cmp_w2_k[l], eye2).reshape(2 * CMP_HIDDEN, KV_WIDTH).astype(BF16)
    w2v_bd = jnp.einsum("ed,hg->hegd", cmp_w2_v[l], eye2).reshape(2 * CMP_HIDDEN, KV_WIDTH).astype(BF16)
    flat = CMP_BLOCK * HEAD_DIM
    posk, posv = cmp_pos_k[l].reshape(1, flat), cmp_pos_v[l].reshape(1, flat)
    w1k_flat, w1v_flat = cmp_w1_k[l].reshape(flat, CMP_HIDDEN), cmp_w1_v[l].reshape(flat, CMP_HIDDEN)
    gex = _gate_expand()
    wout = w_out[l].astype(BF16)
    gffn = w_norm_ffn[l][None, :]
    wup = w_ffn_up[l].astype(BF16)
    wdown = w_ffn_down[l].astype(BF16)

    cos_p, sin_p = _rope_tables(jnp.arange(t_p))
    xp2 = x_prompt.reshape(b_p * t_p, D_MODEL)
    (qn, qr, rows, win, kvb, gq, gk, gv, gg, misc) = _proj(
        xp2, gmix, w_perm, cos_p, sin_p, qg, kg[1], kg[2], ones512, t_p // ROW_TILE)
    rows3 = rows.reshape(b_p, t_p, 4 * KV_WIDTH)
    g_p = _cmp1_prompt(rows3, wk1, wv1)
    kc_p, vc_p = _cmp2(g_p, posk, w1k_flat, posv, w1v_flat, w2k_bd, w2v_bd, kg[0], ones128)
    ov_p = _overlap_matrix(t_p // CMP_STRIDE, LANES)
    ex_p = _expand_matrix(t_p // SEL_KEY_TILE, SEL_KEY_TILE)
    oc, osel, ow = _nsa_prompt(qn.reshape(b_p, t_p, -1), qr.reshape(b_p, t_p, -1), kc_p, vc_p,
                               kvb.reshape(b_p, t_p, -1), ov_p, ex_p)
    s0_p = jnp.zeros((b_p, GLA_HEADS, GLA_DV, GLA_DK), F32)
    r3 = lambda a, g=b_p: a.reshape(g, -1, a.shape[-1])
    og, sfin_p = _gla(r3(gq), r3(gk), r3(gv), r3(gg), r3(misc), s0_p, wa_hi, wa_lo, ba, gnorm, 1)
    y_p = _out(xp2, oc.reshape(b_p * t_p, -1), osel.reshape(b_p * t_p, -1), ow.reshape(b_p * t_p, -1),
               og.reshape(b_p * t_p, -1), misc, gex, wout, gffn, wup, wdown)
    out_rows_p = rows3.reshape(1, b_p, t_p, 4, NSA_KV_HEADS, HEAD_DIM)
    wlen = min(WINDOW, t_p)
    out_win_p = win.reshape(b_p, t_p, 2, NSA_KV_HEADS, HEAD_DIM)[None, :, t_p - wlen:]
    out_gla_p = jnp.swapaxes(sfin_p, 2, 3)[None].astype(state_gla.dtype)

    pos_s = past_len + jnp.arange(t_s)
    cos_s, sin_s = _rope_tables(pos_s)
    cos_s = jnp.tile(cos_s, (ROW_TILE // t_s, 1))
    sin_s = jnp.tile(sin_s, (ROW_TILE // t_s, 1))
    xs2 = x_sample.reshape(b_s * t_s, D_MODEL)
    (qn_s, qr_s, rows_s, win_s, kvb_s, gq_s, gk_s, gv_s, gg_s, misc_s) = _proj(
        xs2, gmix, w_perm, cos_s, sin_s, qg, kg[1], kg[2], ones512, 1)
    cache3d = cache_nsa_kv[l].reshape(n_pool, PAGE_SIZE, 4 * KV_WIDTH)
    g_s = _cmp1_sample(cache3d, page_table, wk1, wv1)
    kc_s, vc_s = _cmp2(g_s, posk, w1k_flat, posv, w1v_flat, w2k_bd, w2v_bd, kg[0], ones128)
    n_sel_s = -(-(past_len + t_s) // SEL_BLOCK)
    sel_cols = -(-n_sel_s // LANES) * LANES
    ov_s = _overlap_matrix(past_len // CMP_STRIDE, sel_cols)
    seq3 = lambda a: a.reshape(b_s, t_s, a.shape[-1])
    oc_s, sel_s = _nsa_sample_cmp(seq3(qn_s), kc_s, vc_s, ov_s, past_len)
    steps = n_pages // PAGES_PER_STEP
    blocks_per_step = PAGES_PER_STEP * PAGE_SIZE // SEL_BLOCK
    selm = sel_s[:, :, :steps * blocks_per_step].reshape(b_s, NSA_KV_HEADS, 1, t_s, steps, blocks_per_step)
    selm = jnp.broadcast_to(selm, (b_s, NSA_KV_HEADS, NSA_GROUP, t_s, steps, blocks_per_step))
    selm = selm.transpose(0, 4, 1, 2, 3, 5).reshape(b_s, steps, NSA_KV_HEADS * NSA_GROUP * t_s, blocks_per_step)
    selm = jnp.pad(selm, ((0, 0), (0, 0), (0, 0), (0, LANES - blocks_per_step)))
    ex_s = _expand_matrix(1, PAGES_PER_STEP * PAGE_SIZE)
    os_s = _nsa_sample_sel(cache3d, page_table, seq3(qr_s), selm, seq3(kvb_s), ex_s)
    cache_win3d = cache_win_kv[l].reshape(b_s, wb, 2 * KV_WIDTH)
    ow_s, new_win = _nsa_sample_win(seq3(qr_s), cache_win3d, seq3(win_s))
    n_seq = GLA_TILE // t_s
    groups = b_s // n_seq
    s0_s = state_gla[l].astype(F32).reshape(groups, n_seq, GLA_HEADS, GLA_DK, GLA_DV)
    s0_s = s0_s.transpose(0, 2, 4, 1, 3).reshape(groups, GLA_HEADS, GLA_DV, n_seq * GLA_DK)
    rg = lambda a: a.reshape(groups, GLA_TILE, a.shape[-1])
    og_s, sfin_s = _gla(rg(gq_s), rg(gk_s), rg(gv_s), rg(gg_s), rg(misc_s), s0_s, wa_hi, wa_lo, ba, gnorm, n_seq)
    y_s = _out(xs2, oc_s.reshape(b_s * t_s, -1), os_s.reshape(b_s * t_s, -1), ow_s.reshape(b_s * t_s, -1),
               og_s.reshape(b_s * t_s, -1), misc_s, gex, wout, gffn, wup, wdown)
    out_rows_s = rows_s.reshape(1, b_s, t_s, 4, NSA_KV_HEADS, HEAD_DIM)
    out_win_s = new_win.reshape(1, b_s, wb, 2, NSA_KV_HEADS, HEAD_DIM)
    out_gla_s = sfin_s.reshape(groups, GLA_HEADS, GLA_DV, n_seq, GLA_DK).transpose(0, 3, 1, 4, 2)
    out_gla_s = out_gla_s.reshape(1, b_s, GLA_HEADS, GLA_DK, GLA_DV).astype(state_gla.dtype)

    return (y_p.reshape(b_p, t_p, D_MODEL), y_s.reshape(b_s, t_s, D_MODEL), out_rows_p, out_win_p, out_gla_p,
            out_rows_s, out_win_s, out_gla_s)
```

```python
import functools
import math

import numpy as np
import jax
import jax.numpy as jnp
from jax import lax
from jax.experimental import pallas as pl
from jax.experimental.pallas import tpu as pltpu

F32 = jnp.float32
BF16 = jnp.bfloat16

D_MODEL = 1024
PAGE_SIZE = 128
NSA_HEADS = 8
NSA_KV_HEADS = 2
NSA_GROUP = NSA_HEADS // NSA_KV_HEADS
HEAD_DIM = 64
NSA_WIDTH = NSA_HEADS * HEAD_DIM
KV_WIDTH = NSA_KV_HEADS * HEAD_DIM
CMP_BLOCK = 32
CMP_STRIDE = 16
CMP_HIDDEN = 2 * HEAD_DIM
SEL_BLOCK = 64
SEL_TOPK = 16
WINDOW = 512
FORCE_SCORE = 1.0e4
GLA_HEADS = 4
GLA_DK = 64
GLA_DV = 128
GLA_WIDTH = GLA_HEADS * GLA_DV
GLA_QK_WIDTH = GLA_HEADS * GLA_DK
GLA_GATE_RANK = 16
GLA_GATE_TAU = 16.0
FFN_HIDDEN = -(-8 * D_MODEL // (3 * 256)) * 256
ROPE_THETA = 10000.0
EPS = 1e-6
NEG = -1.0e30

LANES = 128
MISC_WIDTH = LANES
GATE_COLS = 3 * NSA_HEADS
PROJ_WIDTH = NSA_WIDTH + 6 * KV_WIDTH + 2 * GLA_QK_WIDTH + 2 * GLA_WIDTH + MISC_WIDTH
ROW_TILE = 512
OUT_ROW_TILE = 1024
Q_TILE = 128
SEL_KEY_TILE = 512
GLA_TILE = 128
PAGES_PER_STEP = 16
FFN_CHUNK = 256
VMEM_LIMIT = 56 * 1024 * 1024


def _dot(a, b):
    return jnp.dot(a, b, preferred_element_type=F32)


def _dot_nt(a, b):
    return lax.dot_general(a, b, (((1,), (1,)), ((), ())), preferred_element_type=F32)


def _split(x):
    hi = x.astype(BF16)
    lo = (x - hi.astype(F32)).astype(BF16)
    return hi, lo


def _dot2(x, w_bf16):
    hi, lo = _split(x)
    return _dot(hi, w_bf16) + _dot(lo, w_bf16)


def _silu(x):
    return x * (1.0 / (1.0 + jnp.exp(-x)))


def _head_norm(v, ones_bd, gain):
    ss = _dot((v * v).astype(BF16), ones_bd)
    return v * lax.rsqrt(ss * (1.0 / HEAD_DIM) + EPS) * gain


def _rope(v, cos, sin_signed):
    width = v.shape[-1]
    reps = width // LANES
    if reps > 1:
        cos = jnp.concatenate([cos] * reps, axis=1)
        sin_signed = jnp.concatenate([sin_signed] * reps, axis=1)
    lane = lax.broadcasted_iota(jnp.int32, v.shape, 1)
    first_half = (lane & (HEAD_DIM - 1)) < (HEAD_DIM // 2)
    rot = jnp.where(first_half, pltpu.roll(v, width - HEAD_DIM // 2, 1), pltpu.roll(v, HEAD_DIM // 2, 1))
    return v * cos + rot * sin_signed


def _softmax_rows(s):
    m = jnp.max(s, axis=-1, keepdims=True)
    m = jnp.where(m == -jnp.inf, 0.0, m)
    e = jnp.exp(s - m)
    return e / jnp.maximum(jnp.sum(e, axis=-1, keepdims=True), 1e-30)


def _proj_body(x_ref, gmix_ref, w_ref, cos_ref, sin_ref, qg_ref, kg1_ref, kg2_ref, ones_ref,
               qn_ref, qr_ref, rows_ref, win_ref, kvb_ref, gq_ref, gk_ref, gv_ref, gg_ref, misc_ref, *, rows_minor):
    x = x_ref[...]
    ms = jnp.mean(x * x, axis=-1, keepdims=True)
    h = (x * lax.rsqrt(ms + EPS) * gmix_ref[...]).astype(BF16)
    cos = cos_ref[...]
    sin = sin_ref[...]
    ones512 = ones_ref[...]
    ones128 = ones512[:LANES, :LANES]
    scale = HEAD_DIM ** -0.5

    off = 0
    q = _dot(h, w_ref[:, off:off + NSA_WIDTH])
    off += NSA_WIDTH
    qn = _head_norm(q, ones512, qg_ref[...])
    qn_ref[...] = (qn * scale).astype(BF16)
    qr_ref[...] = (_rope(qn, cos, sin) * scale).astype(BF16)

    kv = _dot(h, w_ref[:, off:off + 6 * KV_WIDTH])
    off += 6 * KV_WIDTH
    k_slc = _rope(_head_norm(kv[:, 2 * KV_WIDTH:3 * KV_WIDTH], ones128, kg1_ref[...]), cos, sin)
    k_win = _rope(_head_norm(kv[:, 4 * KV_WIDTH:5 * KV_WIDTH], ones128, kg2_ref[...]), cos, sin)
    v_slc = kv[:, 3 * KV_WIDTH:4 * KV_WIDTH]
    v_win = kv[:, 5 * KV_WIDTH:6 * KV_WIDTH]
    if rows_minor:
        rows_ref[0, 0:2 * KV_WIDTH, :] = kv[:, 0:2 * KV_WIDTH].T
        rows_ref[0, 2 * KV_WIDTH:3 * KV_WIDTH, :] = k_slc.T
        rows_ref[0, 3 * KV_WIDTH:4 * KV_WIDTH, :] = v_slc.T
        win_ref[0, 0:KV_WIDTH, :] = k_win.T
        win_ref[0, KV_WIDTH:2 * KV_WIDTH, :] = v_win.T
    else:
        rows_ref[:, 0:2 * KV_WIDTH] = kv[:, 0:2 * KV_WIDTH]
        rows_ref[:, 2 * KV_WIDTH:3 * KV_WIDTH] = k_slc
        rows_ref[:, 3 * KV_WIDTH:4 * KV_WIDTH] = v_slc
        win_ref[:, 0:KV_WIDTH] = k_win
        win_ref[:, KV_WIDTH:2 * KV_WIDTH] = v_win
    kvb_ref[:, 0:KV_WIDTH] = k_slc.astype(BF16)
    kvb_ref[:, KV_WIDTH:2 * KV_WIDTH] = v_slc.astype(BF16)
    kvb_ref[:, 2 * KV_WIDTH:3 * KV_WIDTH] = k_win.astype(BF16)
    kvb_ref[:, 3 * KV_WIDTH:4 * KV_WIDTH] = v_win.astype(BF16)

    gq_ref[...] = _dot(h, w_ref[:, off:off + GLA_QK_WIDTH])
    off += GLA_QK_WIDTH
    gk_ref[...] = _dot(h, w_ref[:, off:off + GLA_QK_WIDTH])
    off += GLA_QK_WIDTH
    gv_ref[...] = _dot(h, w_ref[:, off:off + GLA_WIDTH])
    off += GLA_WIDTH
    gg_ref[...] = _dot(h, w_ref[:, off:off + GLA_WIDTH])
    off += GLA_WIDTH
    misc_ref[...] = _dot(h, w_ref[:, off:off + MISC_WIDTH])


def _proj(x2d, gmix, w_perm, cos_t, sin_t, qg, kg1, kg2, ones512, table_blocks, seq_len=None):
    n = x2d.shape[0]
    tm = ROW_TILE
    row = lambda i: (i, 0)
    const = lambda i: (0, 0)
    tab = lambda i: (i % table_blocks, 0)
    widths = [(NSA_WIDTH, BF16), (NSA_WIDTH, BF16), (4 * KV_WIDTH, F32), (2 * KV_WIDTH, F32), (4 * KV_WIDTH, BF16),
              (GLA_QK_WIDTH, F32), (GLA_QK_WIDTH, F32), (GLA_WIDTH, F32), (GLA_WIDTH, F32), (MISC_WIDTH, F32)]
    out_specs = [pl.BlockSpec((tm, w), row) for w, _ in widths]
    out_shape = [jax.ShapeDtypeStruct((n, w), dt) for w, dt in widths]
    if seq_len is not None:
        tiles = seq_len // tm
        for i in (2, 3):
            w, dt = widths[i]
            out_specs[i] = pl.BlockSpec((1, w, tm), lambda i: (i // tiles, 0, i % tiles))
            out_shape[i] = jax.ShapeDtypeStruct((n // seq_len, w, seq_len), dt)
    return pl.pallas_call(
        functools.partial(_proj_body, rows_minor=seq_len is not None),
        grid=(n // tm,),
        in_specs=[
            pl.BlockSpec((tm, D_MODEL), row),
            pl.BlockSpec((1, D_MODEL), const),
            pl.BlockSpec((D_MODEL, PROJ_WIDTH), const),
            pl.BlockSpec((tm, LANES), tab),
            pl.BlockSpec((tm, LANES), tab),
            pl.BlockSpec((1, NSA_WIDTH), const),
            pl.BlockSpec((1, KV_WIDTH), const),
            pl.BlockSpec((1, KV_WIDTH), const),
            pl.BlockSpec((NSA_WIDTH, NSA_WIDTH), const),
        ],
        out_specs=out_specs,
        out_shape=out_shape,
        compiler_params=pltpu.CompilerParams(dimension_semantics=("parallel",), vmem_limit_bytes=VMEM_LIMIT),
        name="proj",
    )(x2d, gmix, w_perm, cos_t, sin_t, qg, kg1, kg2, ones512)


def _cmp1_body(*refs, n_src, n_prefetch):
    refs = refs[n_prefetch:]
    n_refs = len(refs) - 5
    src = refs[:n_refs]
    wk_ref, wv_ref, out_ref, xk_sc, xv_sc = refs[n_refs:]
    for i in range(n_src):
        rows = slice(i * PAGE_SIZE, (i + 1) * PAGE_SIZE)
        if n_refs == 1:
            k_t, v_t = src[0][0, :, rows], src[0][1, :, rows]
        else:
            k_t, v_t = src[i][0], src[i][1]
        xk_sc[rows, :] = k_t.T
        xv_sc[rows, :] = v_t.T
    n_chunks = n_src * PAGE_SIZE // CMP_STRIDE

    def rows_at(ref, s):
        return ref[pl.ds(s, n_chunks, stride=CMP_STRIDE), :]

    acc_k = None
    acc_v = None
    for sp in range(CMP_STRIDE // 2):
        lk = jnp.concatenate([rows_at(xk_sc, 2 * sp), rows_at(xk_sc, 2 * sp + 1)], axis=1).astype(BF16)
        lv = jnp.concatenate([rows_at(xv_sc, 2 * sp), rows_at(xv_sc, 2 * sp + 1)], axis=1).astype(BF16)
        dk = _dot(lk, wk_ref[sp])
        dv = _dot(lv, wv_ref[sp])
        acc_k = dk if acc_k is None else acc_k + dk
        acc_v = dv if acc_v is None else acc_v + dv
    hid = 4 * CMP_HIDDEN
    out_ref[0, :, 0:hid] = acc_k
    out_ref[0, :, hid:2 * hid] = acc_v


def _cmp1_weights(w1):
    w = w1.reshape(2, CMP_STRIDE // 2, 2, HEAD_DIM, CMP_HIDDEN)
    eye = jnp.eye(NSA_KV_HEADS, dtype=w1.dtype)
    big = jnp.einsum("jpsde,hg->pshdgje", w, eye)
    return big.reshape(CMP_STRIDE // 2, 4 * HEAD_DIM, 4 * CMP_HIDDEN).astype(BF16)


def _cmp1_scratch():
    rows = PAGES_PER_STEP * PAGE_SIZE
    return [pltpu.VMEM((rows, KV_WIDTH), F32), pltpu.VMEM((rows, KV_WIDTH), F32)]


def _cmp1_prompt(rows_t, wk, wv):
    b, _, _, t = rows_t.shape
    span = PAGES_PER_STEP * PAGE_SIZE
    chunks = span // CMP_STRIDE
    wspec = pl.BlockSpec(wk.shape, lambda i, g: (0, 0, 0))
    return pl.pallas_call(
        functools.partial(_cmp1_body, n_src=PAGES_PER_STEP, n_prefetch=0),
        grid=(b, t // span),
        in_specs=[pl.BlockSpec((None, 2, KV_WIDTH, span), lambda i, g: (i, 0, 0, g)), wspec, wspec],
        out_specs=pl.BlockSpec((1, chunks, 8 * CMP_HIDDEN), lambda i, g: (i, g, 0)),
        out_shape=jax.ShapeDtypeStruct((b, t // CMP_STRIDE, 8 * CMP_HIDDEN), F32),
        scratch_shapes=_cmp1_scratch(),
        compiler_params=pltpu.CompilerParams(dimension_semantics=("parallel", "parallel"),
                                             vmem_limit_bytes=VMEM_LIMIT),
        name="cmp1_prompt",
    )(rows_t, wk, wv)


def _page_specs(kind_block):
    def spec(i):
        return pl.BlockSpec((None, 2, KV_WIDTH, PAGE_SIZE),
                            lambda b, g, pt: (pt[b, g * PAGES_PER_STEP + i], kind_block, 0, 0))
    return [spec(i) for i in range(PAGES_PER_STEP)]


def _cmp1_sample(cache_t, page_table, wk, wv):
    bs, n_pages = page_table.shape
    steps = n_pages // PAGES_PER_STEP
    chunks_per_step = PAGES_PER_STEP * PAGE_SIZE // CMP_STRIDE
    wspec = pl.BlockSpec(wk.shape, lambda b, g, pt: (0, 0, 0))
    grid_spec = pltpu.PrefetchScalarGridSpec(
        num_scalar_prefetch=1,
        grid=(bs, steps),
        in_specs=_page_specs(0) + [wspec, wspec],
        out_specs=pl.BlockSpec((1, chunks_per_step, 8 * CMP_HIDDEN), lambda b, g, pt: (b, g, 0)),
        scratch_shapes=_cmp1_scratch(),
    )
    return pl.pallas_call(
        functools.partial(_cmp1_body, n_src=PAGES_PER_STEP, n_prefetch=1),
        grid_spec=grid_spec,
        out_shape=jax.ShapeDtypeStruct((bs, steps * chunks_per_step, 8 * CMP_HIDDEN), F32),
        compiler_params=pltpu.CompilerParams(dimension_semantics=("parallel", "parallel"),
                                             vmem_limit_bytes=VMEM_LIMIT),
        name="cmp1_sample",
    )(page_table, *([cache_t] * PAGES_PER_STEP), wk, wv)


def _cmp2_body(g_ref, posk_ref, w1k_ref, posv_ref, w1v_ref, w2k_ref, w2v_ref, kg0_ref, ones_ref, kc_ref, vc_ref):
    g = g_ref[0]
    chunks = g.shape[0]
    hd = CMP_HIDDEN

    def pos_term(pos_ref, w1_ref):
        p = jnp.broadcast_to(pos_ref[...], (8, pos_ref.shape[1]))
        p_hi, p_lo = _split(p)
        w_hi, w_lo = _split(w1_ref[...])
        return (_dot(p_hi, w_hi) + _dot(p_hi, w_lo) + _dot(p_lo, w_hi))[0:1, :]

    def hidden(base, pos):
        out = []
        for h in range(NSA_KV_HEADS):
            j0 = g[:, base + 2 * h * hd: base + (2 * h + 1) * hd]
            j1 = g[:, base + (2 * h + 1) * hd: base + (2 * h + 2) * hd]
            out.append(j0 + pltpu.roll(j1, chunks - 1, 0) + pos)
        return _silu(jnp.concatenate(out, axis=1)).astype(BF16)

    kc = _dot(hidden(0, pos_term(posk_ref, w1k_ref)), w2k_ref[...])
    kc_ref[0] = _head_norm(kc, ones_ref[...], kg0_ref[...])
    vc_ref[0] = _dot(hidden(4 * hd, pos_term(posv_ref, w1v_ref)), w2v_ref[...])


def _cmp2(g, posk, w1k, posv, w1v, w2k_bd, w2v_bd, kg0, ones128):
    b, chunks, _ = g.shape
    c2 = lambda i: (0, 0)
    seq = lambda i: (i, 0, 0)
    flat = CMP_BLOCK * HEAD_DIM
    return pl.pallas_call(
        _cmp2_body,
        grid=(b,),
        in_specs=[
            pl.BlockSpec((1, chunks, 8 * CMP_HIDDEN), seq),
            pl.BlockSpec((1, flat), c2), pl.BlockSpec((flat, CMP_HIDDEN), c2),
            pl.BlockSpec((1, flat), c2), pl.BlockSpec((flat, CMP_HIDDEN), c2),
            pl.BlockSpec((2 * CMP_HIDDEN, KV_WIDTH), c2), pl.BlockSpec((2 * CMP_HIDDEN, KV_WIDTH), c2),
            pl.BlockSpec((1, KV_WIDTH), c2), pl.BlockSpec((LANES, LANES), c2),
        ],
        out_specs=[pl.BlockSpec((1, chunks, KV_WIDTH), seq)] * 2,
        out_shape=[jax.ShapeDtypeStruct((b, chunks, KV_WIDTH), F32)] * 2,
        compiler_params=pltpu.CompilerParams(dimension_semantics=("parallel",), vmem_limit_bytes=VMEM_LIMIT),
        name="cmp2",
    )(g, posk, w1k, posv, w1v, w2k_bd, w2v_bd, kg0, ones128)


def _stack_heads(q, kvh):
    base = kvh * NSA_GROUP * HEAD_DIM
    return jnp.concatenate([q[:, base + g * HEAD_DIM: base + (g + 1) * HEAD_DIM] for g in range(NSA_GROUP)], axis=0)


def _unstack_heads(o, rows):
    return jnp.concatenate([o[g * rows:(g + 1) * rows] for g in range(NSA_GROUP)], axis=1)


def _group_sum(p, rows):
    out = p[0:rows]
    for g in range(1, NSA_GROUP):
        out = out + p[g * rows:(g + 1) * rows]
    return out


def _select_blocks(imp, qpos, n_blocks):
    blk = lax.broadcasted_iota(jnp.int32, imp.shape, 1)
    visible = blk * SEL_BLOCK <= qpos
    forced = jnp.logical_or(blk == 0, blk == (qpos >> int(math.log2(SEL_BLOCK))))
    score = jnp.where(forced, FORCE_SCORE, jnp.where(visible, imp, -jnp.inf))
    beaten = jnp.zeros(imp.shape, F32)
    for i in range(n_blocks):
        ci = score[:, i:i + 1]
        earlier = jnp.where(blk > i, 1.0, 0.0)
        beaten = beaten + jnp.where(ci > score, 1.0, jnp.where(ci == score, earlier, 0.0))
    return jnp.where(beaten < float(SEL_TOPK), jnp.where(score > -jnp.inf, 1.0, 0.0), 0.0)


def _nsa_prompt_body(qn_ref, qr_ref, kc_ref, vc_ref, kvb_ref, ov_ref, ex_ref, oc_ref, os_ref, ow_ref, *, seq_len):
    tq = Q_TILE
    tk = SEL_KEY_TILE
    q0 = pl.program_id(1) * tq
    n_cmp = seq_len // CMP_STRIDE - 1
    n_sel = seq_len // SEL_BLOCK
    rows4 = NSA_GROUP * tq
    qn = qn_ref[0]
    qr = qr_ref[0]
    kc = kc_ref[0].astype(BF16)
    vc = vc_ref[0].astype(BF16)
    qpos1 = q0 + lax.broadcasted_iota(jnp.int32, (tq, 1), 0)
    qpos4 = jnp.concatenate([qpos1] * NSA_GROUP, axis=0)
    win_keys = WINDOW + tq
    wstart = pl.multiple_of(jnp.maximum(q0 - WINDOW, 0), tq)
    n_tiles = (q0 + tq + tk - 1) // tk

    oc, osel, ow = [], [], []
    for kvh in range(NSA_KV_HEADS):
        lo, hi = kvh * HEAD_DIM, (kvh + 1) * HEAD_DIM
        qc = _stack_heads(qn, kvh)
        s = _dot_nt(qc, kc[:, lo:hi])
        n_idx = lax.broadcasted_iota(jnp.int32, s.shape, 1)
        vis = jnp.logical_and(n_idx * CMP_STRIDE + (CMP_BLOCK - 1) <= qpos4, n_idx < n_cmp)
        p = _softmax_rows(jnp.where(vis, s, -jnp.inf))
        oc.append(_unstack_heads(_dot(p.astype(BF16), vc[:, lo:hi]), tq))
        imp = _dot2(_group_sum(p, tq), ov_ref[...])
        sel = _select_blocks(imp, qpos1, n_sel).astype(BF16)

        qs = _stack_heads(qr, kvh)

        def sel_step(kt, carry, qs=qs, sel=sel, lo=lo, hi=hi):
            m_i, l_i, acc = carry
            k0 = pl.multiple_of(kt * tk, tk)
            k_t = kvb_ref[0, pl.ds(k0, tk), lo:hi]
            v_t = kvb_ref[0, pl.ds(k0, tk), KV_WIDTH + lo:KV_WIDTH + hi]
            sc = _dot_nt(qs, k_t).reshape(NSA_GROUP, tq, tk)
            chosen = _dot(sel, ex_ref[kt])
            kpos = k0 + lax.broadcasted_iota(jnp.int32, (tq, tk), 1)
            keep = jnp.logical_and(chosen > 0.5, kpos <= qpos1)
            sc = jnp.where(keep[None], sc, -jnp.inf).reshape(rows4, tk)
            m_new = jnp.maximum(m_i, jnp.max(sc, axis=-1, keepdims=True))
            alpha = jnp.exp(m_i - m_new)
            pe = jnp.exp(sc - m_new)
            l_new = alpha * l_i + jnp.sum(pe, axis=-1, keepdims=True)
            acc_new = alpha * acc + _dot(pe.astype(BF16), v_t)
            return m_new, l_new, acc_new

        init = (jnp.full((rows4, 1), NEG, F32), jnp.zeros((rows4, 1), F32), jnp.zeros((rows4, HEAD_DIM), F32))
        _, l_f, acc_f = lax.fori_loop(0, n_tiles, sel_step, init)
        osel.append(_unstack_heads(acc_f / jnp.maximum(l_f, 1e-30), tq))

        k_w = kvb_ref[0, pl.ds(wstart, win_keys), 2 * KV_WIDTH + lo:2 * KV_WIDTH + hi]
        v_w = kvb_ref[0, pl.ds(wstart, win_keys), 3 * KV_WIDTH + lo:3 * KV_WIDTH + hi]
        sw = _dot_nt(qs, k_w)
        rel = qpos4 - (wstart + lax.broadcasted_iota(jnp.int32, sw.shape, 1))
        inside = jnp.logical_and(rel >= 0, rel < WINDOW)
        pw = _softmax_rows(jnp.where(inside, sw, -jnp.inf))
        ow.append(_unstack_heads(_dot(pw.astype(BF16), v_w), tq))

    oc_ref[0] = jnp.concatenate(oc, axis=1)
    os_ref[0] = jnp.concatenate(osel, axis=1)
    ow_ref[0] = jnp.concatenate(ow, axis=1)


def _nsa_prompt(qn, qr, kc, vc, kvb, ov, ex):
    b, t, _ = qn.shape
    chunks = kc.shape[1]
    tile = lambda i, j: (i, j, 0)
    seq = lambda i, j: (i, 0, 0)
    out = jax.ShapeDtypeStruct((b, t, NSA_WIDTH), F32)
    return pl.pallas_call(
        functools.partial(_nsa_prompt_body, seq_len=t),
        grid=(b, t // Q_TILE),
        in_specs=[
            pl.BlockSpec((1, Q_TILE, NSA_WIDTH), tile),
            pl.BlockSpec((1, Q_TILE, NSA_WIDTH), tile),
            pl.BlockSpec((1, chunks, KV_WIDTH), seq),
            pl.BlockSpec((1, chunks, KV_WIDTH), seq),
            pl.BlockSpec((1, t, 4 * KV_WIDTH), seq),
            pl.BlockSpec(ov.shape, lambda i, j: (0, 0)),
            pl.BlockSpec(ex.shape, lambda i, j: (0, 0, 0)),
        ],
        out_specs=[pl.BlockSpec((1, Q_TILE, NSA_WIDTH), tile)] * 3,
        out_shape=[out] * 3,
        compiler_params=pltpu.CompilerParams(dimension_semantics=("parallel", "parallel"),
                                             vmem_limit_bytes=VMEM_LIMIT),
        name="nsa_prompt",
    )(qn, qr, kc, vc, kvb, ov, ex)


def _nsa_sample_cmp_body(qn_ref, kc_ref, vc_ref, ov_ref, oc_ref, sel_ref, *, past_len, n_new):
    t = n_new
    n_chunks = kc_ref.shape[1]
    n_cmp = (past_len + t) // CMP_STRIDE - 1
    n_sel = -(-(past_len + t) // SEL_BLOCK)
    qn = qn_ref[0]
    kc = kc_ref[0].astype(BF16)
    vc = vc_ref[0].astype(BF16)
    qpos1 = past_len + lax.broadcasted_iota(jnp.int32, (t, 1), 0)
    qpos4 = jnp.concatenate([qpos1] * NSA_GROUP, axis=0)
    oc, sel = [], []
    for kvh in range(NSA_KV_HEADS):
        lo, hi = kvh * HEAD_DIM, (kvh + 1) * HEAD_DIM
        s = _dot_nt(_stack_heads(qn, kvh), kc[:, lo:hi])
        n_idx = lax.broadcasted_iota(jnp.int32, s.shape, 1)
        vis = jnp.logical_and(n_idx * CMP_STRIDE + (CMP_BLOCK - 1) <= qpos4, n_idx < n_cmp)
        p = _softmax_rows(jnp.where(vis, s, -jnp.inf))
        oc.append(_unstack_heads(_dot(p.astype(BF16), vc[:, lo:hi]), t))
        imp = _dot2(_group_sum(p, t), ov_ref[...])
        sel.append(_select_blocks(imp, qpos1, n_sel))
    oc_ref[0] = jnp.concatenate(oc, axis=1)
    sel_ref[0] = jnp.concatenate(sel, axis=0)
    del n_chunks


def _nsa_sample_cmp(qn, kc, vc, ov, past_len):
    bs, t, _ = qn.shape
    chunks = kc.shape[1]
    seq = lambda i: (i, 0, 0)
    return pl.pallas_call(
        functools.partial(_nsa_sample_cmp_body, past_len=past_len, n_new=t),
        grid=(bs,),
        in_specs=[
            pl.BlockSpec((1, t, NSA_WIDTH), seq),
            pl.BlockSpec((1, chunks, KV_WIDTH), seq),
            pl.BlockSpec((1, chunks, KV_WIDTH), seq),
            pl.BlockSpec(ov.shape, lambda i: (0, 0)),
        ],
        out_specs=[pl.BlockSpec((1, t, NSA_WIDTH), seq), pl.BlockSpec((1, NSA_KV_HEADS * t, ov.shape[1]), seq)],
        out_shape=[jax.ShapeDtypeStruct((bs, t, NSA_WIDTH), F32),
                   jax.ShapeDtypeStruct((bs, NSA_KV_HEADS * t, ov.shape[1]), F32)],
        compiler_params=pltpu.CompilerParams(dimension_semantics=("parallel",), vmem_limit_bytes=VMEM_LIMIT),
        name="nsa_sample_cmp",
    )(qn, kc, vc, ov)


def _nsa_sample_sel_body(*refs, n_new):
    pt_ref = refs[0]
    pages = refs[1:1 + PAGES_PER_STEP]
    qr_ref, selm_ref, kvb_ref, ex_ref, os_ref, m_sc, l_sc, acc_sc = refs[1 + PAGES_PER_STEP:]
    del pt_ref
    t = n_new
    rows4 = NSA_GROUP * t
    step = pl.program_id(1)
    last = pl.num_programs(1) - 1
    qr = qr_ref[0]

    @pl.when(step == 0)
    def _():
        m_sc[...] = jnp.full(m_sc.shape, NEG, F32)
        l_sc[...] = jnp.zeros(l_sc.shape, F32)
        acc_sc[...] = jnp.zeros(acc_sc.shape, F32)

    def update(kvh, sc, v_blk, v_transposed):
        m_i, l_i, acc = m_sc[kvh], l_sc[kvh], acc_sc[kvh]
        m_new = jnp.maximum(m_i, jnp.max(sc, axis=-1, keepdims=True))
        alpha = jnp.exp(m_i - m_new)
        pe = jnp.exp(sc - m_new)
        m_sc[kvh] = m_new
        l_sc[kvh] = alpha * l_i + jnp.sum(pe, axis=-1, keepdims=True)
        pe = pe.astype(BF16)
        acc_sc[kvh] = alpha * acc + (_dot_nt(pe, v_blk) if v_transposed else _dot(pe, v_blk))

    for kvh in range(NSA_KV_HEADS):
        lo, hi = kvh * HEAD_DIM, (kvh + 1) * HEAD_DIM
        qs = _stack_heads(qr, kvh)
        k_t = jnp.concatenate([p[0, lo:hi, :] for p in pages], axis=1).astype(BF16)
        v_t = jnp.concatenate([p[1, lo:hi, :] for p in pages], axis=1).astype(BF16)
        chosen = _dot(selm_ref[0, 0, kvh * rows4:(kvh + 1) * rows4, :].astype(BF16), ex_ref[0])
        sc = jnp.where(chosen > 0.5, _dot(qs, k_t), -jnp.inf)
        update(kvh, sc, v_t, True)

    @pl.when(step == last)
    def _():
        pad = jnp.zeros((LANES - t, HEAD_DIM), F32)
        out = []
        for kvh in range(NSA_KV_HEADS):
            lo, hi = kvh * HEAD_DIM, (kvh + 1) * HEAD_DIM
            qs = _stack_heads(qr, kvh)
            k_new = jnp.concatenate([kvb_ref[0, :, lo:hi].astype(F32), pad], axis=0).astype(BF16)
            v_new = jnp.concatenate([kvb_ref[0, :, KV_WIDTH + lo:KV_WIDTH + hi].astype(F32), pad], axis=0).astype(BF16)
            sc = _dot_nt(qs, k_new)
            key_i = lax.broadcasted_iota(jnp.int32, sc.shape, 1)
            q_t = lax.broadcasted_iota(jnp.int32, sc.shape, 0) & (t - 1)
            update(kvh, jnp.where(key_i <= q_t, sc, -jnp.inf), v_new, False)
            out.append(_unstack_heads(acc_sc[kvh] / jnp.maximum(l_sc[kvh], 1e-30), t))
        os_ref[0] = jnp.concatenate(out, axis=1)


def _nsa_sample_sel(cache_t, page_table, qr, selm, kvb, ex):
    bs, n_pages = page_table.shape
    t = qr.shape[1]
    steps = n_pages // PAGES_PER_STEP
    rows = NSA_KV_HEADS * NSA_GROUP * t
    seq = lambda b, g, pt: (b, 0, 0)
    grid_spec = pltpu.PrefetchScalarGridSpec(
        num_scalar_prefetch=1,
        grid=(bs, steps),
        in_specs=_page_specs(1) + [
            pl.BlockSpec((1, t, NSA_WIDTH), seq),
            pl.BlockSpec((1, 1, rows, LANES), lambda b, g, pt: (b, g, 0, 0)),
            pl.BlockSpec((1, t, 4 * KV_WIDTH), seq),
            pl.BlockSpec((1,) + ex.shape[1:], lambda b, g, pt: (0, 0, 0)),
        ],
        out_specs=pl.BlockSpec((1, t, NSA_WIDTH), seq),
        scratch_shapes=[pltpu.VMEM((NSA_KV_HEADS, NSA_GROUP * t, 1), F32),
                        pltpu.VMEM((NSA_KV_HEADS, NSA_GROUP * t, 1), F32),
                        pltpu.VMEM((NSA_KV_HEADS, NSA_GROUP * t, HEAD_DIM), F32)],
    )
    return pl.pallas_call(
        functools.partial(_nsa_sample_sel_body, n_new=t),
        grid_spec=grid_spec,
        out_shape=jax.ShapeDtypeStruct((bs, t, NSA_WIDTH), F32),
        compiler_params=pltpu.CompilerParams(dimension_semantics=("parallel", "arbitrary"),
                                             vmem_limit_bytes=VMEM_LIMIT),
        name="nsa_sample_sel",
    )(page_table, *([cache_t] * PAGES_PER_STEP), qr, selm, kvb, ex)


def _nsa_sample_win_body(qr_ref, cw_ref, nw_ref, ow_ref, neww_ref, *, n_new):
    t = n_new
    wb = cw_ref.shape[2]
    qr = qr_ref[0]
    cw = cw_ref[0]
    nw_t = jnp.concatenate([nw_ref[0], jnp.zeros((LANES - t, 2 * KV_WIDTH), F32)], axis=0).T
    shifted = pltpu.roll(cw, wb - t, 1)
    lane = lax.broadcasted_iota(jnp.int32, (2 * KV_WIDTH, LANES), 1)
    neww_ref[0, :, 0:wb - LANES] = shifted[:, 0:wb - LANES]
    neww_ref[0, :, wb - LANES:wb] = jnp.where(lane >= LANES - t, pltpu.roll(nw_t, LANES - t, 1), shifted[:, wb - LANES:wb])
    keys_t = jnp.concatenate([cw, nw_t], axis=1).astype(BF16)
    out = []
    for kvh in range(NSA_KV_HEADS):
        lo, hi = kvh * HEAD_DIM, (kvh + 1) * HEAD_DIM
        sw = _dot(_stack_heads(qr, kvh), keys_t[lo:hi, :])
        key_i = lax.broadcasted_iota(jnp.int32, sw.shape, 1)
        q_t = lax.broadcasted_iota(jnp.int32, sw.shape, 0) & (t - 1)
        rel = wb + q_t - key_i
        inside = jnp.logical_and(rel >= 0, rel < WINDOW)
        pw = _softmax_rows(jnp.where(inside, sw, -jnp.inf))
        out.append(_unstack_heads(_dot_nt(pw.astype(BF16), keys_t[KV_WIDTH + lo:KV_WIDTH + hi, :]), t))
    ow_ref[0] = jnp.concatenate(out, axis=1)


def _nsa_sample_win(qr, cache_win_t, new_win):
    bs, t, _ = qr.shape
    wb = cache_win_t.shape[2]
    seq = lambda i: (i, 0, 0)
    return pl.pallas_call(
        functools.partial(_nsa_sample_win_body, n_new=t),
        grid=(bs,),
        in_specs=[pl.BlockSpec((1, t, NSA_WIDTH), seq),
                  pl.BlockSpec((1, 2 * KV_WIDTH, wb), seq),
                  pl.BlockSpec((1, t, 2 * KV_WIDTH), seq)],
        out_specs=[pl.BlockSpec((1, t, NSA_WIDTH), seq), pl.BlockSpec((1, 2 * KV_WIDTH, wb), seq)],
        out_shape=[jax.ShapeDtypeStruct((bs, t, NSA_WIDTH), F32),
                   jax.ShapeDtypeStruct((bs, 2 * KV_WIDTH, wb), F32)],
        compiler_params=pltpu.CompilerParams(dimension_semantics=("parallel",), vmem_limit_bytes=VMEM_LIMIT),
        name="nsa_sample_win",
    )(qr, cache_win_t, new_win)


def _gla_constants(n_seq):
    tile = GLA_TILE
    seq_len = tile // n_seq
    t = np.arange(tile)[:, None]
    r = np.arange(tile)[None, :]
    same = (t // seq_len) == (r // seq_len)
    ops = [same & (r <= t), same & (r > t)]
    masks = []
    m = 1
    while 2 * m <= seq_len:
        upper = (t % (2 * m)) >= m
        mid = (t // (2 * m)) * (2 * m) + m - 1
        ops.append(upper & (r > mid) & (r <= t))
        ops.append((~upper) & (r > t) & (r <= mid))
        same_blk = (t // (2 * m)) == (r // (2 * m))
        masks.append(same_blk & upper & ((r % (2 * m)) < m))
        m *= 2
    ops = np.concatenate([o.astype(np.float32) for o in ops], axis=0)
    masks = np.stack([k.astype(np.float32) for k in masks], axis=0)
    lane_seq = (np.arange(n_seq * GLA_DK)[None, :] // GLA_DK) == (np.arange(tile)[:, None] // seq_len)
    rep = (np.arange(n_seq * GLA_DK)[None, :] % GLA_DK) == np.arange(GLA_DK)[:, None]
    return (jnp.asarray(ops, BF16), jnp.asarray(masks, F32), jnp.asarray(lane_seq.astype(np.float32), F32),
            jnp.asarray(rep.astype(np.float32), BF16))


def _gla_body(gq_ref, gk_ref, gv_ref, gg_ref, misc_ref, s0_ref, wa_hi_ref, wa_lo_ref, ba_ref, ops_ref, masks_ref,
              lane_seq_ref, rep_ref, gnorm_ref, o_ref, sfin_ref, st_sc, *, n_levels):
    tile = GLA_TILE
    step = pl.program_id(1)

    @pl.when(step == 0)
    def _():
        st_sc[...] = s0_ref[0]

    m_hi, m_lo = _split(misc_ref[0])
    x = _dot(m_hi, wa_hi_ref[...]) + _dot(m_hi, wa_lo_ref[...]) + _dot(m_lo, wa_hi_ref[...]) + ba_ref[...]
    la = (jnp.minimum(x, 0.0) - jnp.log(1.0 + jnp.exp(-jnp.abs(x)))) * (1.0 / GLA_GATE_TAU)
    la_hi, la_lo = _split(la)
    cums = _dot(ops_ref[...], la_hi) + _dot(ops_ref[...], la_lo)
    lane_seq = lane_seq_ref[...]
    rep = rep_ref[...]
    multi = lane_seq.shape[1] > GLA_DK

    def spread(a):
        a = a.astype(BF16)
        return (_dot(a, rep) * lane_seq).astype(BF16) if multi else a

    for h in range(GLA_HEADS):
        dk = slice(h * GLA_DK, (h + 1) * GLA_DK)
        dv = slice(h * GLA_DV, (h + 1) * GLA_DV)
        q = gq_ref[0, :, dk] * (GLA_DK ** -0.5)
        k = gk_ref[0, :, dk]
        v = gv_ref[0, :, dv]
        v_b = v.astype(BF16)
        st = st_sc[h]
        since_start = cums[0:tile, dk]
        until_end = cums[tile:2 * tile, dk]
        o = _dot_nt(spread(q * jnp.exp(since_start)), st.astype(BF16))
        a = jnp.zeros((tile, tile), F32)
        for lv in range(n_levels):
            dq = cums[(2 + 2 * lv) * tile:(3 + 2 * lv) * tile, dk]
            dkk = cums[(3 + 2 * lv) * tile:(4 + 2 * lv) * tile, dk]
            a = a + _dot_nt((q * jnp.exp(dq)).astype(BF16), (k * jnp.exp(dkk)).astype(BF16)) * masks_ref[lv]
        o = o + _dot(a.astype(BF16), v_b) + jnp.sum(q * k, axis=-1, keepdims=True) * v
        la_h = la[:, dk]
        if multi:
            total = jnp.sum(_dot2(la_h, rep) * lane_seq, axis=0, keepdims=True)
        else:
            total = jnp.sum(la_h, axis=0, keepdims=True)
        st_new = st * jnp.exp(total) + _dot(v.T.astype(BF16), spread(k * jnp.exp(until_end)))
        st_sc[h] = st_new
        ms = jnp.mean(o * o, axis=-1, keepdims=True)
        o_ref[0, :, dv] = o * lax.rsqrt(ms + EPS) * gnorm_ref[...] * _silu(gg_ref[0, :, dv])

    @pl.when(step == pl.num_programs(1) - 1)
    def _():
        sfin_ref[0] = st_sc[...]


def _gla(gq, gk, gv, gg, misc, s0_t, wa_hi, wa_lo, ba, gnorm, n_seq):
    groups, rows, _ = gq.shape
    steps = rows // GLA_TILE
    ops, masks, lane_seq, rep = _gla_constants(n_seq)
    n_levels = masks.shape[0]
    tile = lambda g, s: (g, s, 0)
    c2 = lambda g, s: (0, 0)
    c3 = lambda g, s: (0, 0, 0)
    st_block = (1, GLA_HEADS, GLA_DV, n_seq * GLA_DK)
    st_map = lambda g, s: (g, 0, 0, 0)
    return pl.pallas_call(
        functools.partial(_gla_body, n_levels=n_levels),
        grid=(groups, steps),
        in_specs=[
            pl.BlockSpec((1, GLA_TILE, GLA_QK_WIDTH), tile), pl.BlockSpec((1, GLA_TILE, GLA_QK_WIDTH), tile),
            pl.BlockSpec((1, GLA_TILE, GLA_WIDTH), tile), pl.BlockSpec((1, GLA_TILE, GLA_WIDTH), tile),
            pl.BlockSpec((1, GLA_TILE, MISC_WIDTH), tile),
            pl.BlockSpec(st_block, st_map),
            pl.BlockSpec(wa_hi.shape, c2), pl.BlockSpec(wa_lo.shape, c2), pl.BlockSpec(ba.shape, c2),
            pl.BlockSpec(ops.shape, c2), pl.BlockSpec(masks.shape, c3),
            pl.BlockSpec(lane_seq.shape, c2), pl.BlockSpec(rep.shape, c2), pl.BlockSpec(gnorm.shape, c2),
        ],
        out_specs=[pl.BlockSpec((1, GLA_TILE, GLA_WIDTH), tile), pl.BlockSpec(st_block, st_map)],
        out_shape=[jax.ShapeDtypeStruct((groups, rows, GLA_WIDTH), F32),
                   jax.ShapeDtypeStruct((groups,) + st_block[1:], F32)],
        scratch_shapes=[pltpu.VMEM(st_block[1:], F32)],
        compiler_params=pltpu.CompilerParams(dimension_semantics=("parallel", "arbitrary"),
                                             vmem_limit_bytes=VMEM_LIMIT),
        name="gla",
    )(gq, gk, gv, gg, misc, s0_t, wa_hi, wa_lo, ba, ops, masks, lane_seq, rep, gnorm)


def _out_body(x_ref, oc_ref, os_ref, ow_ref, og_ref, misc_ref, gex_ref, wout_ref, gffn_ref, wa_ref, wb_ref, wdown_ref,
              y_ref, hn_sc):
    c = pl.program_id(1)

    @pl.when(c == 0)
    def _():
        gates = 1.0 / (1.0 + jnp.exp(-misc_ref[...]))
        branch = (oc_ref, os_ref, ow_ref)
        o_nsa = None
        for i in range(3):
            g_i = _dot2(gates, gex_ref[i])
            term = g_i * branch[i][...]
            o_nsa = term if o_nsa is None else o_nsa + term
        x1 = (x_ref[...] + _dot(o_nsa.astype(BF16), wout_ref[0:NSA_WIDTH, :])
              + _dot(og_ref[...].astype(BF16), wout_ref[NSA_WIDTH:NSA_WIDTH + GLA_WIDTH, :]))
        ms = jnp.mean(x1 * x1, axis=-1, keepdims=True)
        hn_sc[...] = (x1 * lax.rsqrt(ms + EPS) * gffn_ref[...]).astype(BF16)
        y_ref[...] = x1

    hn = hn_sc[...]
    act = _silu(_dot(hn, wa_ref[...])) * _dot(hn, wb_ref[...])
    y_ref[...] += _dot(act.astype(BF16), wdown_ref[...])


def _out(x2d, oc, osel, ow, og, misc, gex, wout, gffn, wup, wdown):
    n = x2d.shape[0]
    tm = min(OUT_ROW_TILE, n)
    assert n % tm == 0
    n_chunks = FFN_HIDDEN // FFN_CHUNK
    row = lambda i, c: (i, 0)
    c2 = lambda i, c: (0, 0)
    return pl.pallas_call(
        _out_body,
        grid=(n // tm, n_chunks),
        in_specs=[
            pl.BlockSpec((tm, D_MODEL), row),
            pl.BlockSpec((tm, NSA_WIDTH), row), pl.BlockSpec((tm, NSA_WIDTH), row), pl.BlockSpec((tm, NSA_WIDTH), row),
            pl.BlockSpec((tm, GLA_WIDTH), row), pl.BlockSpec((tm, MISC_WIDTH), row),
            pl.BlockSpec(gex.shape, lambda i, c: (0, 0, 0)),
            pl.BlockSpec(wout.shape, c2),
            pl.BlockSpec(gffn.shape, c2),
            pl.BlockSpec((D_MODEL, FFN_CHUNK), lambda i, c: (0, c)),
            pl.BlockSpec((D_MODEL, FFN_CHUNK), lambda i, c: (0, n_chunks + c)),
            pl.BlockSpec((FFN_CHUNK, D_MODEL), lambda i, c: (c, 0)),
        ],
        out_specs=pl.BlockSpec((tm, D_MODEL), row),
        out_shape=jax.ShapeDtypeStruct((n, D_MODEL), F32),
        scratch_shapes=[pltpu.VMEM((tm, D_MODEL), BF16)],
        compiler_params=pltpu.CompilerParams(dimension_semantics=("parallel", "arbitrary"),
                                             vmem_limit_bytes=VMEM_LIMIT),
        name="out",
    )(x2d, oc, osel, ow, og, misc, gex, wout, gffn, wup, wup, wdown)


def _rope_tables(pos):
    half = HEAD_DIM // 2
    inv = ROPE_THETA ** (-jnp.arange(half, dtype=F32) / half)
    ang = pos.astype(F32)[:, None] * inv[None, :]
    cos, sin = jnp.cos(ang), jnp.sin(ang)
    reps = LANES // HEAD_DIM
    return jnp.tile(jnp.concatenate([cos, cos], axis=1), (1, reps)), jnp.tile(jnp.concatenate([-sin, sin], axis=1), (1, reps))


def _overlap_matrix(n_chunks, n_cols):
    start = np.arange(n_chunks)[:, None] * CMP_STRIDE
    j0 = np.arange(n_cols)[None, :] * SEL_BLOCK
    return jnp.asarray(((start < j0 + SEL_BLOCK) & (start + CMP_BLOCK > j0)).astype(np.float32), BF16)


def _expand_matrix(n_tiles, keys):
    key = np.arange(n_tiles)[:, None, None] * keys + np.arange(keys)[None, None, :]
    blk = np.arange(LANES)[None, :, None]
    return jnp.asarray((key // SEL_BLOCK == blk).astype(np.float32), BF16)


def _block_diag_ones(width):
    i = np.arange(width)
    return jnp.asarray((i[:, None] // HEAD_DIM == i[None, :] // HEAD_DIM).astype(np.float32), BF16)


def _gate_expand():
    lane = np.arange(MISC_WIDTH)[None, :, None]
    out = np.arange(NSA_WIDTH)[None, None, :]
    c = np.arange(3)[:, None, None]
    return jnp.asarray(((lane == (out // HEAD_DIM) * 3 + c) & (lane < GATE_COLS)).astype(np.float32), BF16)


def _pad_cols(w, width):
    return jnp.pad(w, ((0, 0), (0, width - w.shape[1])))


def kernel(x_prompt, x_sample, cache_nsa_kv, cache_win_kv, state_gla, page_table, w_norm_mix, w_in, nsa_q_gain,
           nsa_k_gain, cmp_pos_k, cmp_w1_k, cmp_w2_k, cmp_pos_v, cmp_w1_v, cmp_w2_v, gla_w_a2, gla_b_a,
           gla_norm_gain, w_out, w_norm_ffn, w_ffn_up, w_ffn_down):
    b_p, t_p, _ = x_prompt.shape
    b_s, t_s, _ = x_sample.shape
    depth = w_in.shape[0]
    assert depth == 1, "single layer"
    n_pages = page_table.shape[1]
    past_len = n_pages * PAGE_SIZE
    n_pool = cache_nsa_kv.shape[1]
    wb = cache_win_kv.shape[2]
    assert t_p % SEL_KEY_TILE == 0 and t_p >= WINDOW + Q_TILE and (b_p * t_p) % ROW_TILE == 0
    assert (b_s * t_s) % ROW_TILE == 0 and GLA_TILE % t_s == 0 and b_s % (GLA_TILE // t_s) == 0
    assert n_pages % PAGES_PER_STEP == 0 and wb == WINDOW and ROW_TILE % t_s == 0
    l = 0

    q_w, kv_w, gate_w, gq_w, gk_w, gv_w, alow_w, gg_w = jnp.split(
        w_in[l], np.cumsum([NSA_WIDTH, 6 * KV_WIDTH, GATE_COLS, GLA_QK_WIDTH, GLA_QK_WIDTH, GLA_WIDTH,
                            GLA_GATE_RANK])[:].tolist(), axis=1)
    misc_w = _pad_cols(jnp.concatenate([gate_w, alow_w], axis=1), MISC_WIDTH)
    w_perm = jnp.concatenate([q_w, kv_w, gq_w, gk_w, gv_w, gg_w, misc_w], axis=1).astype(BF16)
    gmix = w_norm_mix[l][None, :]
    qg = jnp.tile(nsa_q_gain[l], NSA_HEADS)[None, :]
    kg = [jnp.tile(nsa_k_gain[l, i], NSA_KV_HEADS)[None, :] for i in range(3)]
    ones512 = _block_diag_ones(NSA_WIDTH)
    ones128 = _block_diag_ones(LANES)
    wa = jnp.zeros((MISC_WIDTH, GLA_QK_WIDTH), F32).at[GATE_COLS:GATE_COLS + GLA_GATE_RANK].set(gla_w_a2[l])
    wa_hi, wa_lo = _split(wa)
    ba = gla_b_a[l][None, :]
    gnorm = gla_norm_gain[l][None, :]
    wk1 = _cmp1_weights(cmp_w1_k[l])
    wv1 = _cmp1_weights(cmp_w1_v[l])
    eye2 = jnp.eye(NSA_KV_HEADS, dtype=F32)
    w2k_bd = jnp.einsum("ed,hg->hegd", cmp_w2_k[l], eye2).reshape(2 * CMP_HIDDEN, KV_WIDTH).astype(BF16)
    w2v_bd = jnp.einsum("ed,hg->hegd", cmp_w2_v[l], eye2).reshape(2 * CMP_HIDDEN, KV_WIDTH).astype(BF16)
    flat = CMP_BLOCK * HEAD_DIM
    posk, posv = cmp_pos_k[l].reshape(1, flat), cmp_pos_v[l].reshape(1, flat)
    w1k_flat, w1v_flat = cmp_w1_k[l].reshape(flat, CMP_HIDDEN), cmp_w1_v[l].reshape(flat, CMP_HIDDEN)
    gex = _gate_expand()
    wout = w_out[l].astype(BF16)
    gffn = w_norm_ffn[l][None, :]
    wup = w_ffn_up[l].astype(BF16)
    wdown = w_ffn_down[l].astype(BF16)

    cos_p, sin_p = _rope_tables(jnp.arange(t_p))
    xp2 = x_prompt.reshape(b_p * t_p, D_MODEL)
    (qn, qr, rows_t, win_t, kvb, gq, gk, gv, gg, misc) = _proj(
        xp2, gmix, w_perm, cos_p, sin_p, qg, kg[1], kg[2], ones512, t_p // ROW_TILE, seq_len=t_p)
    g_p = _cmp1_prompt(rows_t.reshape(b_p, 4, KV_WIDTH, t_p), wk1, wv1)
    kc_p, vc_p = _cmp2(g_p, posk, w1k_flat, posv, w1v_flat, w2k_bd, w2v_bd, kg[0], ones128)
    ov_p = _overlap_matrix(t_p // CMP_STRIDE, LANES)
    ex_p = _expand_matrix(t_p // SEL_KEY_TILE, SEL_KEY_TILE)
    oc, osel, ow = _nsa_prompt(qn.reshape(b_p, t_p, -1), qr.reshape(b_p, t_p, -1), kc_p, vc_p,
                               kvb.reshape(b_p, t_p, -1), ov_p, ex_p)
    s0_p = jnp.zeros((b_p, GLA_HEADS, GLA_DV, GLA_DK), F32)
    r3 = lambda a, g=b_p: a.reshape(g, -1, a.shape[-1])
    og, sfin_p = _gla(r3(gq), r3(gk), r3(gv), r3(gg), r3(misc), s0_p, wa_hi, wa_lo, ba, gnorm, 1)
    y_p = _out(xp2, oc.reshape(b_p * t_p, -1), osel.reshape(b_p * t_p, -1), ow.reshape(b_p * t_p, -1),
               og.reshape(b_p * t_p, -1), misc, gex, wout, gffn, wup, wdown)
    out_rows_p = rows_t.reshape(b_p, 4, NSA_KV_HEADS, HEAD_DIM, t_p).transpose(0, 4, 1, 2, 3)[None]
    wlen = min(WINDOW, t_p)
    out_win_p = win_t[:, :, t_p - wlen:].reshape(b_p, 2, NSA_KV_HEADS, HEAD_DIM, wlen).transpose(0, 4, 1, 2, 3)[None]
    out_gla_p = jnp.swapaxes(sfin_p, 2, 3)[None].astype(state_gla.dtype)

    pos_s = past_len + jnp.arange(t_s)
    cos_s, sin_s = _rope_tables(pos_s)
    cos_s = jnp.tile(cos_s, (ROW_TILE // t_s, 1))
    sin_s = jnp.tile(sin_s, (ROW_TILE // t_s, 1))
    xs2 = x_sample.reshape(b_s * t_s, D_MODEL)
    (qn_s, qr_s, rows_s, win_s, kvb_s, gq_s, gk_s, gv_s, gg_s, misc_s) = _proj(
        xs2, gmix, w_perm, cos_s, sin_s, qg, kg[1], kg[2], ones512, 1)
    cache_t = cache_nsa_kv[l].transpose(0, 2, 3, 4, 1).reshape(n_pool, 4, KV_WIDTH, PAGE_SIZE)
    g_s = _cmp1_sample(cache_t, page_table, wk1, wv1)
    kc_s, vc_s = _cmp2(g_s, posk, w1k_flat, posv, w1v_flat, w2k_bd, w2v_bd, kg[0], ones128)
    n_sel_s = -(-(past_len + t_s) // SEL_BLOCK)
    sel_cols = -(-n_sel_s // LANES) * LANES
    ov_s = _overlap_matrix(past_len // CMP_STRIDE, sel_cols)
    seq3 = lambda a: a.reshape(b_s, t_s, a.shape[-1])
    oc_s, sel_s = _nsa_sample_cmp(seq3(qn_s), kc_s, vc_s, ov_s, past_len)
    steps = n_pages // PAGES_PER_STEP
    blocks_per_step = PAGES_PER_STEP * PAGE_SIZE // SEL_BLOCK
    selm = sel_s[:, :, :steps * blocks_per_step].reshape(b_s, NSA_KV_HEADS, 1, t_s, steps, blocks_per_step)
    selm = jnp.broadcast_to(selm, (b_s, NSA_KV_HEADS, NSA_GROUP, t_s, steps, blocks_per_step))
    selm = selm.transpose(0, 4, 1, 2, 3, 5).reshape(b_s, steps, NSA_KV_HEADS * NSA_GROUP * t_s, blocks_per_step)
    selm = jnp.pad(selm, ((0, 0), (0, 0), (0, 0), (0, LANES - blocks_per_step)))
    ex_s = _expand_matrix(1, PAGES_PER_STEP * PAGE_SIZE)
    os_s = _nsa_sample_sel(cache_t, page_table, seq3(qr_s), selm, seq3(kvb_s), ex_s)
    cache_win_t = cache_win_kv[l].transpose(0, 2, 3, 4, 1).reshape(b_s, 2 * KV_WIDTH, wb)
    ow_s, new_win_t = _nsa_sample_win(seq3(qr_s), cache_win_t, seq3(win_s))
    n_seq = GLA_TILE // t_s
    groups = b_s // n_seq
    s0_s = state_gla[l].astype(F32).reshape(groups, n_seq, GLA_HEADS, GLA_DK, GLA_DV)
    s0_s = s0_s.transpose(0, 2, 4, 1, 3).reshape(groups, GLA_HEADS, GLA_DV, n_seq * GLA_DK)
    rg = lambda a: a.reshape(groups, GLA_TILE, a.shape[-1])
    og_s, sfin_s = _gla(rg(gq_s), rg(gk_s), rg(gv_s), rg(gg_s), rg(misc_s), s0_s, wa_hi, wa_lo, ba, gnorm, n_seq)
    y_s = _out(xs2, oc_s.reshape(b_s * t_s, -1), os_s.reshape(b_s * t_s, -1), ow_s.reshape(b_s * t_s, -1),
               og_s.reshape(b_s * t_s, -1), misc_s, gex, wout, gffn, wup, wdown)
    out_rows_s = rows_s.reshape(1, b_s, t_s, 4, NSA_KV_HEADS, HEAD_DIM)
    out_win_s = new_win_t.reshape(b_s, 2, NSA_KV_HEADS, HEAD_DIM, wb).transpose(0, 4, 1, 2, 3)[None]
    out_gla_s = sfin_s.reshape(groups, GLA_HEADS, GLA_DV, n_seq, GLA_DK).transpose(0, 3, 1, 4, 2)
    out_gla_s = out_gla_s.reshape(1, b_s, GLA_HEADS, GLA_DK, GLA_DV).astype(state_gla.dtype)

    return (y_p.reshape(b_p, t_p, D_MODEL), y_s.reshape(b_s, t_s, D_MODEL), out_rows_p, out_win_p, out_gla_p,
            out_rows_s, out_win_s, out_gla_s)
```

```python
import functools
import math

import numpy as np
import jax
import jax.numpy as jnp
from jax import lax
from jax.experimental import pallas as pl
from jax.experimental.pallas import tpu as pltpu

F32 = jnp.float32
BF16 = jnp.bfloat16

D_MODEL = 1024
PAGE_SIZE = 128
NSA_HEADS = 8
NSA_KV_HEADS = 2
NSA_GROUP = NSA_HEADS // NSA_KV_HEADS
HEAD_DIM = 64
NSA_WIDTH = NSA_HEADS * HEAD_DIM
KV_WIDTH = NSA_KV_HEADS * HEAD_DIM
CMP_BLOCK = 32
CMP_STRIDE = 16
CMP_HIDDEN = 2 * HEAD_DIM
SEL_BLOCK = 64
SEL_TOPK = 16
WINDOW = 512
FORCE_SCORE = 1.0e4
GLA_HEADS = 4
GLA_DK = 64
GLA_DV = 128
GLA_WIDTH = GLA_HEADS * GLA_DV
GLA_QK_WIDTH = GLA_HEADS * GLA_DK
GLA_GATE_RANK = 16
GLA_GATE_TAU = 16.0
FFN_HIDDEN = -(-8 * D_MODEL // (3 * 256)) * 256
ROPE_THETA = 10000.0
EPS = 1e-6
NEG = -1.0e30

LANES = 128
MISC_WIDTH = LANES
GATE_COLS = 3 * NSA_HEADS
PROJ_WIDTH = NSA_WIDTH + 6 * KV_WIDTH + 2 * GLA_QK_WIDTH + 2 * GLA_WIDTH + MISC_WIDTH
ROW_TILE = 512
OUT_ROW_TILE = 1024
Q_TILE = 128
SEL_KEY_TILE = 512
GLA_TILE = 128
GLA_GROUPS_PER_STEP = 2
PAGES_PER_STEP = 16
FFN_CHUNK = 256
VMEM_LIMIT = 56 * 1024 * 1024


def _dot(a, b):
    return jnp.dot(a, b, preferred_element_type=F32)


def _dot_nt(a, b):
    return lax.dot_general(a, b, (((1,), (1,)), ((), ())), preferred_element_type=F32)


def _split(x):
    hi = x.astype(BF16)
    lo = (x - hi.astype(F32)).astype(BF16)
    return hi, lo


def _dot2(x, w_bf16):
    hi, lo = _split(x)
    return _dot(hi, w_bf16) + _dot(lo, w_bf16)


def _silu(x):
    return x * (1.0 / (1.0 + jnp.exp(-x)))


def _head_norm(v, ones_bd, gain):
    ss = _dot((v * v).astype(BF16), ones_bd)
    return v * lax.rsqrt(ss * (1.0 / HEAD_DIM) + EPS) * gain


def _rope(v, cos, sin_signed):
    width = v.shape[-1]
    reps = width // LANES
    if reps > 1:
        cos = jnp.concatenate([cos] * reps, axis=1)
        sin_signed = jnp.concatenate([sin_signed] * reps, axis=1)
    lane = lax.broadcasted_iota(jnp.int32, v.shape, 1)
    first_half = (lane & (HEAD_DIM - 1)) < (HEAD_DIM // 2)
    rot = jnp.where(first_half, pltpu.roll(v, width - HEAD_DIM // 2, 1), pltpu.roll(v, HEAD_DIM // 2, 1))
    return v * cos + rot * sin_signed


def _softmax_rows(s):
    m = jnp.max(s, axis=-1, keepdims=True)
    m = jnp.where(m == -jnp.inf, 0.0, m)
    e = jnp.exp(s - m)
    return e / jnp.maximum(jnp.sum(e, axis=-1, keepdims=True), 1e-30)


def _proj_body(x_ref, gmix_ref, w_ref, cos_ref, sin_ref, qg_ref, kg1_ref, kg2_ref, ones_ref,
               qn_ref, qr_ref, rows_ref, win_ref, kvb_ref, vt_ref, gq_ref, gk_ref, gv_ref, gg_ref, misc_ref, *,
               rows_minor):
    x = x_ref[...]
    ms = jnp.mean(x * x, axis=-1, keepdims=True)
    h = (x * lax.rsqrt(ms + EPS) * gmix_ref[...]).astype(BF16)
    cos = cos_ref[...]
    sin = sin_ref[...]
    ones512 = ones_ref[...]
    ones128 = ones512[:LANES, :LANES]
    scale = HEAD_DIM ** -0.5

    off = 0
    q = _dot(h, w_ref[:, off:off + NSA_WIDTH])
    off += NSA_WIDTH
    qn = _head_norm(q, ones512, qg_ref[...])
    qr = _rope(qn, cos, sin)
    if rows_minor:
        qn_ref[0] = (qn * scale).T.astype(BF16)
        qr_ref[0] = (qr * scale).T.astype(BF16)
    else:
        qn_ref[...] = (qn * scale).astype(BF16)
        qr_ref[...] = (qr * scale).astype(BF16)

    kv = _dot(h, w_ref[:, off:off + 6 * KV_WIDTH])
    off += 6 * KV_WIDTH
    k_slc = _rope(_head_norm(kv[:, 2 * KV_WIDTH:3 * KV_WIDTH], ones128, kg1_ref[...]), cos, sin)
    k_win = _rope(_head_norm(kv[:, 4 * KV_WIDTH:5 * KV_WIDTH], ones128, kg2_ref[...]), cos, sin)
    v_slc = kv[:, 3 * KV_WIDTH:4 * KV_WIDTH]
    v_win = kv[:, 5 * KV_WIDTH:6 * KV_WIDTH]
    v_slc_t = v_slc.T
    v_win_t = v_win.T
    for j in range(v_slc.shape[0] // LANES):
        vt_ref[j, 0:KV_WIDTH, :] = v_slc_t[:, j * LANES:(j + 1) * LANES].astype(BF16)
        vt_ref[j, KV_WIDTH:2 * KV_WIDTH, :] = v_win_t[:, j * LANES:(j + 1) * LANES].astype(BF16)
    if rows_minor:
        rows_ref[0, 0:2 * KV_WIDTH, :] = kv[:, 0:2 * KV_WIDTH].T
        rows_ref[0, 2 * KV_WIDTH:3 * KV_WIDTH, :] = k_slc.T
        rows_ref[0, 3 * KV_WIDTH:4 * KV_WIDTH, :] = v_slc_t
        win_ref[0, 0:KV_WIDTH, :] = k_win.T
        win_ref[0, KV_WIDTH:2 * KV_WIDTH, :] = v_win_t
    else:
        rows_ref[:, 0:2 * KV_WIDTH] = kv[:, 0:2 * KV_WIDTH]
        rows_ref[:, 2 * KV_WIDTH:3 * KV_WIDTH] = k_slc
        rows_ref[:, 3 * KV_WIDTH:4 * KV_WIDTH] = v_slc
        win_ref[:, 0:KV_WIDTH] = k_win
        win_ref[:, KV_WIDTH:2 * KV_WIDTH] = v_win
    kvb_ref[:, 0:KV_WIDTH] = k_slc.astype(BF16)
    kvb_ref[:, KV_WIDTH:2 * KV_WIDTH] = v_slc.astype(BF16)
    kvb_ref[:, 2 * KV_WIDTH:3 * KV_WIDTH] = k_win.astype(BF16)
    kvb_ref[:, 3 * KV_WIDTH:4 * KV_WIDTH] = v_win.astype(BF16)

    gq_ref[...] = _dot(h, w_ref[:, off:off + GLA_QK_WIDTH])
    off += GLA_QK_WIDTH
    gk_ref[...] = _dot(h, w_ref[:, off:off + GLA_QK_WIDTH])
    off += GLA_QK_WIDTH
    gv_ref[...] = _dot(h, w_ref[:, off:off + GLA_WIDTH])
    off += GLA_WIDTH
    gg_ref[...] = _dot(h, w_ref[:, off:off + GLA_WIDTH])
    off += GLA_WIDTH
    misc_ref[...] = _dot(h, w_ref[:, off:off + MISC_WIDTH])


def _proj(x2d, gmix, w_perm, cos_t, sin_t, qg, kg1, kg2, ones512, table_blocks, seq_len=None):
    n = x2d.shape[0]
    tm = ROW_TILE
    row = lambda i: (i, 0)
    const = lambda i: (0, 0)
    tab = lambda i: (i % table_blocks, 0)
    widths = [(NSA_WIDTH, BF16), (NSA_WIDTH, BF16), (4 * KV_WIDTH, F32), (2 * KV_WIDTH, F32), (4 * KV_WIDTH, BF16), None,
              (GLA_QK_WIDTH, F32), (GLA_QK_WIDTH, F32), (GLA_WIDTH, F32), (GLA_WIDTH, F32), (MISC_WIDTH, F32)]
    out_specs = [None if w is None else pl.BlockSpec((tm, w[0]), row) for w in widths]
    out_shape = [None if w is None else jax.ShapeDtypeStruct((n, w[0]), w[1]) for w in widths]
    out_specs[5] = pl.BlockSpec((tm // LANES, 2 * KV_WIDTH, LANES), lambda i: (i, 0, 0))
    out_shape[5] = jax.ShapeDtypeStruct((n // LANES, 2 * KV_WIDTH, LANES), BF16)
    if seq_len is not None:
        tiles = seq_len // tm
        for i in (0, 1, 2, 3):
            w, dt = widths[i]
            out_specs[i] = pl.BlockSpec((1, w, tm), lambda i: (i // tiles, 0, i % tiles))
            out_shape[i] = jax.ShapeDtypeStruct((n // seq_len, w, seq_len), dt)
    return pl.pallas_call(
        functools.partial(_proj_body, rows_minor=seq_len is not None),
        grid=(n // tm,),
        in_specs=[
            pl.BlockSpec((tm, D_MODEL), row),
            pl.BlockSpec((1, D_MODEL), const),
            pl.BlockSpec((D_MODEL, PROJ_WIDTH), const),
            pl.BlockSpec((tm, LANES), tab),
            pl.BlockSpec((tm, LANES), tab),
            pl.BlockSpec((1, NSA_WIDTH), const),
            pl.BlockSpec((1, KV_WIDTH), const),
            pl.BlockSpec((1, KV_WIDTH), const),
            pl.BlockSpec((NSA_WIDTH, NSA_WIDTH), const),
        ],
        out_specs=out_specs,
        out_shape=out_shape,
        compiler_params=pltpu.CompilerParams(dimension_semantics=("parallel",), vmem_limit_bytes=VMEM_LIMIT),
        name="proj",
    )(x2d, gmix, w_perm, cos_t, sin_t, qg, kg1, kg2, ones512)


def _cmp1_body(*refs, n_src, n_prefetch):
    refs = refs[n_prefetch:]
    n_refs = len(refs) - 5
    src = refs[:n_refs]
    wk_ref, wv_ref, out_ref, xk_sc, xv_sc = refs[n_refs:]
    for i in range(n_src):
        rows = slice(i * PAGE_SIZE, (i + 1) * PAGE_SIZE)
        if n_refs == 1:
            k_t, v_t = src[0][0, :, rows], src[0][1, :, rows]
        else:
            k_t, v_t = src[i][0], src[i][1]
        xk_sc[rows, :] = k_t.T
        xv_sc[rows, :] = v_t.T
    n_chunks = n_src * PAGE_SIZE // CMP_STRIDE

    def rows_at(ref, s):
        return ref[pl.ds(s, n_chunks, stride=CMP_STRIDE), :]

    acc_k = None
    acc_v = None
    for sp in range(CMP_STRIDE // 2):
        lk = jnp.concatenate([rows_at(xk_sc, 2 * sp), rows_at(xk_sc, 2 * sp + 1)], axis=1).astype(BF16)
        lv = jnp.concatenate([rows_at(xv_sc, 2 * sp), rows_at(xv_sc, 2 * sp + 1)], axis=1).astype(BF16)
        dk = _dot(lk, wk_ref[sp])
        dv = _dot(lv, wv_ref[sp])
        acc_k = dk if acc_k is None else acc_k + dk
        acc_v = dv if acc_v is None else acc_v + dv
    hid = 4 * CMP_HIDDEN
    out_ref[0, :, 0:hid] = acc_k
    out_ref[0, :, hid:2 * hid] = acc_v


def _cmp1_weights(w1):
    w = w1.reshape(2, CMP_STRIDE // 2, 2, HEAD_DIM, CMP_HIDDEN)
    eye = jnp.eye(NSA_KV_HEADS, dtype=w1.dtype)
    big = jnp.einsum("jpsde,hg->pshdgje", w, eye)
    return big.reshape(CMP_STRIDE // 2, 4 * HEAD_DIM, 4 * CMP_HIDDEN).astype(BF16)


def _cmp1_scratch():
    rows = PAGES_PER_STEP * PAGE_SIZE
    return [pltpu.VMEM((rows, KV_WIDTH), F32), pltpu.VMEM((rows, KV_WIDTH), F32)]


def _cmp1_prompt(rows_t, wk, wv):
    b, _, _, t = rows_t.shape
    span = PAGES_PER_STEP * PAGE_SIZE
    chunks = span // CMP_STRIDE
    wspec = pl.BlockSpec(wk.shape, lambda i, g: (0, 0, 0))
    return pl.pallas_call(
        functools.partial(_cmp1_body, n_src=PAGES_PER_STEP, n_prefetch=0),
        grid=(b, t // span),
        in_specs=[pl.BlockSpec((None, 2, KV_WIDTH, span), lambda i, g: (i, 0, 0, g)), wspec, wspec],
        out_specs=pl.BlockSpec((1, chunks, 8 * CMP_HIDDEN), lambda i, g: (i, g, 0)),
        out_shape=jax.ShapeDtypeStruct((b, t // CMP_STRIDE, 8 * CMP_HIDDEN), F32),
        scratch_shapes=_cmp1_scratch(),
        compiler_params=pltpu.CompilerParams(dimension_semantics=("parallel", "parallel"),
                                             vmem_limit_bytes=VMEM_LIMIT),
        name="cmp1_prompt",
    )(rows_t, wk, wv)


def _page_specs(kind_block):
    def spec(i):
        return pl.BlockSpec((None, 2, KV_WIDTH, PAGE_SIZE),
                            lambda b, g, pt: (pt[b, g * PAGES_PER_STEP + i], kind_block, 0, 0))
    return [spec(i) for i in range(PAGES_PER_STEP)]


def _cmp1_sample(cache_t, page_table, wk, wv):
    bs, n_pages = page_table.shape
    steps = n_pages // PAGES_PER_STEP
    chunks_per_step = PAGES_PER_STEP * PAGE_SIZE // CMP_STRIDE
    wspec = pl.BlockSpec(wk.shape, lambda b, g, pt: (0, 0, 0))
    grid_spec = pltpu.PrefetchScalarGridSpec(
        num_scalar_prefetch=1,
        grid=(bs, steps),
        in_specs=_page_specs(0) + [wspec, wspec],
        out_specs=pl.BlockSpec((1, chunks_per_step, 8 * CMP_HIDDEN), lambda b, g, pt: (b, g, 0)),
        scratch_shapes=_cmp1_scratch(),
    )
    return pl.pallas_call(
        functools.partial(_cmp1_body, n_src=PAGES_PER_STEP, n_prefetch=1),
        grid_spec=grid_spec,
        out_shape=jax.ShapeDtypeStruct((bs, steps * chunks_per_step, 8 * CMP_HIDDEN), F32),
        compiler_params=pltpu.CompilerParams(dimension_semantics=("parallel", "parallel"),
                                             vmem_limit_bytes=VMEM_LIMIT),
        name="cmp1_sample",
    )(page_table, *([cache_t] * PAGES_PER_STEP), wk, wv)


def _cmp2_body(g_ref, posk_ref, w1k_ref, posv_ref, w1v_ref, w2k_ref, w2v_ref, kg0_ref, ones_ref, kc_ref, vc_ref):
    g = g_ref[0]
    chunks = g.shape[0]
    hd = CMP_HIDDEN

    def pos_term(pos_ref, w1_ref):
        p = jnp.broadcast_to(pos_ref[...], (8, pos_ref.shape[1]))
        p_hi, p_lo = _split(p)
        w_hi, w_lo = _split(w1_ref[...])
        return (_dot(p_hi, w_hi) + _dot(p_hi, w_lo) + _dot(p_lo, w_hi))[0:1, :]

    def hidden(base, pos):
        out = []
        for h in range(NSA_KV_HEADS):
            j0 = g[:, base + 2 * h * hd: base + (2 * h + 1) * hd]
            j1 = g[:, base + (2 * h + 1) * hd: base + (2 * h + 2) * hd]
            out.append(j0 + pltpu.roll(j1, chunks - 1, 0) + pos)
        return _silu(jnp.concatenate(out, axis=1)).astype(BF16)

    kc = _dot(hidden(0, pos_term(posk_ref, w1k_ref)), w2k_ref[...])
    kc_ref[0] = _head_norm(kc, ones_ref[...], kg0_ref[...])
    vc_ref[0] = _dot(hidden(4 * hd, pos_term(posv_ref, w1v_ref)), w2v_ref[...])


def _cmp2(g, posk, w1k, posv, w1v, w2k_bd, w2v_bd, kg0, ones128):
    b, chunks, _ = g.shape
    c2 = lambda i: (0, 0)
    seq = lambda i: (i, 0, 0)
    flat = CMP_BLOCK * HEAD_DIM
    return pl.pallas_call(
        _cmp2_body,
        grid=(b,),
        in_specs=[
            pl.BlockSpec((1, chunks, 8 * CMP_HIDDEN), seq),
            pl.BlockSpec((1, flat), c2), pl.BlockSpec((flat, CMP_HIDDEN), c2),
            pl.BlockSpec((1, flat), c2), pl.BlockSpec((flat, CMP_HIDDEN), c2),
            pl.BlockSpec((2 * CMP_HIDDEN, KV_WIDTH), c2), pl.BlockSpec((2 * CMP_HIDDEN, KV_WIDTH), c2),
            pl.BlockSpec((1, KV_WIDTH), c2), pl.BlockSpec((LANES, LANES), c2),
        ],
        out_specs=[pl.BlockSpec((1, chunks, KV_WIDTH), seq)] * 2,
        out_shape=[jax.ShapeDtypeStruct((b, chunks, KV_WIDTH), F32)] * 2,
        compiler_params=pltpu.CompilerParams(dimension_semantics=("parallel",), vmem_limit_bytes=VMEM_LIMIT),
        name="cmp2",
    )(g, posk, w1k, posv, w1v, w2k_bd, w2v_bd, kg0, ones128)


def _stack_heads(q, kvh):
    base = kvh * NSA_GROUP * HEAD_DIM
    return jnp.concatenate([q[:, base + g * HEAD_DIM: base + (g + 1) * HEAD_DIM] for g in range(NSA_GROUP)], axis=0)


def _unstack_heads(o, rows):
    return jnp.concatenate([o[g * rows:(g + 1) * rows] for g in range(NSA_GROUP)], axis=1)


def _group_sum(p, rows):
    out = p[0:rows]
    for g in range(1, NSA_GROUP):
        out = out + p[g * rows:(g + 1) * rows]
    return out


def _select_blocks(imp, qpos, n_blocks):
    blk = lax.broadcasted_iota(jnp.int32, imp.shape, 1)
    visible = blk * SEL_BLOCK <= qpos
    forced = jnp.logical_or(blk == 0, blk == (qpos >> int(math.log2(SEL_BLOCK))))
    score = jnp.where(forced, FORCE_SCORE, jnp.where(visible, imp, -jnp.inf))
    beaten = jnp.zeros(imp.shape, F32)
    for i in range(n_blocks):
        ci = score[:, i:i + 1]
        earlier = jnp.where(blk > i, 1.0, 0.0)
        beaten = beaten + jnp.where(ci > score, 1.0, jnp.where(ci == score, earlier, 0.0))
    return jnp.where(beaten < float(SEL_TOPK), jnp.where(score > -jnp.inf, 1.0, 0.0), 0.0)


def _softmax_cols(s):
    m = jnp.max(s, axis=0, keepdims=True)
    m = jnp.where(m == -jnp.inf, 0.0, m)
    e = jnp.exp(s - m)
    return e / jnp.maximum(jnp.sum(e, axis=0, keepdims=True), 1e-30)


def _select_blocks_t(imp_t, qpos, n_blocks):
    blk = lax.broadcasted_iota(jnp.int32, imp_t.shape, 0)
    visible = blk * SEL_BLOCK <= qpos
    forced = jnp.logical_or(blk == 0, blk == (qpos >> int(math.log2(SEL_BLOCK))))
    score = jnp.where(forced, FORCE_SCORE, jnp.where(visible, imp_t, -jnp.inf))
    beaten = jnp.zeros(imp_t.shape, F32)
    for i in range(n_blocks):
        ci = score[i:i + 1, :]
        earlier = jnp.where(blk > i, 1.0, 0.0)
        beaten = beaten + jnp.where(ci > score, 1.0, jnp.where(ci == score, earlier, 0.0))
    return jnp.where(beaten < float(SEL_TOPK), jnp.where(score > -jnp.inf, 1.0, 0.0), 0.0)


def _nsa_prompt_body(qn_ref, qr_ref, kc_ref, vc_ref, kvb_ref, vt_ref, ovt_ref, ext_ref, oc_ref, os_ref, ow_ref, *,
                     seq_len):
    tq = Q_TILE
    tk = SEL_KEY_TILE
    q0 = pl.program_id(1) * tq
    n_cmp = seq_len // CMP_STRIDE - 1
    n_sel = seq_len // SEL_BLOCK
    cols = NSA_GROUP * tq
    qn_t = qn_ref[0]
    qr_t = qr_ref[0]
    kc = kc_ref[0].astype(BF16)
    vc_t = vc_ref[0].T.astype(BF16)
    qpos1 = q0 + lax.broadcasted_iota(jnp.int32, (1, tq), 1)
    qpos4 = jnp.concatenate([qpos1] * NSA_GROUP, axis=1)
    win_tiles = (WINDOW + tq) // LANES
    wstart = pl.multiple_of(jnp.maximum(q0 - WINDOW, 0), tq)
    n_tiles = (q0 + tq + tk - 1) // tk

    def group_q(q_t, kvh):
        base = kvh * NSA_GROUP * HEAD_DIM
        return jnp.concatenate([q_t[base + g * HEAD_DIM: base + (g + 1) * HEAD_DIM, :] for g in range(NSA_GROUP)],
                               axis=1)

    def per_head(o_t):
        return [o_t[:, g * tq:(g + 1) * tq] for g in range(NSA_GROUP)]

    oc, osel, ow = [], [], []
    for kvh in range(NSA_KV_HEADS):
        lo, hi = kvh * HEAD_DIM, (kvh + 1) * HEAD_DIM
        s = _dot(kc[:, lo:hi], group_q(qn_t, kvh))
        n_idx = lax.broadcasted_iota(jnp.int32, s.shape, 0)
        vis = jnp.logical_and(n_idx * CMP_STRIDE + (CMP_BLOCK - 1) <= qpos4, n_idx < n_cmp)
        p = _softmax_cols(jnp.where(vis, s, -jnp.inf))
        oc += per_head(_dot(vc_t[lo:hi, :], p.astype(BF16)))
        p_sum = p[:, 0:tq]
        for g in range(1, NSA_GROUP):
            p_sum = p_sum + p[:, g * tq:(g + 1) * tq]
        p_hi, p_lo = _split(p_sum)
        imp_t = _dot(ovt_ref[...], p_hi) + _dot(ovt_ref[...], p_lo)
        sel_t = _select_blocks_t(imp_t[0:n_sel, :], qpos1, n_sel)
        sel_t = jnp.concatenate([sel_t, jnp.zeros((LANES - n_sel, tq), F32)], axis=0).astype(BF16)

        qs_t = group_q(qr_t, kvh)

        def sel_step(kt, carry, qs_t=qs_t, sel_t=sel_t, lo=lo, hi=hi):
            m_i, l_i, acc = carry
            k0 = pl.multiple_of(kt * tk, tk)
            k_blk = kvb_ref[0, pl.ds(k0, tk), lo:hi]
            v_t = jnp.concatenate([vt_ref[0, kt * (tk // LANES) + j, lo:hi, :] for j in range(tk // LANES)], axis=1)
            chosen = _dot(ext_ref[kt], sel_t)
            kpos = k0 + lax.broadcasted_iota(jnp.int32, (tk, tq), 0)
            bias = jnp.where(jnp.logical_and(chosen > 0.5, kpos <= qpos1), 0.0, -jnp.inf)
            sc = _dot(k_blk, qs_t) + jnp.concatenate([bias] * NSA_GROUP, axis=1)
            m_new = jnp.maximum(m_i, jnp.max(sc, axis=0, keepdims=True))
            alpha = jnp.exp(m_i - m_new)
            pe = jnp.exp(sc - m_new)
            l_new = alpha * l_i + jnp.sum(pe, axis=0, keepdims=True)
            acc_new = alpha * acc + _dot(v_t, pe.astype(BF16))
            return m_new, l_new, acc_new

        init = (jnp.full((1, cols), NEG, F32), jnp.zeros((1, cols), F32), jnp.zeros((HEAD_DIM, cols), F32))
        _, l_f, acc_f = lax.fori_loop(0, n_tiles, sel_step, init)
        osel += per_head(acc_f / jnp.maximum(l_f, 1e-30))

        k_w = kvb_ref[0, pl.ds(wstart, win_tiles * LANES), 2 * KV_WIDTH + lo:2 * KV_WIDTH + hi]
        w0 = wstart // LANES
        v_wt = jnp.concatenate([vt_ref[0, w0 + j, KV_WIDTH + lo:KV_WIDTH + hi, :] for j in range(win_tiles)], axis=1)
        sw = _dot(k_w, qs_t)
        rel = qpos4 - (wstart + lax.broadcasted_iota(jnp.int32, sw.shape, 0))
        inside = jnp.logical_and(rel >= 0, rel < WINDOW)
        pw = _softmax_cols(jnp.where(inside, sw, -jnp.inf))
        ow += per_head(_dot(v_wt, pw.astype(BF16)))

    oc_ref[0] = jnp.concatenate(oc, axis=0).T
    os_ref[0] = jnp.concatenate(osel, axis=0).T
    ow_ref[0] = jnp.concatenate(ow, axis=0).T


def _nsa_prompt(qn_t, qr_t, kc, vc, kvb, vt, ovt, ext):
    b, _, t = qn_t.shape
    chunks = kc.shape[1]
    qtile = lambda i, j: (i, 0, j)
    otile = lambda i, j: (i, j, 0)
    seq = lambda i, j: (i, 0, 0)
    out = jax.ShapeDtypeStruct((b, t, NSA_WIDTH), F32)
    return pl.pallas_call(
        functools.partial(_nsa_prompt_body, seq_len=t),
        grid=(b, t // Q_TILE),
        in_specs=[
            pl.BlockSpec((1, NSA_WIDTH, Q_TILE), qtile),
            pl.BlockSpec((1, NSA_WIDTH, Q_TILE), qtile),
            pl.BlockSpec((1, chunks, KV_WIDTH), seq),
            pl.BlockSpec((1, chunks, KV_WIDTH), seq),
            pl.BlockSpec((1, t, 4 * KV_WIDTH), seq),
            pl.BlockSpec((1,) + vt.shape[1:], lambda i, j: (i, 0, 0, 0)),
            pl.BlockSpec(ovt.shape, lambda i, j: (0, 0)),
            pl.BlockSpec(ext.shape, lambda i, j: (0, 0, 0)),
        ],
        out_specs=[pl.BlockSpec((1, Q_TILE, NSA_WIDTH), otile)] * 3,
        out_shape=[out] * 3,
        compiler_params=pltpu.CompilerParams(dimension_semantics=("parallel", "parallel"),
                                             vmem_limit_bytes=VMEM_LIMIT),
        name="nsa_prompt",
    )(qn_t, qr_t, kc, vc, kvb, vt, ovt, ext)


def _nsa_sample_cmp_body(qn_ref, kc_ref, vc_ref, ov_ref, oc_ref, sel_ref, *, past_len, n_new):
    t = n_new
    n_chunks = kc_ref.shape[1]
    n_cmp = (past_len + t) // CMP_STRIDE - 1
    n_sel = -(-(past_len + t) // SEL_BLOCK)
    qn = qn_ref[0]
    kc = kc_ref[0].astype(BF16)
    vc = vc_ref[0].astype(BF16)
    qpos1 = past_len + lax.broadcasted_iota(jnp.int32, (t, 1), 0)
    qpos4 = jnp.concatenate([qpos1] * NSA_GROUP, axis=0)
    oc, sel = [], []
    for kvh in range(NSA_KV_HEADS):
        lo, hi = kvh * HEAD_DIM, (kvh + 1) * HEAD_DIM
        s = _dot_nt(_stack_heads(qn, kvh), kc[:, lo:hi])
        n_idx = lax.broadcasted_iota(jnp.int32, s.shape, 1)
        vis = jnp.logical_and(n_idx * CMP_STRIDE + (CMP_BLOCK - 1) <= qpos4, n_idx < n_cmp)
        p = _softmax_rows(jnp.where(vis, s, -jnp.inf))
        oc.append(_unstack_heads(_dot(p.astype(BF16), vc[:, lo:hi]), t))
        imp = _dot2(_group_sum(p, t), ov_ref[...])
        sel.append(_select_blocks(imp, qpos1, n_sel))
    oc_ref[0] = jnp.concatenate(oc, axis=1)
    sel_ref[0] = jnp.concatenate(sel, axis=0)
    del n_chunks


def _nsa_sample_cmp(qn, kc, vc, ov, past_len):
    bs, t, _ = qn.shape
    chunks = kc.shape[1]
    seq = lambda i: (i, 0, 0)
    return pl.pallas_call(
        functools.partial(_nsa_sample_cmp_body, past_len=past_len, n_new=t),
        grid=(bs,),
        in_specs=[
            pl.BlockSpec((1, t, NSA_WIDTH), seq),
            pl.BlockSpec((1, chunks, KV_WIDTH), seq),
            pl.BlockSpec((1, chunks, KV_WIDTH), seq),
            pl.BlockSpec(ov.shape, lambda i: (0, 0)),
        ],
        out_specs=[pl.BlockSpec((1, t, NSA_WIDTH), seq), pl.BlockSpec((1, NSA_KV_HEADS * t, ov.shape[1]), seq)],
        out_shape=[jax.ShapeDtypeStruct((bs, t, NSA_WIDTH), F32),
                   jax.ShapeDtypeStruct((bs, NSA_KV_HEADS * t, ov.shape[1]), F32)],
        compiler_params=pltpu.CompilerParams(dimension_semantics=("parallel",), vmem_limit_bytes=VMEM_LIMIT),
        name="nsa_sample_cmp",
    )(qn, kc, vc, ov)


def _nsa_sample_sel_body(*refs, n_new):
    pt_ref = refs[0]
    pages = refs[1:1 + PAGES_PER_STEP]
    qr_ref, selm_ref, kvb_ref, ex_ref, os_ref, m_sc, l_sc, acc_sc = refs[1 + PAGES_PER_STEP:]
    del pt_ref
    t = n_new
    rows4 = NSA_GROUP * t
    step = pl.program_id(1)
    last = pl.num_programs(1) - 1
    qr = qr_ref[0]

    @pl.when(step == 0)
    def _():
        m_sc[...] = jnp.full(m_sc.shape, NEG, F32)
        l_sc[...] = jnp.zeros(l_sc.shape, F32)
        acc_sc[...] = jnp.zeros(acc_sc.shape, F32)

    def update(kvh, sc, v_blk, v_transposed):
        m_i, l_i, acc = m_sc[kvh], l_sc[kvh], acc_sc[kvh]
        m_new = jnp.maximum(m_i, jnp.max(sc, axis=-1, keepdims=True))
        alpha = jnp.exp(m_i - m_new)
        pe = jnp.exp(sc - m_new)
        m_sc[kvh] = m_new
        l_sc[kvh] = alpha * l_i + jnp.sum(pe, axis=-1, keepdims=True)
        pe = pe.astype(BF16)
        acc_sc[kvh] = alpha * acc + (_dot_nt(pe, v_blk) if v_transposed else _dot(pe, v_blk))

    for kvh in range(NSA_KV_HEADS):
        lo, hi = kvh * HEAD_DIM, (kvh + 1) * HEAD_DIM
        qs = _stack_heads(qr, kvh)
        k_t = jnp.concatenate([p[0, lo:hi, :] for p in pages], axis=1).astype(BF16)
        v_t = jnp.concatenate([p[1, lo:hi, :] for p in pages], axis=1).astype(BF16)
        chosen = _dot(selm_ref[0, 0, kvh * rows4:(kvh + 1) * rows4, :].astype(BF16), ex_ref[0])
        sc = jnp.where(chosen > 0.5, _dot(qs, k_t), -jnp.inf)
        update(kvh, sc, v_t, True)

    @pl.when(step == last)
    def _():
        pad = jnp.zeros((LANES - t, HEAD_DIM), F32)
        out = []
        for kvh in range(NSA_KV_HEADS):
            lo, hi = kvh * HEAD_DIM, (kvh + 1) * HEAD_DIM
            qs = _stack_heads(qr, kvh)
            k_new = jnp.concatenate([kvb_ref[0, :, lo:hi].astype(F32), pad], axis=0).astype(BF16)
            v_new = jnp.concatenate([kvb_ref[0, :, KV_WIDTH + lo:KV_WIDTH + hi].astype(F32), pad], axis=0).astype(BF16)
            sc = _dot_nt(qs, k_new)
            key_i = lax.broadcasted_iota(jnp.int32, sc.shape, 1)
            q_t = lax.broadcasted_iota(jnp.int32, sc.shape, 0) & (t - 1)
            update(kvh, jnp.where(key_i <= q_t, sc, -jnp.inf), v_new, False)
            out.append(_unstack_heads(acc_sc[kvh] / jnp.maximum(l_sc[kvh], 1e-30), t))
        os_ref[0] = jnp.concatenate(out, axis=1)


def _nsa_sample_sel(cache_t, page_table, qr, selm, kvb, ex):
    bs, n_pages = page_table.shape
    t = qr.shape[1]
    steps = n_pages // PAGES_PER_STEP
    rows = NSA_KV_HEADS * NSA_GROUP * t
    seq = lambda b, g, pt: (b, 0, 0)
    grid_spec = pltpu.PrefetchScalarGridSpec(
        num_scalar_prefetch=1,
        grid=(bs, steps),
        in_specs=_page_specs(1) + [
            pl.BlockSpec((1, t, NSA_WIDTH), seq),
            pl.BlockSpec((1, 1, rows, LANES), lambda b, g, pt: (b, g, 0, 0)),
            pl.BlockSpec((1, t, 4 * KV_WIDTH), seq),
            pl.BlockSpec((1,) + ex.shape[1:], lambda b, g, pt: (0, 0, 0)),
        ],
        out_specs=pl.BlockSpec((1, t, NSA_WIDTH), seq),
        scratch_shapes=[pltpu.VMEM((NSA_KV_HEADS, NSA_GROUP * t, 1), F32),
                        pltpu.VMEM((NSA_KV_HEADS, NSA_GROUP * t, 1), F32),
                        pltpu.VMEM((NSA_KV_HEADS, NSA_GROUP * t, HEAD_DIM), F32)],
    )
    return pl.pallas_call(
        functools.partial(_nsa_sample_sel_body, n_new=t),
        grid_spec=grid_spec,
        out_shape=jax.ShapeDtypeStruct((bs, t, NSA_WIDTH), F32),
        compiler_params=pltpu.CompilerParams(dimension_semantics=("parallel", "arbitrary"),
                                             vmem_limit_bytes=VMEM_LIMIT),
        name="nsa_sample_sel",
    )(page_table, *([cache_t] * PAGES_PER_STEP), qr, selm, kvb, ex)


def _nsa_sample_win_body(qr_ref, cw_ref, nw_ref, ow_ref, neww_ref, *, n_new):
    t = n_new
    wb = cw_ref.shape[2]
    qr = qr_ref[0]
    cw = cw_ref[0]
    nw_t = jnp.concatenate([nw_ref[0], jnp.zeros((LANES - t, 2 * KV_WIDTH), F32)], axis=0).T
    shifted = pltpu.roll(cw, wb - t, 1)
    lane = lax.broadcasted_iota(jnp.int32, (2 * KV_WIDTH, LANES), 1)
    neww_ref[0, :, 0:wb - LANES] = shifted[:, 0:wb - LANES]
    neww_ref[0, :, wb - LANES:wb] = jnp.where(lane >= LANES - t, pltpu.roll(nw_t, LANES - t, 1), shifted[:, wb - LANES:wb])
    keys_t = jnp.concatenate([cw, nw_t], axis=1).astype(BF16)
    out = []
    for kvh in range(NSA_KV_HEADS):
        lo, hi = kvh * HEAD_DIM, (kvh + 1) * HEAD_DIM
        sw = _dot(_stack_heads(qr, kvh), keys_t[lo:hi, :])
        key_i = lax.broadcasted_iota(jnp.int32, sw.shape, 1)
        q_t = lax.broadcasted_iota(jnp.int32, sw.shape, 0) & (t - 1)
        rel = wb + q_t - key_i
        inside = jnp.logical_and(rel >= 0, rel < WINDOW)
        pw = _softmax_rows(jnp.where(inside, sw, -jnp.inf))
        out.append(_unstack_heads(_dot_nt(pw.astype(BF16), keys_t[KV_WIDTH + lo:KV_WIDTH + hi, :]), t))
    ow_ref[0] = jnp.concatenate(out, axis=1)


def _nsa_sample_win(qr, cache_win_t, new_win):
    bs, t, _ = qr.shape
    wb = cache_win_t.shape[2]
    seq = lambda i: (i, 0, 0)
    return pl.pallas_call(
        functools.partial(_nsa_sample_win_body, n_new=t),
        grid=(bs,),
        in_specs=[pl.BlockSpec((1, t, NSA_WIDTH), seq),
                  pl.BlockSpec((1, 2 * KV_WIDTH, wb), seq),
                  pl.BlockSpec((1, t, 2 * KV_WIDTH), seq)],
        out_specs=[pl.BlockSpec((1, t, NSA_WIDTH), seq), pl.BlockSpec((1, 2 * KV_WIDTH, wb), seq)],
        out_shape=[jax.ShapeDtypeStruct((bs, t, NSA_WIDTH), F32),
                   jax.ShapeDtypeStruct((bs, 2 * KV_WIDTH, wb), F32)],
        compiler_params=pltpu.CompilerParams(dimension_semantics=("parallel",), vmem_limit_bytes=VMEM_LIMIT),
        name="nsa_sample_win",
    )(qr, cache_win_t, new_win)


def _gla_constants(n_seq):
    tile = GLA_TILE
    seq_len = tile // n_seq
    t = np.arange(tile)[:, None]
    r = np.arange(tile)[None, :]
    same = (t // seq_len) == (r // seq_len)
    ops = [same & (r <= t), same & (r > t)]
    masks = []
    m = 1
    while 2 * m <= seq_len:
        upper = (t % (2 * m)) >= m
        mid = (t // (2 * m)) * (2 * m) + m - 1
        ops.append(upper & (r > mid) & (r <= t))
        ops.append((~upper) & (r > t) & (r <= mid))
        same_blk = (t // (2 * m)) == (r // (2 * m))
        masks.append(same_blk & upper & ((r % (2 * m)) < m))
        m *= 2
    ops = np.concatenate([o.astype(np.float32) for o in ops], axis=0)
    masks = np.stack([k.astype(np.float32) for k in masks], axis=0)
    lane_seq = (np.arange(n_seq * GLA_DK)[None, :] // GLA_DK) == (np.arange(tile)[:, None] // seq_len)
    rep = (np.arange(n_seq * GLA_DK)[None, :] % GLA_DK) == np.arange(GLA_DK)[:, None]
    return (jnp.asarray(ops, BF16), jnp.asarray(masks, F32), jnp.asarray(lane_seq.astype(np.float32), F32),
            jnp.asarray(rep.astype(np.float32), BF16))


def _gla_body(gq_ref, gk_ref, gv_ref, gg_ref, misc_ref, s0_ref, wa_hi_ref, wa_lo_ref, ba_ref, ops_ref, masks_ref,
              lane_seq_ref, rep_ref, gnorm_ref, o_ref, sfin_ref, st_sc, *, n_levels):
    step = pl.program_id(1)

    @pl.when(step == 0)
    def _():
        st_sc[...] = s0_ref[...]

    for g in range(gq_ref.shape[0]):
        _gla_tile(g, gq_ref, gk_ref, gv_ref, gg_ref, misc_ref, wa_hi_ref, wa_lo_ref, ba_ref, ops_ref, masks_ref,
                  lane_seq_ref, rep_ref, gnorm_ref, o_ref, st_sc, n_levels)

    @pl.when(step == pl.num_programs(1) - 1)
    def _():
        sfin_ref[...] = st_sc[...]


def _gla_tile(g, gq_ref, gk_ref, gv_ref, gg_ref, misc_ref, wa_hi_ref, wa_lo_ref, ba_ref, ops_ref, masks_ref,
              lane_seq_ref, rep_ref, gnorm_ref, o_ref, st_sc, n_levels):
    tile = GLA_TILE
    m_hi, m_lo = _split(misc_ref[g])
    x = _dot(m_hi, wa_hi_ref[...]) + _dot(m_hi, wa_lo_ref[...]) + _dot(m_lo, wa_hi_ref[...]) + ba_ref[...]
    la = (jnp.minimum(x, 0.0) - jnp.log(1.0 + jnp.exp(-jnp.abs(x)))) * (1.0 / GLA_GATE_TAU)
    la_hi, la_lo = _split(la)
    cums = _dot(ops_ref[...], la_hi) + _dot(ops_ref[...], la_lo)
    lane_seq = lane_seq_ref[...]
    rep = rep_ref[...]
    multi = lane_seq.shape[1] > GLA_DK

    def spread(a):
        a = a.astype(BF16)
        return (_dot(a, rep) * lane_seq).astype(BF16) if multi else a

    for h in range(GLA_HEADS):
        dk = slice(h * GLA_DK, (h + 1) * GLA_DK)
        dv = slice(h * GLA_DV, (h + 1) * GLA_DV)
        q = gq_ref[g, :, dk] * (GLA_DK ** -0.5)
        k = gk_ref[g, :, dk]
        v = gv_ref[g, :, dv]
        v_b = v.astype(BF16)
        st = st_sc[g, h]
        since_start = cums[0:tile, dk]
        until_end = cums[tile:2 * tile, dk]
        o = _dot_nt(spread(q * jnp.exp(since_start)), st.astype(BF16))
        a = jnp.zeros((tile, tile), F32)
        for lv in range(n_levels):
            dq = cums[(2 + 2 * lv) * tile:(3 + 2 * lv) * tile, dk]
            dkk = cums[(3 + 2 * lv) * tile:(4 + 2 * lv) * tile, dk]
            a = a + _dot_nt((q * jnp.exp(dq)).astype(BF16), (k * jnp.exp(dkk)).astype(BF16)) * masks_ref[lv]
        o = o + _dot(a.astype(BF16), v_b) + jnp.sum(q * k, axis=-1, keepdims=True) * v
        la_h = la[:, dk]
        if multi:
            total = jnp.sum(_dot2(la_h, rep) * lane_seq, axis=0, keepdims=True)
        else:
            total = jnp.sum(la_h, axis=0, keepdims=True)
        st_new = st * jnp.exp(total) + _dot(v.T.astype(BF16), spread(k * jnp.exp(until_end)))
        st_sc[g, h] = st_new
        ms = jnp.mean(o * o, axis=-1, keepdims=True)
        o_ref[g, :, dv] = o * lax.rsqrt(ms + EPS) * gnorm_ref[...] * _silu(gg_ref[g, :, dv])


def _gla(gq, gk, gv, gg, misc, s0_t, wa_hi, wa_lo, ba, gnorm, n_seq):
    groups, rows, _ = gq.shape
    steps = rows // GLA_TILE
    par = GLA_GROUPS_PER_STEP
    assert groups % par == 0
    ops, masks, lane_seq, rep = _gla_constants(n_seq)
    n_levels = masks.shape[0]
    tile = lambda g, s: (g, s, 0)
    c2 = lambda g, s: (0, 0)
    c3 = lambda g, s: (0, 0, 0)
    st_block = (par, GLA_HEADS, GLA_DV, n_seq * GLA_DK)
    st_map = lambda g, s: (g, 0, 0, 0)
    return pl.pallas_call(
        functools.partial(_gla_body, n_levels=n_levels),
        grid=(groups // par, steps),
        in_specs=[
            pl.BlockSpec((par, GLA_TILE, GLA_QK_WIDTH), tile), pl.BlockSpec((par, GLA_TILE, GLA_QK_WIDTH), tile),
            pl.BlockSpec((par, GLA_TILE, GLA_WIDTH), tile), pl.BlockSpec((par, GLA_TILE, GLA_WIDTH), tile),
            pl.BlockSpec((par, GLA_TILE, MISC_WIDTH), tile),
            pl.BlockSpec(st_block, st_map),
            pl.BlockSpec(wa_hi.shape, c2), pl.BlockSpec(wa_lo.shape, c2), pl.BlockSpec(ba.shape, c2),
            pl.BlockSpec(ops.shape, c2), pl.BlockSpec(masks.shape, c3),
            pl.BlockSpec(lane_seq.shape, c2), pl.BlockSpec(rep.shape, c2), pl.BlockSpec(gnorm.shape, c2),
        ],
        out_specs=[pl.BlockSpec((par, GLA_TILE, GLA_WIDTH), tile), pl.BlockSpec(st_block, st_map)],
        out_shape=[jax.ShapeDtypeStruct((groups, rows, GLA_WIDTH), F32),
                   jax.ShapeDtypeStruct((groups,) + st_block[1:], F32)],
        scratch_shapes=[pltpu.VMEM(st_block, F32)],
        compiler_params=pltpu.CompilerParams(dimension_semantics=("parallel", "arbitrary"),
                                             vmem_limit_bytes=VMEM_LIMIT),
        name="gla",
    )(gq, gk, gv, gg, misc, s0_t, wa_hi, wa_lo, ba, ops, masks, lane_seq, rep, gnorm)


def _out_body(x_ref, oc_ref, os_ref, ow_ref, og_ref, misc_ref, gex_ref, wout_ref, gffn_ref, wa_ref, wb_ref, wdown_ref,
              y_ref, hn_sc):
    c = pl.program_id(1)

    @pl.when(c == 0)
    def _():
        gates = 1.0 / (1.0 + jnp.exp(-misc_ref[...]))
        branch = (oc_ref, os_ref, ow_ref)
        o_nsa = None
        for i in range(3):
            g_i = _dot2(gates, gex_ref[i])
            term = g_i * branch[i][...]
            o_nsa = term if o_nsa is None else o_nsa + term
        x1 = (x_ref[...] + _dot(o_nsa.astype(BF16), wout_ref[0:NSA_WIDTH, :])
              + _dot(og_ref[...].astype(BF16), wout_ref[NSA_WIDTH:NSA_WIDTH + GLA_WIDTH, :]))
        ms = jnp.mean(x1 * x1, axis=-1, keepdims=True)
        hn_sc[...] = (x1 * lax.rsqrt(ms + EPS) * gffn_ref[...]).astype(BF16)
        y_ref[...] = x1

    hn = hn_sc[...]
    act = _silu(_dot(hn, wa_ref[...])) * _dot(hn, wb_ref[...])
    y_ref[...] += _dot(act.astype(BF16), wdown_ref[...])


def _out(x2d, oc, osel, ow, og, misc, gex, wout, gffn, wup, wdown):
    n = x2d.shape[0]
    tm = min(OUT_ROW_TILE, n)
    assert n % tm == 0
    n_chunks = FFN_HIDDEN // FFN_CHUNK
    row = lambda i, c: (i, 0)
    c2 = lambda i, c: (0, 0)
    return pl.pallas_call(
        _out_body,
        grid=(n // tm, n_chunks),
        in_specs=[
            pl.BlockSpec((tm, D_MODEL), row),
            pl.BlockSpec((tm, NSA_WIDTH), row), pl.BlockSpec((tm, NSA_WIDTH), row), pl.BlockSpec((tm, NSA_WIDTH), row),
            pl.BlockSpec((tm, GLA_WIDTH), row), pl.BlockSpec((tm, MISC_WIDTH), row),
            pl.BlockSpec(gex.shape, lambda i, c: (0, 0, 0)),
            pl.BlockSpec(wout.shape, c2),
            pl.BlockSpec(gffn.shape, c2),
            pl.BlockSpec((D_MODEL, FFN_CHUNK), lambda i, c: (0, c)),
            pl.BlockSpec((D_MODEL, FFN_CHUNK), lambda i, c: (0, n_chunks + c)),
            pl.BlockSpec((FFN_CHUNK, D_MODEL), lambda i, c: (c, 0)),
        ],
        out_specs=pl.BlockSpec((tm, D_MODEL), row),
        out_shape=jax.ShapeDtypeStruct((n, D_MODEL), F32),
        scratch_shapes=[pltpu.VMEM((tm, D_MODEL), BF16)],
        compiler_params=pltpu.CompilerParams(dimension_semantics=("parallel", "arbitrary"),
                                             vmem_limit_bytes=VMEM_LIMIT),
        name="out",
    )(x2d, oc, osel, ow, og, misc, gex, wout, gffn, wup, wup, wdown)


def _rope_tables(pos):
    half = HEAD_DIM // 2
    inv = ROPE_THETA ** (-jnp.arange(half, dtype=F32) / half)
    ang = pos.astype(F32)[:, None] * inv[None, :]
    cos, sin = jnp.cos(ang), jnp.sin(ang)
    reps = LANES // HEAD_DIM
    return jnp.tile(jnp.concatenate([cos, cos], axis=1), (1, reps)), jnp.tile(jnp.concatenate([-sin, sin], axis=1), (1, reps))


def _overlap_matrix(n_chunks, n_cols):
    start = np.arange(n_chunks)[:, None] * CMP_STRIDE
    j0 = np.arange(n_cols)[None, :] * SEL_BLOCK
    return jnp.asarray(((start < j0 + SEL_BLOCK) & (start + CMP_BLOCK > j0)).astype(np.float32), BF16)


def _expand_matrix(n_tiles, keys):
    key = np.arange(n_tiles)[:, None, None] * keys + np.arange(keys)[None, None, :]
    blk = np.arange(LANES)[None, :, None]
    return jnp.asarray((key // SEL_BLOCK == blk).astype(np.float32), BF16)


def _block_diag_ones(width):
    i = np.arange(width)
    return jnp.asarray((i[:, None] // HEAD_DIM == i[None, :] // HEAD_DIM).astype(np.float32), BF16)


def _gate_expand():
    lane = np.arange(MISC_WIDTH)[None, :, None]
    out = np.arange(NSA_WIDTH)[None, None, :]
    c = np.arange(3)[:, None, None]
    return jnp.asarray(((lane == (out // HEAD_DIM) * 3 + c) & (lane < GATE_COLS)).astype(np.float32), BF16)


def _pad_cols(w, width):
    return jnp.pad(w, ((0, 0), (0, width - w.shape[1])))


def kernel(x_prompt, x_sample, cache_nsa_kv, cache_win_kv, state_gla, page_table, w_norm_mix, w_in, nsa_q_gain,
           nsa_k_gain, cmp_pos_k, cmp_w1_k, cmp_w2_k, cmp_pos_v, cmp_w1_v, cmp_w2_v, gla_w_a2, gla_b_a,
           gla_norm_gain, w_out, w_norm_ffn, w_ffn_up, w_ffn_down):
    b_p, t_p, _ = x_prompt.shape
    b_s, t_s, _ = x_sample.shape
    depth = w_in.shape[0]
    assert depth == 1, "single layer"
    n_pages = page_table.shape[1]
    past_len = n_pages * PAGE_SIZE
    n_pool = cache_nsa_kv.shape[1]
    wb = cache_win_kv.shape[2]
    assert t_p % SEL_KEY_TILE == 0 and t_p >= WINDOW + Q_TILE and (b_p * t_p) % ROW_TILE == 0
    assert (b_s * t_s) % ROW_TILE == 0 and GLA_TILE % t_s == 0 and b_s % (GLA_TILE // t_s) == 0
    assert n_pages % PAGES_PER_STEP == 0 and wb == WINDOW and ROW_TILE % t_s == 0
    l = 0

    q_w, kv_w, gate_w, gq_w, gk_w, gv_w, alow_w, gg_w = jnp.split(
        w_in[l], np.cumsum([NSA_WIDTH, 6 * KV_WIDTH, GATE_COLS, GLA_QK_WIDTH, GLA_QK_WIDTH, GLA_WIDTH,
                            GLA_GATE_RANK])[:].tolist(), axis=1)
    misc_w = _pad_cols(jnp.concatenate([gate_w, alow_w], axis=1), MISC_WIDTH)
    w_perm = jnp.concatenate([q_w, kv_w, gq_w, gk_w, gv_w, gg_w, misc_w], axis=1).astype(BF16)
    gmix = w_norm_mix[l][None, :]
    qg = jnp.tile(nsa_q_gain[l], NSA_HEADS)[None, :]
    kg = [jnp.tile(nsa_k_gain[l, i], NSA_KV_HEADS)[None, :] for i in range(3)]
    ones512 = _block_diag_ones(NSA_WIDTH)
    ones128 = _block_diag_ones(LANES)
    wa = jnp.zeros((MISC_WIDTH, GLA_QK_WIDTH), F32).at[GATE_COLS:GATE_COLS + GLA_GATE_RANK].set(gla_w_a2[l])
    wa_hi, wa_lo = _split(wa)
    ba = gla_b_a[l][None, :]
    gnorm = gla_norm_gain[l][None, :]
    wk1 = _cmp1_weights(cmp_w1_k[l])
    wv1 = _cmp1_weights(cmp_w1_v[l])
    eye2 = jnp.eye(NSA_KV_HEADS, dtype=F32)
    w2k_bd = jnp.einsum("ed,hg->hegd", cmp_w2_k[l], eye2).reshape(2 * CMP_HIDDEN, KV_WIDTH).astype(BF16)
    w2v_bd = jnp.einsum("ed,hg->hegd", cmp_w2_v[l], eye2).reshape(2 * CMP_HIDDEN, KV_WIDTH).astype(BF16)
    flat = CMP_BLOCK * HEAD_DIM
    posk, posv = cmp_pos_k[l].reshape(1, flat), cmp_pos_v[l].reshape(1, flat)
    w1k_flat, w1v_flat = cmp_w1_k[l].reshape(flat, CMP_HIDDEN), cmp_w1_v[l].reshape(flat, CMP_HIDDEN)
    gex = _gate_expand()
    wout = w_out[l].astype(BF16)
    gffn = w_norm_ffn[l][None, :]
    wup = w_ffn_up[l].astype(BF16)
    wdown = w_ffn_down[l].astype(BF16)

    cos_p, sin_p = _rope_tables(jnp.arange(t_p))
    xp2 = x_prompt.reshape(b_p * t_p, D_MODEL)
    (qn_t, qr_t, rows_t, win_t, kvb, vt, gq, gk, gv, gg, misc) = _proj(
        xp2, gmix, w_perm, cos_p, sin_p, qg, kg[1], kg[2], ones512, t_p // ROW_TILE, seq_len=t_p)
    g_p = _cmp1_prompt(rows_t.reshape(b_p, 4, KV_WIDTH, t_p), wk1, wv1)
    kc_p, vc_p = _cmp2(g_p, posk, w1k_flat, posv, w1v_flat, w2k_bd, w2v_bd, kg[0], ones128)
    ovt_p = _overlap_matrix(t_p // CMP_STRIDE, LANES).T
    ext_p = jnp.swapaxes(_expand_matrix(t_p // SEL_KEY_TILE, SEL_KEY_TILE), 1, 2)
    oc, osel, ow = _nsa_prompt(qn_t, qr_t, kc_p, vc_p, kvb.reshape(b_p, t_p, -1),
                               vt.reshape(b_p, t_p // LANES, 2 * KV_WIDTH, LANES), ovt_p, ext_p)
    s0_p = jnp.zeros((b_p, GLA_HEADS, GLA_DV, GLA_DK), F32)
    r3 = lambda a, g=b_p: a.reshape(g, -1, a.shape[-1])
    og, sfin_p = _gla(r3(gq), r3(gk), r3(gv), r3(gg), r3(misc), s0_p, wa_hi, wa_lo, ba, gnorm, 1)
    y_p = _out(xp2, oc.reshape(b_p * t_p, -1), osel.reshape(b_p * t_p, -1), ow.reshape(b_p * t_p, -1),
               og.reshape(b_p * t_p, -1), misc, gex, wout, gffn, wup, wdown)
    out_rows_p = rows_t.reshape(b_p, 4, NSA_KV_HEADS, HEAD_DIM, t_p).transpose(0, 4, 1, 2, 3)[None]
    wlen = min(WINDOW, t_p)
    out_win_p = win_t[:, :, t_p - wlen:].reshape(b_p, 2, NSA_KV_HEADS, HEAD_DIM, wlen).transpose(0, 4, 1, 2, 3)[None]
    out_gla_p = jnp.swapaxes(sfin_p, 2, 3)[None].astype(state_gla.dtype)

    pos_s = past_len + jnp.arange(t_s)
    cos_s, sin_s = _rope_tables(pos_s)
    cos_s = jnp.tile(cos_s, (ROW_TILE // t_s, 1))
    sin_s = jnp.tile(sin_s, (ROW_TILE // t_s, 1))
    xs2 = x_sample.reshape(b_s * t_s, D_MODEL)
    (qn_s, qr_s, rows_s, win_s, kvb_s, _, gq_s, gk_s, gv_s, gg_s, misc_s) = _proj(
        xs2, gmix, w_perm, cos_s, sin_s, qg, kg[1], kg[2], ones512, 1)
    cache_t = cache_nsa_kv[l].transpose(0, 2, 3, 4, 1).reshape(n_pool, 4, KV_WIDTH, PAGE_SIZE)
    g_s = _cmp1_sample(cache_t, page_table, wk1, wv1)
    kc_s, vc_s = _cmp2(g_s, posk, w1k_flat, posv, w1v_flat, w2k_bd, w2v_bd, kg[0], ones128)
    n_sel_s = -(-(past_len + t_s) // SEL_BLOCK)
    sel_cols = -(-n_sel_s // LANES) * LANES
    ov_s = _overlap_matrix(past_len // CMP_STRIDE, sel_cols)
    seq3 = lambda a: a.reshape(b_s, t_s, a.shape[-1])
    oc_s, sel_s = _nsa_sample_cmp(seq3(qn_s), kc_s, vc_s, ov_s, past_len)
    steps = n_pages // PAGES_PER_STEP
    blocks_per_step = PAGES_PER_STEP * PAGE_SIZE // SEL_BLOCK
    selm = sel_s[:, :, :steps * blocks_per_step].reshape(b_s, NSA_KV_HEADS, 1, t_s, steps, blocks_per_step)
    selm = jnp.broadcast_to(selm, (b_s, NSA_KV_HEADS, NSA_GROUP, t_s, steps, blocks_per_step))
    selm = selm.transpose(0, 4, 1, 2, 3, 5).reshape(b_s, steps, NSA_KV_HEADS * NSA_GROUP * t_s, blocks_per_step)
    selm = jnp.pad(selm, ((0, 0), (0, 0), (0, 0), (0, LANES - blocks_per_step)))
    ex_s = _expand_matrix(1, PAGES_PER_STEP * PAGE_SIZE)
    os_s = _nsa_sample_sel(cache_t, page_table, seq3(qr_s), selm, seq3(kvb_s), ex_s)
    cache_win_t = cache_win_kv[l].transpose(0, 2, 3, 4, 1).reshape(b_s, 2 * KV_WIDTH, wb)
    ow_s, new_win_t = _nsa_sample_win(seq3(qr_s), cache_win_t, seq3(win_s))
    n_seq = GLA_TILE // t_s
    groups = b_s // n_seq
    s0_s = state_gla[l].astype(F32).reshape(groups, n_seq, GLA_HEADS, GLA_DK, GLA_DV)
    s0_s = s0_s.transpose(0, 2, 4, 1, 3).reshape(groups, GLA_HEADS, GLA_DV, n_seq * GLA_DK)
    rg = lambda a: a.reshape(groups, GLA_TILE, a.shape[-1])
    og_s, sfin_s = _gla(rg(gq_s), rg(gk_s), rg(gv_s), rg(gg_s), rg(misc_s), s0_s, wa_hi, wa_lo, ba, gnorm, n_seq)
    y_s = _out(xs2, oc_s.reshape(b_s * t_s, -1), os_s.reshape(b_s * t_s, -1), ow_s.reshape(b_s * t_s, -1),
               og_s.reshape(b_s * t_s, -1), misc_s, gex, wout, gffn, wup, wdown)
    out_rows_s = rows_s.reshape(1, b_s, t_s, 4, NSA_KV_HEADS, HEAD_DIM)
    out_win_s = new_win_t.reshape(b_s, 2, NSA_KV_HEADS, HEAD_DIM, wb).transpose(0, 4, 1, 2, 3)[None]
    out_gla_s = sfin_s.reshape(groups, GLA_HEADS, GLA_DV, n_seq, GLA_DK).transpose(0, 3, 1, 4, 2)
    out_gla_s = out_gla_s.reshape(1, b_s, GLA_HEADS, GLA_DK, GLA_DV).astype(state_gla.dtype)

    return (y_p.reshape(b_p, t_p, D_MODEL), y_s.reshape(b_s, t_s, D_MODEL), out_rows_p, out_win_p, out_gla_p,
            out_rows_s, out_win_s, out_gla_s)
```

```python
import functools
import math

import numpy as np
import jax
import jax.numpy as jnp
from jax import lax
from jax.experimental import pallas as pl
from jax.experimental.pallas import tpu as pltpu

F32 = jnp.float32
BF16 = jnp.bfloat16

D_MODEL = 1024
PAGE_SIZE = 128
NSA_HEADS = 8
NSA_KV_HEADS = 2
NSA_GROUP = NSA_HEADS // NSA_KV_HEADS
HEAD_DIM = 64
NSA_WIDTH = NSA_HEADS * HEAD_DIM
KV_WIDTH = NSA_KV_HEADS * HEAD_DIM
CMP_BLOCK = 32
CMP_STRIDE = 16
CMP_HIDDEN = 2 * HEAD_DIM
SEL_BLOCK = 64
SEL_TOPK = 16
WINDOW = 512
FORCE_SCORE = 1.0e4
GLA_HEADS = 4
GLA_DK = 64
GLA_DV = 128
GLA_WIDTH = GLA_HEADS * GLA_DV
GLA_QK_WIDTH = GLA_HEADS * GLA_DK
GLA_GATE_RANK = 16
GLA_GATE_TAU = 16.0
FFN_HIDDEN = -(-8 * D_MODEL // (3 * 256)) * 256
ROPE_THETA = 10000.0
EPS = 1e-6
NEG = -1.0e30

LANES = 128
MISC_WIDTH = LANES
GATE_COLS = 3 * NSA_HEADS
PROJ_WIDTH = NSA_WIDTH + 6 * KV_WIDTH + 2 * GLA_QK_WIDTH + 2 * GLA_WIDTH + MISC_WIDTH
ROW_TILE = 512
OUT_ROW_TILE = 1024
Q_TILE = 128
SEL_KEY_TILE = 512
GLA_TILE = 128
GLA_GROUPS_PER_STEP = 2
FFN_CHUNK = 256
VMEM_LIMIT = 56 * 1024 * 1024


def _dot(a, b):
    return jnp.dot(a, b, preferred_element_type=F32)


def _dot_nt(a, b):
    return lax.dot_general(a, b, (((1,), (1,)), ((), ())), preferred_element_type=F32)


def _split(x):
    hi = x.astype(BF16)
    lo = (x - hi.astype(F32)).astype(BF16)
    return hi, lo


def _dot2(x, w_bf16):
    hi, lo = _split(x)
    return _dot(hi, w_bf16) + _dot(lo, w_bf16)


def _silu(x):
    return x * (1.0 / (1.0 + jnp.exp(-x)))


def _head_norm(v, ones_bd, gain):
    ss = _dot((v * v).astype(BF16), ones_bd)
    return v * lax.rsqrt(ss * (1.0 / HEAD_DIM) + EPS) * gain


def _rope(v, cos, sin_signed):
    width = v.shape[-1]
    reps = width // LANES
    if reps > 1:
        cos = jnp.concatenate([cos] * reps, axis=1)
        sin_signed = jnp.concatenate([sin_signed] * reps, axis=1)
    lane = lax.broadcasted_iota(jnp.int32, v.shape, 1)
    first_half = (lane & (HEAD_DIM - 1)) < (HEAD_DIM // 2)
    rot = jnp.where(first_half, pltpu.roll(v, width - HEAD_DIM // 2, 1), pltpu.roll(v, HEAD_DIM // 2, 1))
    return v * cos + rot * sin_signed


def _softmax_rows(s):
    m = jnp.max(s, axis=-1, keepdims=True)
    m = jnp.where(m == -jnp.inf, 0.0, m)
    e = jnp.exp(s - m)
    return e / jnp.maximum(jnp.sum(e, axis=-1, keepdims=True), 1e-30)


def _proj_body(x_ref, gmix_ref, w_ref, cos_ref, sin_ref, qg_ref, kg1_ref, kg2_ref, ones_ref,
               qn_ref, qr_ref, rows_ref, win_ref, kvb_ref, vt_ref, gq_ref, gk_ref, gv_ref, gg_ref, misc_ref, *,
               rows_minor):
    x = x_ref[...]
    ms = jnp.mean(x * x, axis=-1, keepdims=True)
    h = (x * lax.rsqrt(ms + EPS) * gmix_ref[...]).astype(BF16)
    cos = cos_ref[...]
    sin = sin_ref[...]
    ones512 = ones_ref[...]
    ones128 = ones512[:LANES, :LANES]
    scale = HEAD_DIM ** -0.5

    off = 0
    q = _dot(h, w_ref[:, off:off + NSA_WIDTH])
    off += NSA_WIDTH
    qn = _head_norm(q, ones512, qg_ref[...])
    qr = _rope(qn, cos, sin)
    if rows_minor:
        qn_ref[0] = (qn * scale).T.astype(BF16)
        qr_ref[0] = (qr * scale).T.astype(BF16)
    else:
        qn_ref[...] = (qn * scale).astype(BF16)
        qr_ref[...] = (qr * scale).astype(BF16)

    kv = _dot(h, w_ref[:, off:off + 6 * KV_WIDTH])
    off += 6 * KV_WIDTH
    k_slc = _rope(_head_norm(kv[:, 2 * KV_WIDTH:3 * KV_WIDTH], ones128, kg1_ref[...]), cos, sin)
    k_win = _rope(_head_norm(kv[:, 4 * KV_WIDTH:5 * KV_WIDTH], ones128, kg2_ref[...]), cos, sin)
    v_slc = kv[:, 3 * KV_WIDTH:4 * KV_WIDTH]
    v_win = kv[:, 5 * KV_WIDTH:6 * KV_WIDTH]
    v_slc_t = v_slc.T
    v_win_t = v_win.T
    for j in range(v_slc.shape[0] // LANES):
        vt_ref[j, 0:KV_WIDTH, :] = v_slc_t[:, j * LANES:(j + 1) * LANES].astype(BF16)
        vt_ref[j, KV_WIDTH:2 * KV_WIDTH, :] = v_win_t[:, j * LANES:(j + 1) * LANES].astype(BF16)
    if rows_minor:
        rows_ref[0, 0:2 * KV_WIDTH, :] = kv[:, 0:2 * KV_WIDTH].T
        rows_ref[0, 2 * KV_WIDTH:3 * KV_WIDTH, :] = k_slc.T
        rows_ref[0, 3 * KV_WIDTH:4 * KV_WIDTH, :] = v_slc_t
        win_ref[0, 0:KV_WIDTH, :] = k_win.T
        win_ref[0, KV_WIDTH:2 * KV_WIDTH, :] = v_win_t
    else:
        rows_ref[:, 0:2 * KV_WIDTH] = kv[:, 0:2 * KV_WIDTH]
        rows_ref[:, 2 * KV_WIDTH:3 * KV_WIDTH] = k_slc
        rows_ref[:, 3 * KV_WIDTH:4 * KV_WIDTH] = v_slc
        win_ref[:, 0:KV_WIDTH] = k_win
        win_ref[:, KV_WIDTH:2 * KV_WIDTH] = v_win
    kvb_ref[:, 0:KV_WIDTH] = k_slc.astype(BF16)
    kvb_ref[:, KV_WIDTH:2 * KV_WIDTH] = v_slc.astype(BF16)
    kvb_ref[:, 2 * KV_WIDTH:3 * KV_WIDTH] = k_win.astype(BF16)
    kvb_ref[:, 3 * KV_WIDTH:4 * KV_WIDTH] = v_win.astype(BF16)

    gq_ref[...] = _dot(h, w_ref[:, off:off + GLA_QK_WIDTH])
    off += GLA_QK_WIDTH
    gk_ref[...] = _dot(h, w_ref[:, off:off + GLA_QK_WIDTH])
    off += GLA_QK_WIDTH
    gv_ref[...] = _dot(h, w_ref[:, off:off + GLA_WIDTH])
    off += GLA_WIDTH
    gg_ref[...] = _dot(h, w_ref[:, off:off + GLA_WIDTH])
    off += GLA_WIDTH
    misc_ref[...] = _dot(h, w_ref[:, off:off + MISC_WIDTH])


def _proj(x2d, gmix, w_perm, cos_t, sin_t, qg, kg1, kg2, ones512, table_blocks, seq_len=None):
    n = x2d.shape[0]
    tm = ROW_TILE
    row = lambda i: (i, 0)
    const = lambda i: (0, 0)
    tab = lambda i: (i % table_blocks, 0)
    widths = [(NSA_WIDTH, BF16), (NSA_WIDTH, BF16), (4 * KV_WIDTH, F32), (2 * KV_WIDTH, F32), (4 * KV_WIDTH, BF16), None,
              (GLA_QK_WIDTH, F32), (GLA_QK_WIDTH, F32), (GLA_WIDTH, F32), (GLA_WIDTH, F32), (MISC_WIDTH, F32)]
    out_specs = [None if w is None else pl.BlockSpec((tm, w[0]), row) for w in widths]
    out_shape = [None if w is None else jax.ShapeDtypeStruct((n, w[0]), w[1]) for w in widths]
    out_specs[5] = pl.BlockSpec((tm // LANES, 2 * KV_WIDTH, LANES), lambda i: (i, 0, 0))
    out_shape[5] = jax.ShapeDtypeStruct((n // LANES, 2 * KV_WIDTH, LANES), BF16)
    if seq_len is not None:
        tiles = seq_len // tm
        for i in (0, 1, 2, 3):
            w, dt = widths[i]
            out_specs[i] = pl.BlockSpec((1, w, tm), lambda i: (i // tiles, 0, i % tiles))
            out_shape[i] = jax.ShapeDtypeStruct((n // seq_len, w, seq_len), dt)
    return pl.pallas_call(
        functools.partial(_proj_body, rows_minor=seq_len is not None),
        grid=(n // tm,),
        in_specs=[
            pl.BlockSpec((tm, D_MODEL), row),
            pl.BlockSpec((1, D_MODEL), const),
            pl.BlockSpec((D_MODEL, PROJ_WIDTH), const),
            pl.BlockSpec((tm, LANES), tab),
            pl.BlockSpec((tm, LANES), tab),
            pl.BlockSpec((1, NSA_WIDTH), const),
            pl.BlockSpec((1, KV_WIDTH), const),
            pl.BlockSpec((1, KV_WIDTH), const),
            pl.BlockSpec((NSA_WIDTH, NSA_WIDTH), const),
        ],
        out_specs=out_specs,
        out_shape=out_shape,
        compiler_params=pltpu.CompilerParams(dimension_semantics=("parallel",), vmem_limit_bytes=VMEM_LIMIT),
        name="proj",
    )(x2d, gmix, w_perm, cos_t, sin_t, qg, kg1, kg2, ones512)


def _cmp_body(*refs, n_src, n_prefetch):
    refs = refs[n_prefetch:]
    n_refs = len(refs) - 14
    src = refs[:n_refs]
    (wk_ref, wv_ref, posk_ref, w1k_ref, posv_ref, w1v_ref, w2k_ref, w2v_ref, kg0_ref, ones_ref,
     kc_ref, vc_ref, xk_sc, xv_sc) = refs[n_refs:]
    for i in range(n_src):
        rows = slice(i * PAGE_SIZE, (i + 1) * PAGE_SIZE)
        if n_refs == 1:
            k_t, v_t = src[0][0, :, rows], src[0][1, :, rows]
        else:
            k_t, v_t = src[i][0], src[i][1]
        xk_sc[rows, :] = k_t.T
        xv_sc[rows, :] = v_t.T
    n_chunks = n_src * PAGE_SIZE // CMP_STRIDE

    def rows_at(ref, s):
        return ref[pl.ds(s, n_chunks, stride=CMP_STRIDE), :]

    acc_k = None
    acc_v = None
    for sp in range(CMP_STRIDE // 2):
        lk = jnp.concatenate([rows_at(xk_sc, 2 * sp), rows_at(xk_sc, 2 * sp + 1)], axis=1).astype(BF16)
        lv = jnp.concatenate([rows_at(xv_sc, 2 * sp), rows_at(xv_sc, 2 * sp + 1)], axis=1).astype(BF16)
        dk = _dot(lk, wk_ref[sp])
        dv = _dot(lv, wv_ref[sp])
        acc_k = dk if acc_k is None else acc_k + dk
        acc_v = dv if acc_v is None else acc_v + dv
    hd = CMP_HIDDEN

    def pos_term(pos_ref, w1_ref):
        p = jnp.broadcast_to(pos_ref[...], (8, pos_ref.shape[1]))
        p_hi, p_lo = _split(p)
        w_hi, w_lo = _split(w1_ref[...])
        return (_dot(p_hi, w_hi) + _dot(p_hi, w_lo) + _dot(p_lo, w_hi))[0:1, :]

    def hidden(acc, pos):
        out = []
        for h in range(NSA_KV_HEADS):
            j0 = acc[:, 2 * h * hd:(2 * h + 1) * hd]
            j1 = acc[:, (2 * h + 1) * hd:(2 * h + 2) * hd]
            out.append(j0 + pltpu.roll(j1, n_chunks - 1, 0) + pos)
        return _silu(jnp.concatenate(out, axis=1)).astype(BF16)

    kc = _dot(hidden(acc_k, pos_term(posk_ref, w1k_ref)), w2k_ref[...])
    kc_ref[0] = _head_norm(kc, ones_ref[...], kg0_ref[...])
    vc_ref[0] = _dot(hidden(acc_v, pos_term(posv_ref, w1v_ref)), w2v_ref[...])


def _cmp1_weights(w1):
    w = w1.reshape(2, CMP_STRIDE // 2, 2, HEAD_DIM, CMP_HIDDEN)
    eye = jnp.eye(NSA_KV_HEADS, dtype=w1.dtype)
    big = jnp.einsum("jpsde,hg->pshdgje", w, eye)
    return big.reshape(CMP_STRIDE // 2, 4 * HEAD_DIM, 4 * CMP_HIDDEN).astype(BF16)


def _cmp_call(src_args, src_specs, n_src, n_prefetch, grid_spec_fn, consts, n_seq, name):
    rows = n_src * PAGE_SIZE
    chunks = rows // CMP_STRIDE
    const_specs = [pl.BlockSpec(c.shape, (lambda *_, nd=c.ndim: (0,) * nd)) for c in consts]
    seq = lambda i, *_: (i, 0, 0)
    return pl.pallas_call(
        functools.partial(_cmp_body, n_src=n_src, n_prefetch=n_prefetch),
        grid_spec=grid_spec_fn(
            grid=(n_seq,),
            in_specs=src_specs + const_specs,
            out_specs=[pl.BlockSpec((1, chunks, KV_WIDTH), seq)] * 2,
            scratch_shapes=[pltpu.VMEM((rows, KV_WIDTH), F32), pltpu.VMEM((rows, KV_WIDTH), F32)],
        ),
        out_shape=[jax.ShapeDtypeStruct((n_seq, chunks, KV_WIDTH), F32)] * 2,
        compiler_params=pltpu.CompilerParams(dimension_semantics=("parallel",), vmem_limit_bytes=VMEM_LIMIT),
        name=name,
    )(*src_args, *consts)


def _cmp_prompt(rows_t, consts):
    b, _, _, t = rows_t.shape
    spec = pl.BlockSpec((None, 2, KV_WIDTH, t), lambda i: (i, 0, 0, 0))
    grid_spec_fn = functools.partial(pltpu.PrefetchScalarGridSpec, num_scalar_prefetch=0)
    return _cmp_call([rows_t], [spec], t // PAGE_SIZE, 0, grid_spec_fn, consts, b, "cmp_prompt")


def _page_specs(kind_block, n_pages):
    def spec(i):
        return pl.BlockSpec((None, 2, KV_WIDTH, PAGE_SIZE), lambda b, pt: (pt[b, i], kind_block, 0, 0))
    return [spec(i) for i in range(n_pages)]


def _cmp_sample(cache_t, page_table, consts):
    bs, n_pages = page_table.shape
    grid_spec_fn = functools.partial(pltpu.PrefetchScalarGridSpec, num_scalar_prefetch=1)
    return _cmp_call([page_table] + [cache_t] * n_pages, _page_specs(0, n_pages), n_pages, 1, grid_spec_fn, consts, bs,
                     "cmp_sample")


def _stack_heads(q, kvh):
    base = kvh * NSA_GROUP * HEAD_DIM
    return jnp.concatenate([q[:, base + g * HEAD_DIM: base + (g + 1) * HEAD_DIM] for g in range(NSA_GROUP)], axis=0)


def _unstack_heads(o, rows):
    return jnp.concatenate([o[g * rows:(g + 1) * rows] for g in range(NSA_GROUP)], axis=1)


def _group_sum(p, rows):
    out = p[0:rows]
    for g in range(1, NSA_GROUP):
        out = out + p[g * rows:(g + 1) * rows]
    return out


def _select_blocks(imp, qpos, n_blocks):
    blk = lax.broadcasted_iota(jnp.int32, imp.shape, 1)
    visible = blk * SEL_BLOCK <= qpos
    forced = jnp.logical_or(blk == 0, blk == (qpos >> int(math.log2(SEL_BLOCK))))
    score = jnp.where(forced, FORCE_SCORE, jnp.where(visible, imp, -jnp.inf))
    beaten = jnp.zeros(imp.shape, F32)
    for i in range(n_blocks):
        ci = score[:, i:i + 1]
        earlier = jnp.where(blk > i, 1.0, 0.0)
        beaten = beaten + jnp.where(ci > score, 1.0, jnp.where(ci == score, earlier, 0.0))
    return jnp.where(beaten < float(SEL_TOPK), jnp.where(score > -jnp.inf, 1.0, 0.0), 0.0)


def _softmax_cols(s):
    m = jnp.max(s, axis=0, keepdims=True)
    m = jnp.where(m == -jnp.inf, 0.0, m)
    e = jnp.exp(s - m)
    return e / jnp.maximum(jnp.sum(e, axis=0, keepdims=True), 1e-30)


def _select_blocks_t(imp_t, qpos, n_blocks):
    blk = lax.broadcasted_iota(jnp.int32, imp_t.shape, 0)
    visible = blk * SEL_BLOCK <= qpos
    forced = jnp.logical_or(blk == 0, blk == (qpos >> int(math.log2(SEL_BLOCK))))
    score = jnp.where(forced, FORCE_SCORE, jnp.where(visible, imp_t, -jnp.inf))
    beaten = jnp.zeros(imp_t.shape, F32)
    for i in range(n_blocks):
        ci = score[i:i + 1, :]
        earlier = jnp.where(blk > i, 1.0, 0.0)
        beaten = beaten + jnp.where(ci > score, 1.0, jnp.where(ci == score, earlier, 0.0))
    return jnp.where(beaten < float(SEL_TOPK), jnp.where(score > -jnp.inf, 1.0, 0.0), 0.0)


def _nsa_prompt_body(qn_ref, qr_ref, kc_ref, vc_ref, kvb_ref, vt_ref, ovt_ref, ext_ref, oc_ref, os_ref, ow_ref, *,
                     seq_len):
    tq = Q_TILE
    tk = SEL_KEY_TILE
    q0 = pl.program_id(1) * tq
    n_cmp = seq_len // CMP_STRIDE - 1
    n_sel = seq_len // SEL_BLOCK
    cols = NSA_GROUP * tq
    qn_t = qn_ref[0]
    qr_t = qr_ref[0]
    kc = kc_ref[0].astype(BF16)
    vc_t = vc_ref[0].T.astype(BF16)
    qpos1 = q0 + lax.broadcasted_iota(jnp.int32, (1, tq), 1)
    qpos4 = jnp.concatenate([qpos1] * NSA_GROUP, axis=1)
    win_tiles = (WINDOW + tq) // LANES
    wstart = pl.multiple_of(jnp.maximum(q0 - WINDOW, 0), tq)
    n_tiles = (q0 + tq + tk - 1) // tk

    def group_q(q_t, kvh):
        base = kvh * NSA_GROUP * HEAD_DIM
        return jnp.concatenate([q_t[base + g * HEAD_DIM: base + (g + 1) * HEAD_DIM, :] for g in range(NSA_GROUP)],
                               axis=1)

    def per_head(o_t):
        return [o_t[:, g * tq:(g + 1) * tq] for g in range(NSA_GROUP)]

    oc, osel, ow = [], [], []
    for kvh in range(NSA_KV_HEADS):
        lo, hi = kvh * HEAD_DIM, (kvh + 1) * HEAD_DIM
        s = _dot(kc[:, lo:hi], group_q(qn_t, kvh))
        n_idx = lax.broadcasted_iota(jnp.int32, s.shape, 0)
        vis = jnp.logical_and(n_idx * CMP_STRIDE + (CMP_BLOCK - 1) <= qpos4, n_idx < n_cmp)
        p = _softmax_cols(jnp.where(vis, s, -jnp.inf))
        oc += per_head(_dot(vc_t[lo:hi, :], p.astype(BF16)))
        p_sum = p[:, 0:tq]
        for g in range(1, NSA_GROUP):
            p_sum = p_sum + p[:, g * tq:(g + 1) * tq]
        p_hi, p_lo = _split(p_sum)
        imp_t = _dot(ovt_ref[...], p_hi) + _dot(ovt_ref[...], p_lo)
        sel_t = _select_blocks_t(imp_t[0:n_sel, :], qpos1, n_sel)
        sel_t = jnp.concatenate([sel_t, jnp.zeros((LANES - n_sel, tq), F32)], axis=0).astype(BF16)

        qs_t = group_q(qr_t, kvh)

        def sel_step(kt, carry, qs_t=qs_t, sel_t=sel_t, lo=lo, hi=hi):
            m_i, l_i, acc = carry
            k0 = pl.multiple_of(kt * tk, tk)
            k_blk = kvb_ref[0, pl.ds(k0, tk), lo:hi]
            v_t = jnp.concatenate([vt_ref[0, kt * (tk // LANES) + j, lo:hi, :] for j in range(tk // LANES)], axis=1)
            chosen = _dot(ext_ref[kt], sel_t)
            kpos = k0 + lax.broadcasted_iota(jnp.int32, (tk, tq), 0)
            bias = jnp.where(jnp.logical_and(chosen > 0.5, kpos <= qpos1), 0.0, -jnp.inf)
            sc = _dot(k_blk, qs_t) + jnp.concatenate([bias] * NSA_GROUP, axis=1)
            m_new = jnp.maximum(m_i, jnp.max(sc, axis=0, keepdims=True))
            alpha = jnp.exp(m_i - m_new)
            pe = jnp.exp(sc - m_new)
            l_new = alpha * l_i + jnp.sum(pe, axis=0, keepdims=True)
            acc_new = alpha * acc + _dot(v_t, pe.astype(BF16))
            return m_new, l_new, acc_new

        init = (jnp.full((1, cols), NEG, F32), jnp.zeros((1, cols), F32), jnp.zeros((HEAD_DIM, cols), F32))
        _, l_f, acc_f = lax.fori_loop(0, n_tiles, sel_step, init)
        osel += per_head(acc_f / jnp.maximum(l_f, 1e-30))

        k_w = kvb_ref[0, pl.ds(wstart, win_tiles * LANES), 2 * KV_WIDTH + lo:2 * KV_WIDTH + hi]
        w0 = wstart // LANES
        v_wt = jnp.concatenate([vt_ref[0, w0 + j, KV_WIDTH + lo:KV_WIDTH + hi, :] for j in range(win_tiles)], axis=1)
        sw = _dot(k_w, qs_t)
        rel = qpos4 - (wstart + lax.broadcasted_iota(jnp.int32, sw.shape, 0))
        inside = jnp.logical_and(rel >= 0, rel < WINDOW)
        pw = _softmax_cols(jnp.where(inside, sw, -jnp.inf))
        ow += per_head(_dot(v_wt, pw.astype(BF16)))

    oc_ref[0] = jnp.concatenate(oc, axis=0).T
    os_ref[0] = jnp.concatenate(osel, axis=0).T
    ow_ref[0] = jnp.concatenate(ow, axis=0).T


def _nsa_prompt(qn_t, qr_t, kc, vc, kvb, vt, ovt, ext):
    b, _, t = qn_t.shape
    chunks = kc.shape[1]
    qtile = lambda i, j: (i, 0, j)
    otile = lambda i, j: (i, j, 0)
    seq = lambda i, j: (i, 0, 0)
    out = jax.ShapeDtypeStruct((b, t, NSA_WIDTH), F32)
    return pl.pallas_call(
        functools.partial(_nsa_prompt_body, seq_len=t),
        grid=(b, t // Q_TILE),
        in_specs=[
            pl.BlockSpec((1, NSA_WIDTH, Q_TILE), qtile),
            pl.BlockSpec((1, NSA_WIDTH, Q_TILE), qtile),
            pl.BlockSpec((1, chunks, KV_WIDTH), seq),
            pl.BlockSpec((1, chunks, KV_WIDTH), seq),
            pl.BlockSpec((1, t, 4 * KV_WIDTH), seq),
            pl.BlockSpec((1,) + vt.shape[1:], lambda i, j: (i, 0, 0, 0)),
            pl.BlockSpec(ovt.shape, lambda i, j: (0, 0)),
            pl.BlockSpec(ext.shape, lambda i, j: (0, 0, 0)),
        ],
        out_specs=[pl.BlockSpec((1, Q_TILE, NSA_WIDTH), otile)] * 3,
        out_shape=[out] * 3,
        compiler_params=pltpu.CompilerParams(dimension_semantics=("parallel", "parallel"),
                                             vmem_limit_bytes=VMEM_LIMIT),
        name="nsa_prompt",
    )(qn_t, qr_t, kc, vc, kvb, vt, ovt, ext)


def _nsa_sample_body(*refs, n_pages, past_len, n_new):
    pages = refs[1:1 + n_pages]
    (qn_ref, qr_ref, kc_ref, vc_ref, ov_ref, ex_ref, kvb_ref, cw_ref, nw_ref,
     oc_ref, os_ref, ow_ref, neww_ref) = refs[1 + n_pages:]
    t = n_new
    rows4 = NSA_GROUP * t
    n_cmp = (past_len + t) // CMP_STRIDE - 1
    n_sel = -(-(past_len + t) // SEL_BLOCK)
    qn = qn_ref[0]
    qr = qr_ref[0]
    kc = kc_ref[0].astype(BF16)
    vc = vc_ref[0].astype(BF16)
    qpos1 = past_len + lax.broadcasted_iota(jnp.int32, (t, 1), 0)
    qpos4 = jnp.concatenate([qpos1] * NSA_GROUP, axis=0)
    qs = [_stack_heads(qr, kvh) for kvh in range(NSA_KV_HEADS)]

    oc, imp = [], []
    for kvh in range(NSA_KV_HEADS):
        lo, hi = kvh * HEAD_DIM, (kvh + 1) * HEAD_DIM
        s = _dot_nt(_stack_heads(qn, kvh), kc[:, lo:hi])
        n_idx = lax.broadcasted_iota(jnp.int32, s.shape, 1)
        vis = jnp.logical_and(n_idx * CMP_STRIDE + (CMP_BLOCK - 1) <= qpos4, n_idx < n_cmp)
        p = _softmax_rows(jnp.where(vis, s, -jnp.inf))
        oc.append(_unstack_heads(_dot(p.astype(BF16), vc[:, lo:hi]), t))
        imp.append(_dot2(_group_sum(p, t), ov_ref[...]))
    oc_ref[0] = jnp.concatenate(oc, axis=1)
    sel = _select_blocks(jnp.concatenate(imp, axis=0), jnp.concatenate([qpos1] * NSA_KV_HEADS, axis=0), n_sel)

    zero = jnp.zeros((rows4, HEAD_DIM), BF16)
    q_bd = jnp.concatenate([jnp.concatenate([qs[0], zero], axis=1), jnp.concatenate([zero, qs[1]], axis=1)], axis=0)
    sel_rows = jnp.concatenate(
        [jnp.concatenate([sel[kvh * t:(kvh + 1) * t, 0:ex_ref.shape[0]]] * NSA_GROUP, axis=0)
         for kvh in range(NSA_KV_HEADS)], axis=0).astype(BF16)
    k_t = jnp.concatenate([p[0] for p in pages], axis=1).astype(BF16)
    v_t = jnp.concatenate([p[1] for p in pages], axis=1).astype(BF16)
    chosen = _dot(sel_rows, ex_ref[...])
    sc = jnp.where(chosen > 0.5, _dot(q_bd, k_t), -jnp.inf)
    pad = jnp.zeros((LANES - t, KV_WIDTH), F32)
    k_new = jnp.concatenate([kvb_ref[0, :, 0:KV_WIDTH].astype(F32), pad], axis=0).astype(BF16)
    v_new = jnp.concatenate([kvb_ref[0, :, KV_WIDTH:2 * KV_WIDTH].astype(F32), pad], axis=0).astype(BF16)
    sn = _dot_nt(q_bd, k_new)
    key_i = lax.broadcasted_iota(jnp.int32, sn.shape, 1)
    q_t = lax.broadcasted_iota(jnp.int32, sn.shape, 0) & (t - 1)
    sn = jnp.where(key_i <= q_t, sn, -jnp.inf)
    m = jnp.maximum(jnp.max(sc, axis=-1, keepdims=True), jnp.max(sn, axis=-1, keepdims=True))
    e_past = jnp.exp(sc - m)
    e_new = jnp.exp(sn - m)
    denom = jnp.sum(e_past, axis=-1, keepdims=True) + jnp.sum(e_new, axis=-1, keepdims=True)
    o_both = (_dot_nt(e_past.astype(BF16), v_t) + _dot(e_new.astype(BF16), v_new)) / denom
    os_ref[0] = jnp.concatenate([_unstack_heads(o_both[0:rows4, 0:HEAD_DIM], t),
                                 _unstack_heads(o_both[rows4:2 * rows4, HEAD_DIM:KV_WIDTH], t)], axis=1)

    wb = cw_ref.shape[2]
    cw = cw_ref[0]
    nw_t = jnp.concatenate([nw_ref[0], jnp.zeros((LANES - t, 2 * KV_WIDTH), F32)], axis=0).T
    shifted = pltpu.roll(cw, wb - t, 1)
    lane = lax.broadcasted_iota(jnp.int32, (2 * KV_WIDTH, LANES), 1)
    neww_ref[0, :, 0:wb - LANES] = shifted[:, 0:wb - LANES]
    neww_ref[0, :, wb - LANES:wb] = jnp.where(lane >= LANES - t, pltpu.roll(nw_t, LANES - t, 1), shifted[:, wb - LANES:wb])
    keys_t = jnp.concatenate([cw, nw_t], axis=1).astype(BF16)
    out = []
    for kvh in range(NSA_KV_HEADS):
        lo, hi = kvh * HEAD_DIM, (kvh + 1) * HEAD_DIM
        sw = _dot(qs[kvh], keys_t[lo:hi, :])
        key_i = lax.broadcasted_iota(jnp.int32, sw.shape, 1)
        q_t = lax.broadcasted_iota(jnp.int32, sw.shape, 0) & (t - 1)
        rel = wb + q_t - key_i
        inside = jnp.logical_and(rel >= 0, rel < WINDOW)
        pw = _softmax_rows(jnp.where(inside, sw, -jnp.inf))
        out.append(_unstack_heads(_dot_nt(pw.astype(BF16), keys_t[KV_WIDTH + lo:KV_WIDTH + hi, :]), t))
    ow_ref[0] = jnp.concatenate(out, axis=1)


def _nsa_sample(cache_t, page_table, qn, qr, kc, vc, ov, ex, kvb, cache_win_t, new_win, past_len):
    bs, n_pages = page_table.shape
    t = qr.shape[1]
    chunks = kc.shape[1]
    wb = cache_win_t.shape[2]
    seq = lambda b, pt: (b, 0, 0)
    c2 = lambda b, pt: (0, 0)
    grid_spec = pltpu.PrefetchScalarGridSpec(
        num_scalar_prefetch=1,
        grid=(bs,),
        in_specs=_page_specs(1, n_pages) + [
            pl.BlockSpec((1, t, NSA_WIDTH), seq), pl.BlockSpec((1, t, NSA_WIDTH), seq),
            pl.BlockSpec((1, chunks, KV_WIDTH), seq), pl.BlockSpec((1, chunks, KV_WIDTH), seq),
            pl.BlockSpec(ov.shape, c2), pl.BlockSpec(ex.shape, c2),
            pl.BlockSpec((1, t, 4 * KV_WIDTH), seq),
            pl.BlockSpec((1, 2 * KV_WIDTH, wb), seq), pl.BlockSpec((1, t, 2 * KV_WIDTH), seq),
        ],
        out_specs=[pl.BlockSpec((1, t, NSA_WIDTH), seq)] * 3 + [pl.BlockSpec((1, 2 * KV_WIDTH, wb), seq)],
    )
    attn = jax.ShapeDtypeStruct((bs, t, NSA_WIDTH), F32)
    return pl.pallas_call(
        functools.partial(_nsa_sample_body, n_pages=n_pages, past_len=past_len, n_new=t),
        grid_spec=grid_spec,
        out_shape=[attn, attn, attn, jax.ShapeDtypeStruct((bs, 2 * KV_WIDTH, wb), F32)],
        compiler_params=pltpu.CompilerParams(dimension_semantics=("parallel",), vmem_limit_bytes=VMEM_LIMIT),
        name="nsa_sample",
    )(page_table, *([cache_t] * n_pages), qn, qr, kc, vc, ov, ex, kvb, cache_win_t, new_win)


def _gla_constants(n_seq):
    tile = GLA_TILE
    seq_len = tile // n_seq
    t = np.arange(tile)[:, None]
    r = np.arange(tile)[None, :]
    same = (t // seq_len) == (r // seq_len)
    ops = [same & (r <= t), same & (r > t)]
    masks = []
    m = 1
    while 2 * m <= seq_len:
        upper = (t % (2 * m)) >= m
        mid = (t // (2 * m)) * (2 * m) + m - 1
        ops.append(upper & (r > mid) & (r <= t))
        ops.append((~upper) & (r > t) & (r <= mid))
        same_blk = (t // (2 * m)) == (r // (2 * m))
        masks.append(same_blk & upper & ((r % (2 * m)) < m))
        m *= 2
    ops = np.concatenate([o.astype(np.float32) for o in ops], axis=0)
    masks = np.stack([k.astype(np.float32) for k in masks], axis=0)
    lane_seq = (np.arange(n_seq * GLA_DK)[None, :] // GLA_DK) == (np.arange(tile)[:, None] // seq_len)
    rep = (np.arange(n_seq * GLA_DK)[None, :] % GLA_DK) == np.arange(GLA_DK)[:, None]
    return (jnp.asarray(ops, BF16), jnp.asarray(masks, F32), jnp.asarray(lane_seq.astype(np.float32), F32),
            jnp.asarray(rep.astype(np.float32), BF16))


def _gla_body(gq_ref, gk_ref, gv_ref, gg_ref, misc_ref, s0_ref, wa_hi_ref, wa_lo_ref, ba_ref, ops_ref, masks_ref,
              lane_seq_ref, rep_ref, gnorm_ref, o_ref, sfin_ref, st_sc, *, n_levels):
    step = pl.program_id(1)

    @pl.when(step == 0)
    def _():
        st_sc[...] = s0_ref[...]

    for g in range(gq_ref.shape[0]):
        _gla_tile(g, gq_ref, gk_ref, gv_ref, gg_ref, misc_ref, wa_hi_ref, wa_lo_ref, ba_ref, ops_ref, masks_ref,
                  lane_seq_ref, rep_ref, gnorm_ref, o_ref, st_sc, n_levels)

    @pl.when(step == pl.num_programs(1) - 1)
    def _():
        sfin_ref[...] = st_sc[...]


def _gla_tile(g, gq_ref, gk_ref, gv_ref, gg_ref, misc_ref, wa_hi_ref, wa_lo_ref, ba_ref, ops_ref, masks_ref,
              lane_seq_ref, rep_ref, gnorm_ref, o_ref, st_sc, n_levels):
    tile = GLA_TILE
    m_hi, m_lo = _split(misc_ref[g])
    x = _dot(m_hi, wa_hi_ref[...]) + _dot(m_hi, wa_lo_ref[...]) + _dot(m_lo, wa_hi_ref[...]) + ba_ref[...]
    la = (jnp.minimum(x, 0.0) - jnp.log(1.0 + jnp.exp(-jnp.abs(x)))) * (1.0 / GLA_GATE_TAU)
    la_hi, la_lo = _split(la)
    cums = _dot(ops_ref[...], la_hi) + _dot(ops_ref[...], la_lo)
    lane_seq = lane_seq_ref[...]
    rep = rep_ref[...]
    multi = lane_seq.shape[1] > GLA_DK

    def spread(a):
        a = a.astype(BF16)
        return (_dot(a, rep) * lane_seq).astype(BF16) if multi else a

    for h in range(GLA_HEADS):
        dk = slice(h * GLA_DK, (h + 1) * GLA_DK)
        dv = slice(h * GLA_DV, (h + 1) * GLA_DV)
        q = gq_ref[g, :, dk] * (GLA_DK ** -0.5)
        k = gk_ref[g, :, dk]
        v = gv_ref[g, :, dv]
        v_b = v.astype(BF16)
        st = st_sc[g, h]
        since_start = cums[0:tile, dk]
        until_end = cums[tile:2 * tile, dk]
        o = _dot_nt(spread(q * jnp.exp(since_start)), st.astype(BF16))
        a = jnp.zeros((tile, tile), F32)
        for lv in range(n_levels):
            dq = cums[(2 + 2 * lv) * tile:(3 + 2 * lv) * tile, dk]
            dkk = cums[(3 + 2 * lv) * tile:(4 + 2 * lv) * tile, dk]
            a = a + _dot_nt((q * jnp.exp(dq)).astype(BF16), (k * jnp.exp(dkk)).astype(BF16)) * masks_ref[lv]
        o = o + _dot(a.astype(BF16), v_b) + jnp.sum(q * k, axis=-1, keepdims=True) * v
        la_h = la[:, dk]
        if multi:
            total = jnp.sum(_dot2(la_h, rep) * lane_seq, axis=0, keepdims=True)
        else:
            total = jnp.sum(la_h, axis=0, keepdims=True)
        st_new = st * jnp.exp(total) + _dot(v.T.astype(BF16), spread(k * jnp.exp(until_end)))
        st_sc[g, h] = st_new
        ms = jnp.mean(o * o, axis=-1, keepdims=True)
        o_ref[g, :, dv] = o * lax.rsqrt(ms + EPS) * gnorm_ref[...] * _silu(gg_ref[g, :, dv])


def _gla(gq, gk, gv, gg, misc, s0_t, wa_hi, wa_lo, ba, gnorm, n_seq):
    groups, rows, _ = gq.shape
    steps = rows // GLA_TILE
    par = GLA_GROUPS_PER_STEP
    assert groups % par == 0
    ops, masks, lane_seq, rep = _gla_constants(n_seq)
    n_levels = masks.shape[0]
    tile = lambda g, s: (g, s, 0)
    c2 = lambda g, s: (0, 0)
    c3 = lambda g, s: (0, 0, 0)
    st_block = (par, GLA_HEADS, GLA_DV, n_seq * GLA_DK)
    st_map = lambda g, s: (g, 0, 0, 0)
    return pl.pallas_call(
        functools.partial(_gla_body, n_levels=n_levels),
        grid=(groups // par, steps),
        in_specs=[
            pl.BlockSpec((par, GLA_TILE, GLA_QK_WIDTH), tile), pl.BlockSpec((par, GLA_TILE, GLA_QK_WIDTH), tile),
            pl.BlockSpec((par, GLA_TILE, GLA_WIDTH), tile), pl.BlockSpec((par, GLA_TILE, GLA_WIDTH), tile),
            pl.BlockSpec((par, GLA_TILE, MISC_WIDTH), tile),
            pl.BlockSpec(st_block, st_map),
            pl.BlockSpec(wa_hi.shape, c2), pl.BlockSpec(wa_lo.shape, c2), pl.BlockSpec(ba.shape, c2),
            pl.BlockSpec(ops.shape, c2), pl.BlockSpec(masks.shape, c3),
            pl.BlockSpec(lane_seq.shape, c2), pl.BlockSpec(rep.shape, c2), pl.BlockSpec(gnorm.shape, c2),
        ],
        out_specs=[pl.BlockSpec((par, GLA_TILE, GLA_WIDTH), tile), pl.BlockSpec(st_block, st_map)],
        out_shape=[jax.ShapeDtypeStruct((groups, rows, GLA_WIDTH), F32),
                   jax.ShapeDtypeStruct((groups,) + st_block[1:], F32)],
        scratch_shapes=[pltpu.VMEM(st_block, F32)],
        compiler_params=pltpu.CompilerParams(dimension_semantics=("parallel", "arbitrary"),
                                             vmem_limit_bytes=VMEM_LIMIT),
        name="gla",
    )(gq, gk, gv, gg, misc, s0_t, wa_hi, wa_lo, ba, ops, masks, lane_seq, rep, gnorm)


def _out_body(x_ref, oc_ref, os_ref, ow_ref, og_ref, misc_ref, gex_ref, wout_ref, gffn_ref, wa_ref, wb_ref, wdown_ref,
              y_ref, hn_sc):
    c = pl.program_id(1)

    @pl.when(c == 0)
    def _():
        gates = 1.0 / (1.0 + jnp.exp(-misc_ref[...]))
        branch = (oc_ref, os_ref, ow_ref)
        o_nsa = None
        for i in range(3):
            g_i = _dot2(gates, gex_ref[i])
            term = g_i * branch[i][...]
            o_nsa = term if o_nsa is None else o_nsa + term
        x1 = (x_ref[...] + _dot(o_nsa.astype(BF16), wout_ref[0:NSA_WIDTH, :])
              + _dot(og_ref[...].astype(BF16), wout_ref[NSA_WIDTH:NSA_WIDTH + GLA_WIDTH, :]))
        ms = jnp.mean(x1 * x1, axis=-1, keepdims=True)
        hn_sc[...] = (x1 * lax.rsqrt(ms + EPS) * gffn_ref[...]).astype(BF16)
        y_ref[...] = x1

    hn = hn_sc[...]
    act = _silu(_dot(hn, wa_ref[...])) * _dot(hn, wb_ref[...])
    y_ref[...] += _dot(act.astype(BF16), wdown_ref[...])


def _out(x2d, oc, osel, ow, og, misc, gex, wout, gffn, wup, wdown):
    n = x2d.shape[0]
    tm = min(OUT_ROW_TILE, n)
    assert n % tm == 0
    n_chunks = FFN_HIDDEN // FFN_CHUNK
    row = lambda i, c: (i, 0)
    c2 = lambda i, c: (0, 0)
    return pl.pallas_call(
        _out_body,
        grid=(n // tm, n_chunks),
        in_specs=[
            pl.BlockSpec((tm, D_MODEL), row),
            pl.BlockSpec((tm, NSA_WIDTH), row), pl.BlockSpec((tm, NSA_WIDTH), row), pl.BlockSpec((tm, NSA_WIDTH), row),
            pl.BlockSpec((tm, GLA_WIDTH), row), pl.BlockSpec((tm, MISC_WIDTH), row),
            pl.BlockSpec(gex.shape, lambda i, c: (0, 0, 0)),
            pl.BlockSpec(wout.shape, c2),
            pl.BlockSpec(gffn.shape, c2),
            pl.BlockSpec((D_MODEL, FFN_CHUNK), lambda i, c: (0, c)),
            pl.BlockSpec((D_MODEL, FFN_CHUNK), lambda i, c: (0, n_chunks + c)),
            pl.BlockSpec((FFN_CHUNK, D_MODEL), lambda i, c: (c, 0)),
        ],
        out_specs=pl.BlockSpec((tm, D_MODEL), row),
        out_shape=jax.ShapeDtypeStruct((n, D_MODEL), F32),
        scratch_shapes=[pltpu.VMEM((tm, D_MODEL), BF16)],
        compiler_params=pltpu.CompilerParams(dimension_semantics=("parallel", "arbitrary"),
                                             vmem_limit_bytes=VMEM_LIMIT),
        name="out",
    )(x2d, oc, osel, ow, og, misc, gex, wout, gffn, wup, wup, wdown)


def _rope_tables(pos):
    half = HEAD_DIM // 2
    inv = ROPE_THETA ** (-jnp.arange(half, dtype=F32) / half)
    ang = pos.astype(F32)[:, None] * inv[None, :]
    cos, sin = jnp.cos(ang), jnp.sin(ang)
    reps = LANES // HEAD_DIM
    return jnp.tile(jnp.concatenate([cos, cos], axis=1), (1, reps)), jnp.tile(jnp.concatenate([-sin, sin], axis=1), (1, reps))


def _overlap_matrix(n_chunks, n_cols):
    start = np.arange(n_chunks)[:, None] * CMP_STRIDE
    j0 = np.arange(n_cols)[None, :] * SEL_BLOCK
    return jnp.asarray(((start < j0 + SEL_BLOCK) & (start + CMP_BLOCK > j0)).astype(np.float32), BF16)


def _expand_matrix(n_tiles, keys):
    key = np.arange(n_tiles)[:, None, None] * keys + np.arange(keys)[None, None, :]
    blk = np.arange(LANES)[None, :, None]
    return jnp.asarray((key // SEL_BLOCK == blk).astype(np.float32), BF16)


def _block_diag_ones(width):
    i = np.arange(width)
    return jnp.asarray((i[:, None] // HEAD_DIM == i[None, :] // HEAD_DIM).astype(np.float32), BF16)


def _gate_expand():
    lane = np.arange(MISC_WIDTH)[None, :, None]
    out = np.arange(NSA_WIDTH)[None, None, :]
    c = np.arange(3)[:, None, None]
    return jnp.asarray(((lane == (out // HEAD_DIM) * 3 + c) & (lane < GATE_COLS)).astype(np.float32), BF16)


def _pad_cols(w, width):
    return jnp.pad(w, ((0, 0), (0, width - w.shape[1])))


def kernel(x_prompt, x_sample, cache_nsa_kv, cache_win_kv, state_gla, page_table, w_norm_mix, w_in, nsa_q_gain,
           nsa_k_gain, cmp_pos_k, cmp_w1_k, cmp_w2_k, cmp_pos_v, cmp_w1_v, cmp_w2_v, gla_w_a2, gla_b_a,
           gla_norm_gain, w_out, w_norm_ffn, w_ffn_up, w_ffn_down):
    b_p, t_p, _ = x_prompt.shape
    b_s, t_s, _ = x_sample.shape
    depth = w_in.shape[0]
    assert depth == 1, "single layer"
    n_pages = page_table.shape[1]
    past_len = n_pages * PAGE_SIZE
    n_pool = cache_nsa_kv.shape[1]
    wb = cache_win_kv.shape[2]
    assert t_p % SEL_KEY_TILE == 0 and t_p >= WINDOW + Q_TILE and (b_p * t_p) % ROW_TILE == 0
    assert (b_s * t_s) % ROW_TILE == 0 and GLA_TILE % t_s == 0 and b_s % (GLA_TILE // t_s) == 0
    assert wb == WINDOW and ROW_TILE % t_s == 0 and t_p % PAGE_SIZE == 0 and past_len // SEL_BLOCK <= LANES
    l = 0

    q_w, kv_w, gate_w, gq_w, gk_w, gv_w, alow_w, gg_w = jnp.split(
        w_in[l], np.cumsum([NSA_WIDTH, 6 * KV_WIDTH, GATE_COLS, GLA_QK_WIDTH, GLA_QK_WIDTH, GLA_WIDTH,
                            GLA_GATE_RANK])[:].tolist(), axis=1)
    misc_w = _pad_cols(jnp.concatenate([gate_w, alow_w], axis=1), MISC_WIDTH)
    w_perm = jnp.concatenate([q_w, kv_w, gq_w, gk_w, gv_w, gg_w, misc_w], axis=1).astype(BF16)
    gmix = w_norm_mix[l][None, :]
    qg = jnp.tile(nsa_q_gain[l], NSA_HEADS)[None, :]
    kg = [jnp.tile(nsa_k_gain[l, i], NSA_KV_HEADS)[None, :] for i in range(3)]
    ones512 = _block_diag_ones(NSA_WIDTH)
    ones128 = _block_diag_ones(LANES)
    wa = jnp.zeros((MISC_WIDTH, GLA_QK_WIDTH), F32).at[GATE_COLS:GATE_COLS + GLA_GATE_RANK].set(gla_w_a2[l])
    wa_hi, wa_lo = _split(wa)
    ba = gla_b_a[l][None, :]
    gnorm = gla_norm_gain[l][None, :]
    wk1 = _cmp1_weights(cmp_w1_k[l])
    wv1 = _cmp1_weights(cmp_w1_v[l])
    eye2 = jnp.eye(NSA_KV_HEADS, dtype=F32)
    w2k_bd = jnp.einsum("ed,hg->hegd", cmp_w2_k[l], eye2).reshape(2 * CMP_HIDDEN, KV_WIDTH).astype(BF16)
    w2v_bd = jnp.einsum("ed,hg->hegd", cmp_w2_v[l], eye2).reshape(2 * CMP_HIDDEN, KV_WIDTH).astype(BF16)
    flat = CMP_BLOCK * HEAD_DIM
    posk, posv = cmp_pos_k[l].reshape(1, flat), cmp_pos_v[l].reshape(1, flat)
    w1k_flat, w1v_flat = cmp_w1_k[l].reshape(flat, CMP_HIDDEN), cmp_w1_v[l].reshape(flat, CMP_HIDDEN)
    cmp_consts = (wk1, wv1, posk, w1k_flat, posv, w1v_flat, w2k_bd, w2v_bd, kg[0], ones128)
    gex = _gate_expand()
    wout = w_out[l].astype(BF16)
    gffn = w_norm_ffn[l][None, :]
    wup = w_ffn_up[l].astype(BF16)
    wdown = w_ffn_down[l].astype(BF16)

    cos_p, sin_p = _rope_tables(jnp.arange(t_p))
    xp2 = x_prompt.reshape(b_p * t_p, D_MODEL)
    (qn_t, qr_t, rows_t, win_t, kvb, vt, gq, gk, gv, gg, misc) = _proj(
        xp2, gmix, w_perm, cos_p, sin_p, qg, kg[1], kg[2], ones512, t_p // ROW_TILE, seq_len=t_p)
    kc_p, vc_p = _cmp_prompt(rows_t.reshape(b_p, 4, KV_WIDTH, t_p), cmp_consts)
    ovt_p = _overlap_matrix(t_p // CMP_STRIDE, LANES).T
    ext_p = jnp.swapaxes(_expand_matrix(t_p // SEL_KEY_TILE, SEL_KEY_TILE), 1, 2)
    oc, osel, ow = _nsa_prompt(qn_t, qr_t, kc_p, vc_p, kvb.reshape(b_p, t_p, -1),
                               vt.reshape(b_p, t_p // LANES, 2 * KV_WIDTH, LANES), ovt_p, ext_p)
    s0_p = jnp.zeros((b_p, GLA_HEADS, GLA_DV, GLA_DK), F32)
    r3 = lambda a, g=b_p: a.reshape(g, -1, a.shape[-1])
    og, sfin_p = _gla(r3(gq), r3(gk), r3(gv), r3(gg), r3(misc), s0_p, wa_hi, wa_lo, ba, gnorm, 1)
    y_p = _out(xp2, oc.reshape(b_p * t_p, -1), osel.reshape(b_p * t_p, -1), ow.reshape(b_p * t_p, -1),
               og.reshape(b_p * t_p, -1), misc, gex, wout, gffn, wup, wdown)
    out_rows_p = rows_t.reshape(b_p, 4, NSA_KV_HEADS, HEAD_DIM, t_p).transpose(0, 4, 1, 2, 3)[None]
    wlen = min(WINDOW, t_p)
    out_win_p = win_t[:, :, t_p - wlen:].reshape(b_p, 2, NSA_KV_HEADS, HEAD_DIM, wlen).transpose(0, 4, 1, 2, 3)[None]
    out_gla_p = jnp.swapaxes(sfin_p, 2, 3)[None].astype(state_gla.dtype)

    pos_s = past_len + jnp.arange(t_s)
    cos_s, sin_s = _rope_tables(pos_s)
    cos_s = jnp.tile(cos_s, (ROW_TILE // t_s, 1))
    sin_s = jnp.tile(sin_s, (ROW_TILE // t_s, 1))
    xs2 = x_sample.reshape(b_s * t_s, D_MODEL)
    (qn_s, qr_s, rows_s, win_s, kvb_s, _, gq_s, gk_s, gv_s, gg_s, misc_s) = _proj(
        xs2, gmix, w_perm, cos_s, sin_s, qg, kg[1], kg[2], ones512, 1)
    cache_t = cache_nsa_kv[l].transpose(0, 2, 3, 4, 1).reshape(n_pool, 4, KV_WIDTH, PAGE_SIZE)
    kc_s, vc_s = _cmp_sample(cache_t, page_table, cmp_consts)
    n_sel_s = -(-(past_len + t_s) // SEL_BLOCK)
    sel_cols = -(-n_sel_s // LANES) * LANES
    ov_s = _overlap_matrix(past_len // CMP_STRIDE, sel_cols)
    ex_s = _expand_matrix(1, past_len)[0]
    seq3 = lambda a: a.reshape(b_s, t_s, a.shape[-1])
    cache_win_t = cache_win_kv[l].transpose(0, 2, 3, 4, 1).reshape(b_s, 2 * KV_WIDTH, wb)
    oc_s, os_s, ow_s, new_win_t = _nsa_sample(cache_t, page_table, seq3(qn_s), seq3(qr_s), kc_s, vc_s, ov_s, ex_s,
                                              seq3(kvb_s), cache_win_t, seq3(win_s), past_len)
    n_seq = GLA_TILE // t_s
    groups = b_s // n_seq
    s0_s = state_gla[l].astype(F32).reshape(groups, n_seq, GLA_HEADS, GLA_DK, GLA_DV)
    s0_s = s0_s.transpose(0, 2, 4, 1, 3).reshape(groups, GLA_HEADS, GLA_DV, n_seq * GLA_DK)
    rg = lambda a: a.reshape(groups, GLA_TILE, a.shape[-1])
    og_s, sfin_s = _gla(rg(gq_s), rg(gk_s), rg(gv_s), rg(gg_s), rg(misc_s), s0_s, wa_hi, wa_lo, ba, gnorm, n_seq)
    y_s = _out(xs2, oc_s.reshape(b_s * t_s, -1), os_s.reshape(b_s * t_s, -1), ow_s.reshape(b_s * t_s, -1),
               og_s.reshape(b_s * t_s, -1), misc_s, gex, wout, gffn, wup, wdown)
    out_rows_s = rows_s.reshape(1, b_s, t_s, 4, NSA_KV_HEADS, HEAD_DIM)
    out_win_s = new_win_t.reshape(b_s, 2, NSA_KV_HEADS, HEAD_DIM, wb).transpose(0, 4, 1, 2, 3)[None]
    out_gla_s = sfin_s.reshape(groups, GLA_HEADS, GLA_DV, n_seq, GLA_DK).transpose(0, 3, 1, 4, 2)
    out_gla_s = out_gla_s.reshape(1, b_s, GLA_HEADS, GLA_DK, GLA_DV).astype(state_gla.dtype)

    return (y_p.reshape(b_p, t_p, D_MODEL), y_s.reshape(b_s, t_s, D_MODEL), out_rows_p, out_win_p, out_gla_p,
            out_rows_s, out_win_s, out_gla_s)
```

```python
import functools
import math

import numpy as np
import jax
import jax.numpy as jnp
from jax import lax
from jax.experimental import pallas as pl
from jax.experimental.pallas import tpu as pltpu

F32 = jnp.float32
BF16 = jnp.bfloat16

D_MODEL = 1024
PAGE_SIZE = 128
NSA_HEADS = 8
NSA_KV_HEADS = 2
NSA_GROUP = NSA_HEADS // NSA_KV_HEADS
HEAD_DIM = 64
NSA_WIDTH = NSA_HEADS * HEAD_DIM
KV_WIDTH = NSA_KV_HEADS * HEAD_DIM
CMP_BLOCK = 32
CMP_STRIDE = 16
CMP_HIDDEN = 2 * HEAD_DIM
SEL_BLOCK = 64
SEL_TOPK = 16
WINDOW = 512
FORCE_SCORE = 1.0e4
GLA_HEADS = 4
GLA_DK = 64
GLA_DV = 128
GLA_WIDTH = GLA_HEADS * GLA_DV
GLA_QK_WIDTH = GLA_HEADS * GLA_DK
GLA_GATE_RANK = 16
GLA_GATE_TAU = 16.0
FFN_HIDDEN = -(-8 * D_MODEL // (3 * 256)) * 256
ROPE_THETA = 10000.0
EPS = 1e-6
NEG = -1.0e30

LANES = 128
MISC_WIDTH = LANES
GATE_COLS = 3 * NSA_HEADS
PROJ_WIDTH = NSA_WIDTH + 6 * KV_WIDTH + 2 * GLA_QK_WIDTH + 2 * GLA_WIDTH + MISC_WIDTH
ROW_TILE = 512
OUT_ROW_TILE = 1024
Q_TILE = 128
SEL_KEY_TILE = 512
ONES_ROWS = 16
GLA_TILE = 128
GLA_GROUPS_PER_STEP = 2
FFN_CHUNK = 256
VMEM_LIMIT = 56 * 1024 * 1024


def _dot(a, b):
    return jnp.dot(a, b, preferred_element_type=F32)


def _dot_nt(a, b):
    return lax.dot_general(a, b, (((1,), (1,)), ((), ())), preferred_element_type=F32)


def _split(x):
    hi = x.astype(BF16)
    lo = (x - hi.astype(F32)).astype(BF16)
    return hi, lo


def _dot2(x, w_bf16):
    hi, lo = _split(x)
    return _dot(hi, w_bf16) + _dot(lo, w_bf16)


def _silu(x):
    return x * (1.0 / (1.0 + jnp.exp(-x)))


def _head_norm(v, ones_bd, gain):
    ss = _dot((v * v).astype(BF16), ones_bd)
    return v * lax.rsqrt(ss * (1.0 / HEAD_DIM) + EPS) * gain


def _rope(v, cos, sin_signed):
    width = v.shape[-1]
    reps = width // LANES
    if reps > 1:
        cos = jnp.concatenate([cos] * reps, axis=1)
        sin_signed = jnp.concatenate([sin_signed] * reps, axis=1)
    lane = lax.broadcasted_iota(jnp.int32, v.shape, 1)
    first_half = (lane & (HEAD_DIM - 1)) < (HEAD_DIM // 2)
    rot = jnp.where(first_half, pltpu.roll(v, width - HEAD_DIM // 2, 1), pltpu.roll(v, HEAD_DIM // 2, 1))
    return v * cos + rot * sin_signed


def _softmax_rows(s):
    m = jnp.max(s, axis=-1, keepdims=True)
    m = jnp.where(m == -jnp.inf, 0.0, m)
    e = jnp.exp2(s - m)
    return e / jnp.maximum(jnp.sum(e, axis=-1, keepdims=True), 1e-30)


def _proj_body(x_ref, gmix_ref, w_ref, cos_ref, sin_ref, qg_ref, kg1_ref, kg2_ref, ones_ref,
               qn_ref, qr_ref, rows_ref, win_ref, kvb_ref, vt_ref, gq_ref, gk_ref, gv_ref, gg_ref, misc_ref, *,
               rows_minor):
    x = x_ref[...]
    ms = jnp.mean(x * x, axis=-1, keepdims=True)
    h = (x * lax.rsqrt(ms + EPS) * gmix_ref[...]).astype(BF16)
    cos = cos_ref[...]
    sin = sin_ref[...]
    ones512 = ones_ref[...]
    ones128 = ones512[:LANES, :LANES]
    scale = HEAD_DIM ** -0.5 * math.log2(math.e)

    off = 0
    q = _dot(h, w_ref[:, off:off + NSA_WIDTH])
    off += NSA_WIDTH
    qn = _head_norm(q, ones512, qg_ref[...])
    qr = _rope(qn, cos, sin)
    if rows_minor:
        qn_ref[0] = (qn * scale).T.astype(BF16)
        qr_ref[0] = (qr * scale).T.astype(BF16)
    else:
        qn_ref[...] = (qn * scale).astype(BF16)
        qr_ref[...] = (qr * scale).astype(BF16)

    kv = _dot(h, w_ref[:, off:off + 6 * KV_WIDTH])
    off += 6 * KV_WIDTH
    k_slc = _rope(_head_norm(kv[:, 2 * KV_WIDTH:3 * KV_WIDTH], ones128, kg1_ref[...]), cos, sin)
    k_win = _rope(_head_norm(kv[:, 4 * KV_WIDTH:5 * KV_WIDTH], ones128, kg2_ref[...]), cos, sin)
    v_slc = kv[:, 3 * KV_WIDTH:4 * KV_WIDTH]
    v_win = kv[:, 5 * KV_WIDTH:6 * KV_WIDTH]
    v_slc_t = v_slc.T
    v_win_t = v_win.T
    for j in range(v_slc.shape[0] // LANES):
        vt_ref[j, 0:KV_WIDTH, :] = v_slc_t[:, j * LANES:(j + 1) * LANES].astype(BF16)
        vt_ref[j, KV_WIDTH:2 * KV_WIDTH, :] = v_win_t[:, j * LANES:(j + 1) * LANES].astype(BF16)
    if rows_minor:
        rows_ref[0, 0:2 * KV_WIDTH, :] = kv[:, 0:2 * KV_WIDTH].T
        rows_ref[0, 2 * KV_WIDTH:3 * KV_WIDTH, :] = k_slc.T
        rows_ref[0, 3 * KV_WIDTH:4 * KV_WIDTH, :] = v_slc_t
        win_ref[0, 0:KV_WIDTH, :] = k_win.T
        win_ref[0, KV_WIDTH:2 * KV_WIDTH, :] = v_win_t
    else:
        rows_ref[:, 0:2 * KV_WIDTH] = kv[:, 0:2 * KV_WIDTH]
        rows_ref[:, 2 * KV_WIDTH:3 * KV_WIDTH] = k_slc
        rows_ref[:, 3 * KV_WIDTH:4 * KV_WIDTH] = v_slc
        win_ref[:, 0:KV_WIDTH] = k_win
        win_ref[:, KV_WIDTH:2 * KV_WIDTH] = v_win
    kvb_ref[:, 0:KV_WIDTH] = k_slc.astype(BF16)
    kvb_ref[:, KV_WIDTH:2 * KV_WIDTH] = v_slc.astype(BF16)
    kvb_ref[:, 2 * KV_WIDTH:3 * KV_WIDTH] = k_win.astype(BF16)
    kvb_ref[:, 3 * KV_WIDTH:4 * KV_WIDTH] = v_win.astype(BF16)

    gq_ref[...] = _dot(h, w_ref[:, off:off + GLA_QK_WIDTH])
    off += GLA_QK_WIDTH
    gk_ref[...] = _dot(h, w_ref[:, off:off + GLA_QK_WIDTH])
    off += GLA_QK_WIDTH
    gv_ref[...] = _dot(h, w_ref[:, off:off + GLA_WIDTH])
    off += GLA_WIDTH
    gg_ref[...] = _dot(h, w_ref[:, off:off + GLA_WIDTH])
    off += GLA_WIDTH
    misc_ref[...] = _dot(h, w_ref[:, off:off + MISC_WIDTH])


def _proj(x2d, gmix, w_perm, cos_t, sin_t, qg, kg1, kg2, ones512, table_blocks, seq_len=None):
    n = x2d.shape[0]
    tm = ROW_TILE
    row = lambda i: (i, 0)
    const = lambda i: (0, 0)
    tab = lambda i: (i % table_blocks, 0)
    widths = [(NSA_WIDTH, BF16), (NSA_WIDTH, BF16), (4 * KV_WIDTH, F32), (2 * KV_WIDTH, F32), (4 * KV_WIDTH, BF16), None,
              (GLA_QK_WIDTH, F32), (GLA_QK_WIDTH, F32), (GLA_WIDTH, F32), (GLA_WIDTH, F32), (MISC_WIDTH, F32)]
    out_specs = [None if w is None else pl.BlockSpec((tm, w[0]), row) for w in widths]
    out_shape = [None if w is None else jax.ShapeDtypeStruct((n, w[0]), w[1]) for w in widths]
    out_specs[5] = pl.BlockSpec((tm // LANES, 2 * KV_WIDTH, LANES), lambda i: (i, 0, 0))
    out_shape[5] = jax.ShapeDtypeStruct((n // LANES, 2 * KV_WIDTH, LANES), BF16)
    if seq_len is not None:
        tiles = seq_len // tm
        for i in (0, 1, 2, 3):
            w, dt = widths[i]
            out_specs[i] = pl.BlockSpec((1, w, tm), lambda i: (i // tiles, 0, i % tiles))
            out_shape[i] = jax.ShapeDtypeStruct((n // seq_len, w, seq_len), dt)
    return pl.pallas_call(
        functools.partial(_proj_body, rows_minor=seq_len is not None),
        grid=(n // tm,),
        in_specs=[
            pl.BlockSpec((tm, D_MODEL), row),
            pl.BlockSpec((1, D_MODEL), const),
            pl.BlockSpec((D_MODEL, PROJ_WIDTH), const),
            pl.BlockSpec((tm, LANES), tab),
            pl.BlockSpec((tm, LANES), tab),
            pl.BlockSpec((1, NSA_WIDTH), const),
            pl.BlockSpec((1, KV_WIDTH), const),
            pl.BlockSpec((1, KV_WIDTH), const),
            pl.BlockSpec((NSA_WIDTH, NSA_WIDTH), const),
        ],
        out_specs=out_specs,
        out_shape=out_shape,
        compiler_params=pltpu.CompilerParams(dimension_semantics=("parallel",), vmem_limit_bytes=VMEM_LIMIT),
        name="proj",
    )(x2d, gmix, w_perm, cos_t, sin_t, qg, kg1, kg2, ones512)


def _cmp_body(*refs, n_src, n_prefetch):
    refs = refs[n_prefetch:]
    n_refs = len(refs) - 15
    src = refs[:n_refs]
    (perm_ref, wk_ref, wv_ref, posk_ref, w1k_ref, posv_ref, w1v_ref, w2k_ref, w2v_ref, kg0_ref, ones_ref,
     kc_ref, vc_ref, xk_sc, xv_sc) = refs[n_refs:]
    chunks_per_slab = PAGE_SIZE // CMP_STRIDE

    def slab(i):
        if n_refs == 1:
            return src[0][:, :, i * PAGE_SIZE:(i + 1) * PAGE_SIZE].reshape(2 * KV_WIDTH, PAGE_SIZE)
        return src[i][...].reshape(2 * KV_WIDTH, PAGE_SIZE)

    for i in range(0, n_src, 2):
        two = jnp.concatenate([slab(i), slab(i + 1)], axis=1).astype(BF16)
        by_row = _dot_nt(perm_ref[...], two)
        for j in range(2):
            for s in range(CMP_STRIDE):
                piece = by_row[j * PAGE_SIZE + s * chunks_per_slab:j * PAGE_SIZE + (s + 1) * chunks_per_slab, :]
                rows = slice((i + j) * chunks_per_slab, (i + j + 1) * chunks_per_slab)
                xk_sc[rows, s * KV_WIDTH:(s + 1) * KV_WIDTH] = piece[:, 0:KV_WIDTH]
                xv_sc[rows, s * KV_WIDTH:(s + 1) * KV_WIDTH] = piece[:, KV_WIDTH:2 * KV_WIDTH]
    n_chunks = n_src * chunks_per_slab
    acc_k = _dot(xk_sc[...].astype(BF16), wk_ref[...])
    acc_v = _dot(xv_sc[...].astype(BF16), wv_ref[...])
    hd = CMP_HIDDEN

    def pos_term(pos_ref, w1_ref):
        p = jnp.broadcast_to(pos_ref[...], (8, pos_ref.shape[1]))
        p_hi, p_lo = _split(p)
        w_hi, w_lo = _split(w1_ref[...])
        return (_dot(p_hi, w_hi) + _dot(p_hi, w_lo) + _dot(p_lo, w_hi))[0:1, :]

    def hidden(acc, pos):
        out = []
        for h in range(NSA_KV_HEADS):
            j0 = acc[:, 2 * h * hd:(2 * h + 1) * hd]
            j1 = acc[:, (2 * h + 1) * hd:(2 * h + 2) * hd]
            out.append(j0 + pltpu.roll(j1, n_chunks - 1, 0) + pos)
        return _silu(jnp.concatenate(out, axis=1)).astype(BF16)

    kc = _dot(hidden(acc_k, pos_term(posk_ref, w1k_ref)), w2k_ref[...])
    kc_ref[0] = _head_norm(kc, ones_ref[...], kg0_ref[...])
    vc_ref[0] = _dot(hidden(acc_v, pos_term(posv_ref, w1v_ref)), w2v_ref[...])


def _cmp1_weights(w1):
    w = w1.reshape(2, CMP_STRIDE, HEAD_DIM, CMP_HIDDEN)
    eye = jnp.eye(NSA_KV_HEADS, dtype=w1.dtype)
    big = jnp.einsum("jsde,hg->shdgje", w, eye)
    return big.reshape(CMP_STRIDE * KV_WIDTH, 4 * CMP_HIDDEN).astype(BF16)


def _cmp_call(src_args, src_specs, n_src, n_prefetch, grid_spec_fn, consts, n_seq, name):
    chunks = n_src * PAGE_SIZE // CMP_STRIDE
    lhs = pltpu.VMEM((chunks, CMP_STRIDE * KV_WIDTH), F32)
    const_specs = [pl.BlockSpec(c.shape, (lambda *_, nd=c.ndim: (0,) * nd)) for c in consts]
    seq = lambda i, *_: (i, 0, 0)
    return pl.pallas_call(
        functools.partial(_cmp_body, n_src=n_src, n_prefetch=n_prefetch),
        grid_spec=grid_spec_fn(
            grid=(n_seq,),
            in_specs=src_specs + const_specs,
            out_specs=[pl.BlockSpec((1, chunks, KV_WIDTH), seq)] * 2,
            scratch_shapes=[lhs, lhs],
        ),
        out_shape=[jax.ShapeDtypeStruct((n_seq, chunks, KV_WIDTH), F32)] * 2,
        compiler_params=pltpu.CompilerParams(dimension_semantics=("parallel",), vmem_limit_bytes=VMEM_LIMIT),
        name=name,
    )(*src_args, *consts)


def _cmp_prompt(rows_t, consts):
    b, _, _, t = rows_t.shape
    spec = pl.BlockSpec((None, 2, KV_WIDTH, t), lambda i: (i, 0, 0, 0))
    grid_spec_fn = functools.partial(pltpu.PrefetchScalarGridSpec, num_scalar_prefetch=0)
    return _cmp_call([rows_t], [spec], t // PAGE_SIZE, 0, grid_spec_fn, consts, b, "cmp_prompt")


def _page_specs(kind_block, n_pages):
    def spec(i):
        return pl.BlockSpec((None, 2, KV_WIDTH, PAGE_SIZE), lambda b, pt: (pt[b, i], kind_block, 0, 0))
    return [spec(i) for i in range(n_pages)]


def _cmp_sample(cache_t, page_table, consts):
    bs, n_pages = page_table.shape
    grid_spec_fn = functools.partial(pltpu.PrefetchScalarGridSpec, num_scalar_prefetch=1)
    return _cmp_call([page_table] + [cache_t] * n_pages, _page_specs(0, n_pages), n_pages, 1, grid_spec_fn, consts, bs,
                     "cmp_sample")


def _stack_heads(q, kvh):
    base = kvh * NSA_GROUP * HEAD_DIM
    return jnp.concatenate([q[:, base + g * HEAD_DIM: base + (g + 1) * HEAD_DIM] for g in range(NSA_GROUP)], axis=0)


def _unstack_heads(o, rows):
    return jnp.concatenate([o[g * rows:(g + 1) * rows] for g in range(NSA_GROUP)], axis=1)


def _group_sum(p, rows):
    out = p[0:rows]
    for g in range(1, NSA_GROUP):
        out = out + p[g * rows:(g + 1) * rows]
    return out


def _select_blocks(imp, qpos, n_blocks):
    blk = lax.broadcasted_iota(jnp.int32, imp.shape, 1)
    visible = blk * SEL_BLOCK <= qpos
    forced = jnp.logical_or(blk == 0, blk == (qpos >> int(math.log2(SEL_BLOCK))))
    score = jnp.where(forced, FORCE_SCORE, jnp.where(visible, imp, -jnp.inf))
    beaten = jnp.zeros(imp.shape, F32)
    for i in range(n_blocks):
        ci = score[:, i:i + 1]
        earlier = jnp.where(blk > i, 1.0, 0.0)
        beaten = beaten + jnp.where(ci > score, 1.0, jnp.where(ci == score, earlier, 0.0))
    return jnp.where(beaten < float(SEL_TOPK), jnp.where(score > -jnp.inf, 1.0, 0.0), 0.0)


def _softmax_cols(s):
    m = jnp.max(s, axis=0, keepdims=True)
    m = jnp.where(m == -jnp.inf, 0.0, m)
    e = jnp.exp2(s - m)
    return e / jnp.maximum(jnp.sum(e, axis=0, keepdims=True), 1e-30)


def _select_blocks_t(imp_t, qpos, n_blocks):
    blk = lax.broadcasted_iota(jnp.int32, imp_t.shape, 0)
    visible = blk * SEL_BLOCK <= qpos
    forced = jnp.logical_or(blk == 0, blk == (qpos >> int(math.log2(SEL_BLOCK))))
    score = jnp.where(forced, FORCE_SCORE, jnp.where(visible, imp_t, -jnp.inf))
    beaten = jnp.zeros(imp_t.shape, F32)
    for i in range(n_blocks):
        ci = score[i:i + 1, :]
        earlier = jnp.where(blk > i, 1.0, 0.0)
        beaten = beaten + jnp.where(ci > score, 1.0, jnp.where(ci == score, earlier, 0.0))
    return jnp.where(beaten < float(SEL_TOPK), jnp.where(score > -jnp.inf, 1.0, 0.0), 0.0)


def _combine_branches(gate_logits, gex_ref, branches):
    gates = 1.0 / (1.0 + jnp.exp(-gate_logits))
    out = None
    for c, branch in enumerate(branches):
        term = _dot2(gates, gex_ref[c]) * branch
        out = term if out is None else out + term
    return out


def _nsa_prompt_body(qn_ref, qr_ref, kc_ref, vc_ref, kvb_ref, vt_ref, ovt_ref, ext_ref, misc_ref, gex_ref, o_ref, *,
                     seq_len):
    tq = Q_TILE
    tk = SEL_KEY_TILE
    q0 = pl.program_id(1) * tq
    n_cmp = seq_len // CMP_STRIDE - 1
    n_sel = seq_len // SEL_BLOCK
    cols = NSA_GROUP * tq
    qn_t = qn_ref[0]
    qr_t = qr_ref[0]
    kc = kc_ref[0].astype(BF16)
    vc_t = vc_ref[0].T.astype(BF16)
    qpos1 = q0 + lax.broadcasted_iota(jnp.int32, (1, tq), 1)
    qpos4 = jnp.concatenate([qpos1] * NSA_GROUP, axis=1)
    win_tiles = (WINDOW + tq) // LANES
    wstart = pl.multiple_of(jnp.maximum(q0 - WINDOW, 0), tq)
    n_tiles = (q0 + tq + tk - 1) // tk

    def group_q(q_t, kvh):
        base = kvh * NSA_GROUP * HEAD_DIM
        return jnp.concatenate([q_t[base + g * HEAD_DIM: base + (g + 1) * HEAD_DIM, :] for g in range(NSA_GROUP)],
                               axis=1)

    def per_head(o_t):
        return [o_t[:, g * tq:(g + 1) * tq] for g in range(NSA_GROUP)]

    oc, osel, ow = [], [], []
    kv_heads = range(NSA_KV_HEADS)
    dims = [slice(h * HEAD_DIM, (h + 1) * HEAD_DIM) for h in kv_heads]
    qs_ts = [group_q(qr_t, h) for h in kv_heads]

    s_c = [_dot(kc[:, dims[h]], group_q(qn_t, h)) for h in kv_heads]
    k_w = [kvb_ref[0, pl.ds(wstart, win_tiles * LANES), 2 * KV_WIDTH + h * HEAD_DIM:2 * KV_WIDTH + (h + 1) * HEAD_DIM]
           for h in kv_heads]
    s_w = [_dot(k_w[h], qs_ts[h]) for h in kv_heads]
    n_idx = lax.broadcasted_iota(jnp.int32, s_c[0].shape, 0)
    vis = jnp.logical_and(n_idx * CMP_STRIDE + (CMP_BLOCK - 1) <= qpos4, n_idx < n_cmp)
    p_c = [_softmax_cols(jnp.where(vis, s, -jnp.inf)) for s in s_c]
    rel = qpos4 - (wstart + lax.broadcasted_iota(jnp.int32, s_w[0].shape, 0))
    inside = jnp.logical_and(rel >= 0, rel < WINDOW)
    p_w = [_softmax_cols(jnp.where(inside, s, -jnp.inf)) for s in s_w]
    w0 = wstart // LANES
    for h in kv_heads:
        oc += per_head(_dot(vc_t[dims[h], :], p_c[h].astype(BF16)))
    for h in kv_heads:
        v_wt = jnp.concatenate([vt_ref[0, w0 + j, KV_WIDTH + h * HEAD_DIM:KV_WIDTH + (h + 1) * HEAD_DIM, :]
                                for j in range(win_tiles)], axis=1)
        ow += per_head(_dot(v_wt, p_w[h].astype(BF16)))

    imp_t = []
    for h in kv_heads:
        p_sum = p_c[h][:, 0:tq]
        for g in range(1, NSA_GROUP):
            p_sum = p_sum + p_c[h][:, g * tq:(g + 1) * tq]
        p_hi, p_lo = _split(p_sum)
        imp_t.append((_dot(ovt_ref[...], p_hi) + _dot(ovt_ref[...], p_lo))[0:n_sel, :])
    sel_both = _select_blocks_t(jnp.concatenate(imp_t, axis=1), jnp.concatenate([qpos1] * NSA_KV_HEADS, axis=1), n_sel)
    sel_both = jnp.concatenate([sel_both, jnp.zeros((LANES - n_sel, NSA_KV_HEADS * tq), F32)], axis=0).astype(BF16)
    sel_ts = [sel_both[:, h * tq:(h + 1) * tq] for h in kv_heads]

    ones_rows = jnp.ones((ONES_ROWS, tk), BF16)

    def sel_step(kt, carry):
        k0 = pl.multiple_of(kt * tk, tk)
        causal = (k0 + lax.broadcasted_iota(jnp.int32, (tk, tq), 0)) <= qpos1
        k_blk = [kvb_ref[0, pl.ds(k0, tk), h * HEAD_DIM:(h + 1) * HEAD_DIM] for h in kv_heads]
        v_t = [jnp.concatenate([vt_ref[0, kt * (tk // LANES) + j, h * HEAD_DIM:(h + 1) * HEAD_DIM, :]
                                for j in range(tk // LANES)] , axis=1) for h in kv_heads]
        v_t = [jnp.concatenate([v, ones_rows], axis=0) for v in v_t]
        chosen = [_dot(ext_ref[kt], sel_ts[h]) for h in kv_heads]
        bias = [jnp.where(jnp.logical_and(c > 0.5, causal), 0.0, -jnp.inf) for c in chosen]
        sc = [_dot(k_blk[h], qs_ts[h]) + jnp.concatenate([bias[h]] * NSA_GROUP, axis=1) for h in kv_heads]
        m_new = [jnp.maximum(carry[h][0], jnp.max(sc[h], axis=0, keepdims=True)) for h in kv_heads]
        pe = [jnp.exp2(sc[h] - m_new[h]).astype(BF16) for h in kv_heads]
        return tuple((m_new[h], jnp.exp2(carry[h][0] - m_new[h]) * carry[h][1] + _dot(v_t[h], pe[h])) for h in kv_heads)

    init = (jnp.full((1, cols), NEG, F32), jnp.zeros((HEAD_DIM + ONES_ROWS, cols), F32))
    for _, acc_f in lax.fori_loop(0, n_tiles, sel_step, (init,) * NSA_KV_HEADS):
        osel += per_head(acc_f[0:HEAD_DIM, :] / jnp.maximum(acc_f[HEAD_DIM:HEAD_DIM + 1, :], 1e-30))

    branches = [jnp.concatenate(o, axis=0).T for o in (oc, osel, ow)]
    o_ref[0] = _combine_branches(misc_ref[0], gex_ref, branches)


def _nsa_prompt(qn_t, qr_t, kc, vc, kvb, vt, ovt, ext, misc, gex):
    b, _, t = qn_t.shape
    chunks = kc.shape[1]
    qtile = lambda i, j: (i, 0, j)
    otile = lambda i, j: (i, j, 0)
    seq = lambda i, j: (i, 0, 0)
    return pl.pallas_call(
        functools.partial(_nsa_prompt_body, seq_len=t),
        grid=(b, t // Q_TILE),
        in_specs=[
            pl.BlockSpec((1, NSA_WIDTH, Q_TILE), qtile),
            pl.BlockSpec((1, NSA_WIDTH, Q_TILE), qtile),
            pl.BlockSpec((1, chunks, KV_WIDTH), seq),
            pl.BlockSpec((1, chunks, KV_WIDTH), seq),
            pl.BlockSpec((1, t, 4 * KV_WIDTH), seq),
            pl.BlockSpec((1,) + vt.shape[1:], lambda i, j: (i, 0, 0, 0)),
            pl.BlockSpec(ovt.shape, lambda i, j: (0, 0)),
            pl.BlockSpec(ext.shape, lambda i, j: (0, 0, 0)),
            pl.BlockSpec((1, Q_TILE, MISC_WIDTH), otile),
            pl.BlockSpec(gex.shape, lambda i, j: (0, 0, 0)),
        ],
        out_specs=pl.BlockSpec((1, Q_TILE, NSA_WIDTH), otile),
        out_shape=jax.ShapeDtypeStruct((b, t, NSA_WIDTH), F32),
        compiler_params=pltpu.CompilerParams(dimension_semantics=("parallel", "parallel"),
                                             vmem_limit_bytes=VMEM_LIMIT),
        name="nsa_prompt",
    )(qn_t, qr_t, kc, vc, kvb, vt, ovt, ext, misc, gex)


def _nsa_sample_body(*refs, n_pages, past_len, n_new):
    pages = refs[1:1 + n_pages]
    (qn_ref, qr_ref, kc_ref, vc_ref, ov_ref, ex_ref, kvb_ref, cw_ref, nw_ref, misc_ref, gex_ref,
     o_ref, neww_ref) = refs[1 + n_pages:]
    t = n_new
    rows4 = NSA_GROUP * t
    n_cmp = (past_len + t) // CMP_STRIDE - 1
    n_sel = -(-(past_len + t) // SEL_BLOCK)
    qn = qn_ref[0]
    qr = qr_ref[0]
    kc = kc_ref[0].astype(BF16)
    vc = vc_ref[0].astype(BF16)
    qpos1 = past_len + lax.broadcasted_iota(jnp.int32, (t, 1), 0)
    qpos4 = jnp.concatenate([qpos1] * NSA_GROUP, axis=0)
    qs = [_stack_heads(qr, kvh) for kvh in range(NSA_KV_HEADS)]

    oc, imp = [], []
    for kvh in range(NSA_KV_HEADS):
        lo, hi = kvh * HEAD_DIM, (kvh + 1) * HEAD_DIM
        s = _dot_nt(_stack_heads(qn, kvh), kc[:, lo:hi])
        n_idx = lax.broadcasted_iota(jnp.int32, s.shape, 1)
        vis = jnp.logical_and(n_idx * CMP_STRIDE + (CMP_BLOCK - 1) <= qpos4, n_idx < n_cmp)
        p = _softmax_rows(jnp.where(vis, s, -jnp.inf))
        oc.append(_unstack_heads(_dot(p.astype(BF16), vc[:, lo:hi]), t))
        imp.append(_dot2(_group_sum(p, t), ov_ref[...]))
    o_cmp = jnp.concatenate(oc, axis=1)
    sel = _select_blocks(jnp.concatenate(imp, axis=0), jnp.concatenate([qpos1] * NSA_KV_HEADS, axis=0), n_sel)

    zero = jnp.zeros((rows4, HEAD_DIM), BF16)
    q_bd = jnp.concatenate([jnp.concatenate([qs[0], zero], axis=1), jnp.concatenate([zero, qs[1]], axis=1)], axis=0)
    sel_rows = jnp.concatenate(
        [jnp.concatenate([sel[kvh * t:(kvh + 1) * t, 0:ex_ref.shape[0]]] * NSA_GROUP, axis=0)
         for kvh in range(NSA_KV_HEADS)], axis=0).astype(BF16)
    k_t = jnp.concatenate([p[0] for p in pages], axis=1).astype(BF16)
    v_t = jnp.concatenate([p[1] for p in pages], axis=1).astype(BF16)
    chosen = _dot(sel_rows, ex_ref[...])
    sc = jnp.where(chosen > 0.5, _dot(q_bd, k_t), -jnp.inf)
    pad = jnp.zeros((LANES - t, KV_WIDTH), F32)
    k_new = jnp.concatenate([kvb_ref[0, :, 0:KV_WIDTH].astype(F32), pad], axis=0).astype(BF16)
    v_new = jnp.concatenate([kvb_ref[0, :, KV_WIDTH:2 * KV_WIDTH].astype(F32), pad], axis=0).astype(BF16)
    sn = _dot_nt(q_bd, k_new)
    key_i = lax.broadcasted_iota(jnp.int32, sn.shape, 1)
    q_t = lax.broadcasted_iota(jnp.int32, sn.shape, 0) & (t - 1)
    sn = jnp.where(key_i <= q_t, sn, -jnp.inf)
    m = jnp.maximum(jnp.max(sc, axis=-1, keepdims=True), jnp.max(sn, axis=-1, keepdims=True))
    e_past = jnp.exp2(sc - m)
    e_new = jnp.exp2(sn - m)
    denom = jnp.sum(e_past, axis=-1, keepdims=True) + jnp.sum(e_new, axis=-1, keepdims=True)
    o_both = (_dot_nt(e_past.astype(BF16), v_t) + _dot(e_new.astype(BF16), v_new)) / denom
    o_sel = jnp.concatenate([_unstack_heads(o_both[0:rows4, 0:HEAD_DIM], t),
                             _unstack_heads(o_both[rows4:2 * rows4, HEAD_DIM:KV_WIDTH], t)], axis=1)

    wb = cw_ref.shape[2]
    cw = cw_ref[0]
    nw_t = jnp.concatenate([nw_ref[0], jnp.zeros((LANES - t, 2 * KV_WIDTH), F32)], axis=0).T
    shifted = pltpu.roll(cw, wb - t, 1)
    lane = lax.broadcasted_iota(jnp.int32, (2 * KV_WIDTH, LANES), 1)
    neww_ref[0, :, 0:wb - LANES] = shifted[:, 0:wb - LANES]
    neww_ref[0, :, wb - LANES:wb] = jnp.where(lane >= LANES - t, pltpu.roll(nw_t, LANES - t, 1), shifted[:, wb - LANES:wb])
    keys_t = jnp.concatenate([cw, nw_t], axis=1).astype(BF16)
    out = []
    for kvh in range(NSA_KV_HEADS):
        lo, hi = kvh * HEAD_DIM, (kvh + 1) * HEAD_DIM
        sw = _dot(qs[kvh], keys_t[lo:hi, :])
        key_i = lax.broadcasted_iota(jnp.int32, sw.shape, 1)
        q_t = lax.broadcasted_iota(jnp.int32, sw.shape, 0) & (t - 1)
        rel = wb + q_t - key_i
        inside = jnp.logical_and(rel >= 0, rel < WINDOW)
        pw = _softmax_rows(jnp.where(inside, sw, -jnp.inf))
        out.append(_unstack_heads(_dot_nt(pw.astype(BF16), keys_t[KV_WIDTH + lo:KV_WIDTH + hi, :]), t))
    o_ref[0] = _combine_branches(misc_ref[0], gex_ref, [o_cmp, o_sel, jnp.concatenate(out, axis=1)])


def _nsa_sample(cache_t, page_table, qn, qr, kc, vc, ov, ex, kvb, cache_win_t, new_win, misc, gex, past_len):
    bs, n_pages = page_table.shape
    t = qr.shape[1]
    chunks = kc.shape[1]
    wb = cache_win_t.shape[2]
    seq = lambda b, pt: (b, 0, 0)
    c2 = lambda b, pt: (0, 0)
    grid_spec = pltpu.PrefetchScalarGridSpec(
        num_scalar_prefetch=1,
        grid=(bs,),
        in_specs=_page_specs(1, n_pages) + [
            pl.BlockSpec((1, t, NSA_WIDTH), seq), pl.BlockSpec((1, t, NSA_WIDTH), seq),
            pl.BlockSpec((1, chunks, KV_WIDTH), seq), pl.BlockSpec((1, chunks, KV_WIDTH), seq),
            pl.BlockSpec(ov.shape, c2), pl.BlockSpec(ex.shape, c2),
            pl.BlockSpec((1, t, 4 * KV_WIDTH), seq),
            pl.BlockSpec((1, 2 * KV_WIDTH, wb), seq), pl.BlockSpec((1, t, 2 * KV_WIDTH), seq),
            pl.BlockSpec((1, t, MISC_WIDTH), seq), pl.BlockSpec(gex.shape, lambda b, pt: (0, 0, 0)),
        ],
        out_specs=[pl.BlockSpec((1, t, NSA_WIDTH), seq), pl.BlockSpec((1, 2 * KV_WIDTH, wb), seq)],
    )
    return pl.pallas_call(
        functools.partial(_nsa_sample_body, n_pages=n_pages, past_len=past_len, n_new=t),
        grid_spec=grid_spec,
        out_shape=[jax.ShapeDtypeStruct((bs, t, NSA_WIDTH), F32), jax.ShapeDtypeStruct((bs, 2 * KV_WIDTH, wb), F32)],
        compiler_params=pltpu.CompilerParams(dimension_semantics=("parallel",), vmem_limit_bytes=VMEM_LIMIT),
        name="nsa_sample",
    )(page_table, *([cache_t] * n_pages), qn, qr, kc, vc, ov, ex, kvb, cache_win_t, new_win, misc, gex)


def _gla_constants(n_seq):
    tile = GLA_TILE
    seq_len = tile // n_seq
    t = np.arange(tile)[:, None]
    r = np.arange(tile)[None, :]
    same = (t // seq_len) == (r // seq_len)
    ops = [same & (r <= t), same & (r > t)]
    masks = []
    m = 1
    while 2 * m <= seq_len:
        upper = (t % (2 * m)) >= m
        mid = (t // (2 * m)) * (2 * m) + m - 1
        ops.append(upper & (r > mid) & (r <= t))
        ops.append((~upper) & (r > t) & (r <= mid))
        same_blk = (t // (2 * m)) == (r // (2 * m))
        masks.append(same_blk & upper & ((r % (2 * m)) < m))
        m *= 2
    ops = np.concatenate([o.astype(np.float32) for o in ops], axis=0)
    masks = np.stack([k.astype(np.float32) for k in masks], axis=0)
    lane_seq = (np.arange(n_seq * GLA_DK)[None, :] // GLA_DK) == (np.arange(tile)[:, None] // seq_len)
    rep = (np.arange(n_seq * GLA_DK)[None, :] % GLA_DK) == np.arange(GLA_DK)[:, None]
    return (jnp.asarray(ops, BF16), jnp.asarray(masks, F32), jnp.asarray(lane_seq.astype(np.float32), F32),
            jnp.asarray(rep.astype(np.float32), BF16))


def _gla_body(gq_ref, gk_ref, gv_ref, gg_ref, misc_ref, s0_ref, wa_hi_ref, wa_lo_ref, ba_ref, ops_ref, masks_ref,
              lane_seq_ref, rep_ref, gnorm_ref, o_ref, sfin_ref, st_sc, *, n_levels):
    step = pl.program_id(1)

    @pl.when(step == 0)
    def _():
        st_sc[...] = s0_ref[...]

    for g in range(gq_ref.shape[0]):
        _gla_tile(g, gq_ref, gk_ref, gv_ref, gg_ref, misc_ref, wa_hi_ref, wa_lo_ref, ba_ref, ops_ref, masks_ref,
                  lane_seq_ref, rep_ref, gnorm_ref, o_ref, st_sc, n_levels)

    @pl.when(step == pl.num_programs(1) - 1)
    def _():
        sfin_ref[...] = st_sc[...]


def _gla_tile(g, gq_ref, gk_ref, gv_ref, gg_ref, misc_ref, wa_hi_ref, wa_lo_ref, ba_ref, ops_ref, masks_ref,
              lane_seq_ref, rep_ref, gnorm_ref, o_ref, st_sc, n_levels):
    tile = GLA_TILE
    m_hi, m_lo = _split(misc_ref[g])
    x = _dot(m_hi, wa_hi_ref[...]) + _dot(m_hi, wa_lo_ref[...]) + _dot(m_lo, wa_hi_ref[...]) + ba_ref[...]
    la = (jnp.minimum(x, 0.0) - jnp.log(1.0 + jnp.exp(-jnp.abs(x)))) * (1.0 / GLA_GATE_TAU)
    la_hi, la_lo = _split(la)
    cums = _dot(ops_ref[...], la_hi) + _dot(ops_ref[...], la_lo)
    lane_seq = lane_seq_ref[...]
    rep = rep_ref[...]
    multi = lane_seq.shape[1] > GLA_DK

    def spread(a):
        a = a.astype(BF16)
        return (_dot(a, rep) * lane_seq).astype(BF16) if multi else a

    for h in range(GLA_HEADS):
        dk = slice(h * GLA_DK, (h + 1) * GLA_DK)
        dv = slice(h * GLA_DV, (h + 1) * GLA_DV)
        q = gq_ref[g, :, dk] * (GLA_DK ** -0.5)
        k = gk_ref[g, :, dk]
        v = gv_ref[g, :, dv]
        v_b = v.astype(BF16)
        st = st_sc[g, h]
        since_start = cums[0:tile, dk]
        until_end = cums[tile:2 * tile, dk]
        o = _dot_nt(spread(q * jnp.exp(since_start)), st.astype(BF16))
        a = jnp.zeros((tile, tile), F32)
        for lv in range(n_levels):
            dq = cums[(2 + 2 * lv) * tile:(3 + 2 * lv) * tile, dk]
            dkk = cums[(3 + 2 * lv) * tile:(4 + 2 * lv) * tile, dk]
            a = a + _dot_nt((q * jnp.exp(dq)).astype(BF16), (k * jnp.exp(dkk)).astype(BF16)) * masks_ref[lv]
        o = o + _dot(a.astype(BF16), v_b) + jnp.sum(q * k, axis=-1, keepdims=True) * v
        la_h = la[:, dk]
        if multi:
            total = jnp.sum(_dot2(la_h, rep) * lane_seq, axis=0, keepdims=True)
        else:
            total = jnp.sum(la_h, axis=0, keepdims=True)
        st_new = st * jnp.exp(total) + _dot(v.T.astype(BF16), spread(k * jnp.exp(until_end)))
        st_sc[g, h] = st_new
        ms = jnp.mean(o * o, axis=-1, keepdims=True)
        o_ref[g, :, dv] = o * lax.rsqrt(ms + EPS) * gnorm_ref[...] * _silu(gg_ref[g, :, dv])


def _gla(gq, gk, gv, gg, misc, s0_t, wa_hi, wa_lo, ba, gnorm, n_seq):
    groups, rows, _ = gq.shape
    steps = rows // GLA_TILE
    par = GLA_GROUPS_PER_STEP
    assert groups % par == 0
    ops, masks, lane_seq, rep = _gla_constants(n_seq)
    n_levels = masks.shape[0]
    tile = lambda g, s: (g, s, 0)
    c2 = lambda g, s: (0, 0)
    c3 = lambda g, s: (0, 0, 0)
    st_block = (par, GLA_HEADS, GLA_DV, n_seq * GLA_DK)
    st_map = lambda g, s: (g, 0, 0, 0)
    return pl.pallas_call(
        functools.partial(_gla_body, n_levels=n_levels),
        grid=(groups // par, steps),
        in_specs=[
            pl.BlockSpec((par, GLA_TILE, GLA_QK_WIDTH), tile), pl.BlockSpec((par, GLA_TILE, GLA_QK_WIDTH), tile),
            pl.BlockSpec((par, GLA_TILE, GLA_WIDTH), tile), pl.BlockSpec((par, GLA_TILE, GLA_WIDTH), tile),
            pl.BlockSpec((par, GLA_TILE, MISC_WIDTH), tile),
            pl.BlockSpec(st_block, st_map),
            pl.BlockSpec(wa_hi.shape, c2), pl.BlockSpec(wa_lo.shape, c2), pl.BlockSpec(ba.shape, c2),
            pl.BlockSpec(ops.shape, c2), pl.BlockSpec(masks.shape, c3),
            pl.BlockSpec(lane_seq.shape, c2), pl.BlockSpec(rep.shape, c2), pl.BlockSpec(gnorm.shape, c2),
        ],
        out_specs=[pl.BlockSpec((par, GLA_TILE, GLA_WIDTH), tile), pl.BlockSpec(st_block, st_map)],
        out_shape=[jax.ShapeDtypeStruct((groups, rows, GLA_WIDTH), F32),
                   jax.ShapeDtypeStruct((groups,) + st_block[1:], F32)],
        scratch_shapes=[pltpu.VMEM(st_block, F32)],
        compiler_params=pltpu.CompilerParams(dimension_semantics=("parallel", "arbitrary"),
                                             vmem_limit_bytes=VMEM_LIMIT),
        name="gla",
    )(gq, gk, gv, gg, misc, s0_t, wa_hi, wa_lo, ba, ops, masks, lane_seq, rep, gnorm)


def _out_body(x_ref, on_ref, og_ref, wout_ref, gffn_ref, wa_ref, wb_ref, wdown_ref, y_ref, hn_sc):
    c = pl.program_id(1)

    @pl.when(c == 0)
    def _():
        x1 = (x_ref[...] + _dot(on_ref[...].astype(BF16), wout_ref[0:NSA_WIDTH, :])
              + _dot(og_ref[...].astype(BF16), wout_ref[NSA_WIDTH:NSA_WIDTH + GLA_WIDTH, :]))
        ms = jnp.mean(x1 * x1, axis=-1, keepdims=True)
        hn_sc[...] = (x1 * lax.rsqrt(ms + EPS) * gffn_ref[...]).astype(BF16)
        y_ref[...] = x1

    hn = hn_sc[...]
    act = _silu(_dot(hn, wa_ref[...])) * _dot(hn, wb_ref[...])
    y_ref[...] += _dot(act.astype(BF16), wdown_ref[...])


def _out(x2d, o_nsa, og, wout, gffn, wup, wdown):
    n = x2d.shape[0]
    tm = min(OUT_ROW_TILE, n)
    assert n % tm == 0
    n_chunks = FFN_HIDDEN // FFN_CHUNK
    row = lambda i, c: (i, 0)
    c2 = lambda i, c: (0, 0)
    return pl.pallas_call(
        _out_body,
        grid=(n // tm, n_chunks),
        in_specs=[
            pl.BlockSpec((tm, D_MODEL), row),
            pl.BlockSpec((tm, NSA_WIDTH), row), pl.BlockSpec((tm, GLA_WIDTH), row),
            pl.BlockSpec(wout.shape, c2),
            pl.BlockSpec(gffn.shape, c2),
            pl.BlockSpec((D_MODEL, FFN_CHUNK), lambda i, c: (0, c)),
            pl.BlockSpec((D_MODEL, FFN_CHUNK), lambda i, c: (0, n_chunks + c)),
            pl.BlockSpec((FFN_CHUNK, D_MODEL), lambda i, c: (c, 0)),
        ],
        out_specs=pl.BlockSpec((tm, D_MODEL), row),
        out_shape=jax.ShapeDtypeStruct((n, D_MODEL), F32),
        scratch_shapes=[pltpu.VMEM((tm, D_MODEL), BF16)],
        compiler_params=pltpu.CompilerParams(dimension_semantics=("parallel", "arbitrary"),
                                             vmem_limit_bytes=VMEM_LIMIT),
        name="out",
    )(x2d, o_nsa, og, wout, gffn, wup, wup, wdown)


def _rope_tables(pos):
    half = HEAD_DIM // 2
    inv = ROPE_THETA ** (-jnp.arange(half, dtype=F32) / half)
    ang = pos.astype(F32)[:, None] * inv[None, :]
    cos, sin = jnp.cos(ang), jnp.sin(ang)
    reps = LANES // HEAD_DIM
    return jnp.tile(jnp.concatenate([cos, cos], axis=1), (1, reps)), jnp.tile(jnp.concatenate([-sin, sin], axis=1), (1, reps))


def _overlap_matrix(n_chunks, n_cols):
    start = np.arange(n_chunks)[:, None] * CMP_STRIDE
    j0 = np.arange(n_cols)[None, :] * SEL_BLOCK
    return jnp.asarray(((start < j0 + SEL_BLOCK) & (start + CMP_BLOCK > j0)).astype(np.float32), BF16)


def _expand_matrix(n_tiles, keys):
    key = np.arange(n_tiles)[:, None, None] * keys + np.arange(keys)[None, None, :]
    blk = np.arange(LANES)[None, :, None]
    return jnp.asarray((key // SEL_BLOCK == blk).astype(np.float32), BF16)


def _chunk_row_permutation():
    out = np.arange(2 * PAGE_SIZE)
    page, within = out // PAGE_SIZE, out % PAGE_SIZE
    chunks = PAGE_SIZE // CMP_STRIDE
    src = page * PAGE_SIZE + (within % chunks) * CMP_STRIDE + within // chunks
    return jnp.asarray((src[:, None] == np.arange(2 * PAGE_SIZE)[None, :]).astype(np.float32), BF16)


def _block_diag_ones(width):
    i = np.arange(width)
    return jnp.asarray((i[:, None] // HEAD_DIM == i[None, :] // HEAD_DIM).astype(np.float32), BF16)


def _gate_expand():
    lane = np.arange(MISC_WIDTH)[None, :, None]
    out = np.arange(NSA_WIDTH)[None, None, :]
    c = np.arange(3)[:, None, None]
    return jnp.asarray(((lane == (out // HEAD_DIM) * 3 + c) & (lane < GATE_COLS)).astype(np.float32), BF16)


def _pad_cols(w, width):
    return jnp.pad(w, ((0, 0), (0, width - w.shape[1])))


def kernel(x_prompt, x_sample, cache_nsa_kv, cache_win_kv, state_gla, page_table, w_norm_mix, w_in, nsa_q_gain,
           nsa_k_gain, cmp_pos_k, cmp_w1_k, cmp_w2_k, cmp_pos_v, cmp_w1_v, cmp_w2_v, gla_w_a2, gla_b_a,
           gla_norm_gain, w_out, w_norm_ffn, w_ffn_up, w_ffn_down):
    b_p, t_p, _ = x_prompt.shape
    b_s, t_s, _ = x_sample.shape
    depth = w_in.shape[0]
    assert depth == 1, "single layer"
    n_pages = page_table.shape[1]
    past_len = n_pages * PAGE_SIZE
    n_pool = cache_nsa_kv.shape[1]
    wb = cache_win_kv.shape[2]
    assert t_p % SEL_KEY_TILE == 0 and t_p >= WINDOW + Q_TILE and (b_p * t_p) % ROW_TILE == 0
    assert (b_s * t_s) % ROW_TILE == 0 and GLA_TILE % t_s == 0 and b_s % (GLA_TILE // t_s) == 0
    assert wb == WINDOW and ROW_TILE % t_s == 0 and t_p % PAGE_SIZE == 0 and past_len // SEL_BLOCK <= LANES
    l = 0

    q_w, kv_w, gate_w, gq_w, gk_w, gv_w, alow_w, gg_w = jnp.split(
        w_in[l], np.cumsum([NSA_WIDTH, 6 * KV_WIDTH, GATE_COLS, GLA_QK_WIDTH, GLA_QK_WIDTH, GLA_WIDTH,
                            GLA_GATE_RANK])[:].tolist(), axis=1)
    misc_w = _pad_cols(jnp.concatenate([gate_w, alow_w], axis=1), MISC_WIDTH)
    w_perm = jnp.concatenate([q_w, kv_w, gq_w, gk_w, gv_w, gg_w, misc_w], axis=1).astype(BF16)
    gmix = w_norm_mix[l][None, :]
    qg = jnp.tile(nsa_q_gain[l], NSA_HEADS)[None, :]
    kg = [jnp.tile(nsa_k_gain[l, i], NSA_KV_HEADS)[None, :] for i in range(3)]
    ones512 = _block_diag_ones(NSA_WIDTH)
    ones128 = _block_diag_ones(LANES)
    wa = jnp.zeros((MISC_WIDTH, GLA_QK_WIDTH), F32).at[GATE_COLS:GATE_COLS + GLA_GATE_RANK].set(gla_w_a2[l])
    wa_hi, wa_lo = _split(wa)
    ba = gla_b_a[l][None, :]
    gnorm = gla_norm_gain[l][None, :]
    wk1 = _cmp1_weights(cmp_w1_k[l])
    wv1 = _cmp1_weights(cmp_w1_v[l])
    eye2 = jnp.eye(NSA_KV_HEADS, dtype=F32)
    w2k_bd = jnp.einsum("ed,hg->hegd", cmp_w2_k[l], eye2).reshape(2 * CMP_HIDDEN, KV_WIDTH).astype(BF16)
    w2v_bd = jnp.einsum("ed,hg->hegd", cmp_w2_v[l], eye2).reshape(2 * CMP_HIDDEN, KV_WIDTH).astype(BF16)
    flat = CMP_BLOCK * HEAD_DIM
    posk, posv = cmp_pos_k[l].reshape(1, flat), cmp_pos_v[l].reshape(1, flat)
    w1k_flat, w1v_flat = cmp_w1_k[l].reshape(flat, CMP_HIDDEN), cmp_w1_v[l].reshape(flat, CMP_HIDDEN)
    cmp_consts = (_chunk_row_permutation(), wk1, wv1, posk, w1k_flat, posv, w1v_flat, w2k_bd, w2v_bd, kg[0], ones128)
    gex = _gate_expand()
    wout = w_out[l].astype(BF16)
    gffn = w_norm_ffn[l][None, :]
    wup = w_ffn_up[l].astype(BF16)
    wdown = w_ffn_down[l].astype(BF16)

    cos_p, sin_p = _rope_tables(jnp.arange(t_p))
    xp2 = x_prompt.reshape(b_p * t_p, D_MODEL)
    (qn_t, qr_t, rows_t, win_t, kvb, vt, gq, gk, gv, gg, misc) = _proj(
        xp2, gmix, w_perm, cos_p, sin_p, qg, kg[1], kg[2], ones512, t_p // ROW_TILE, seq_len=t_p)
    kc_p, vc_p = _cmp_prompt(rows_t.reshape(b_p, 4, KV_WIDTH, t_p), cmp_consts)
    ovt_p = _overlap_matrix(t_p // CMP_STRIDE, LANES).T
    ext_p = jnp.swapaxes(_expand_matrix(t_p // SEL_KEY_TILE, SEL_KEY_TILE), 1, 2)
    o_nsa = _nsa_prompt(qn_t, qr_t, kc_p, vc_p, kvb.reshape(b_p, t_p, -1),
                        vt.reshape(b_p, t_p // LANES, 2 * KV_WIDTH, LANES), ovt_p, ext_p,
                        misc.reshape(b_p, t_p, MISC_WIDTH), gex)
    s0_p = jnp.zeros((b_p, GLA_HEADS, GLA_DV, GLA_DK), F32)
    r3 = lambda a, g=b_p: a.reshape(g, -1, a.shape[-1])
    og, sfin_p = _gla(r3(gq), r3(gk), r3(gv), r3(gg), r3(misc), s0_p, wa_hi, wa_lo, ba, gnorm, 1)
    y_p = _out(xp2, o_nsa.reshape(b_p * t_p, -1), og.reshape(b_p * t_p, -1), wout, gffn, wup, wdown)
    out_rows_p = rows_t.reshape(b_p, 4, NSA_KV_HEADS, HEAD_DIM, t_p).transpose(0, 4, 1, 2, 3)[None]
    wlen = min(WINDOW, t_p)
    out_win_p = win_t[:, :, t_p - wlen:].reshape(b_p, 2, NSA_KV_HEADS, HEAD_DIM, wlen).transpose(0, 4, 1, 2, 3)[None]
    out_gla_p = jnp.swapaxes(sfin_p, 2, 3)[None].astype(state_gla.dtype)

    pos_s = past_len + jnp.arange(t_s)
    cos_s, sin_s = _rope_tables(pos_s)
    cos_s = jnp.tile(cos_s, (ROW_TILE // t_s, 1))
    sin_s = jnp.tile(sin_s, (ROW_TILE // t_s, 1))
    xs2 = x_sample.reshape(b_s * t_s, D_MODEL)
    (qn_s, qr_s, rows_s, win_s, kvb_s, _, gq_s, gk_s, gv_s, gg_s, misc_s) = _proj(
        xs2, gmix, w_perm, cos_s, sin_s, qg, kg[1], kg[2], ones512, 1)
    cache_t = cache_nsa_kv[l].transpose(0, 2, 3, 4, 1).reshape(n_pool, 4, KV_WIDTH, PAGE_SIZE)
    kc_s, vc_s = _cmp_sample(cache_t, page_table, cmp_consts)
    n_sel_s = -(-(past_len + t_s) // SEL_BLOCK)
    sel_cols = -(-n_sel_s // LANES) * LANES
    ov_s = _overlap_matrix(past_len // CMP_STRIDE, sel_cols)
    ex_s = _expand_matrix(1, past_len)[0]
    seq3 = lambda a: a.reshape(b_s, t_s, a.shape[-1])
    cache_win_t = cache_win_kv[l].transpose(0, 2, 3, 4, 1).reshape(b_s, 2 * KV_WIDTH, wb)
    o_nsa_s, new_win_t = _nsa_sample(cache_t, page_table, seq3(qn_s), seq3(qr_s), kc_s, vc_s, ov_s, ex_s, seq3(kvb_s),
                                     cache_win_t, seq3(win_s), seq3(misc_s), gex, past_len)
    n_seq = GLA_TILE // t_s
    groups = b_s // n_seq
    s0_s = state_gla[l].astype(F32).reshape(groups, n_seq, GLA_HEADS, GLA_DK, GLA_DV)
    s0_s = s0_s.transpose(0, 2, 4, 1, 3).reshape(groups, GLA_HEADS, GLA_DV, n_seq * GLA_DK)
    rg = lambda a: a.reshape(groups, GLA_TILE, a.shape[-1])
    og_s, sfin_s = _gla(rg(gq_s), rg(gk_s), rg(gv_s), rg(gg_s), rg(misc_s), s0_s, wa_hi, wa_lo, ba, gnorm, n_seq)
    y_s = _out(xs2, o_nsa_s.reshape(b_s * t_s, -1), og_s.reshape(b_s * t_s, -1), wout, gffn, wup, wdown)
    out_rows_s = rows_s.reshape(1, b_s, t_s, 4, NSA_KV_HEADS, HEAD_DIM)
    out_win_s = new_win_t.reshape(b_s, 2, NSA_KV_HEADS, HEAD_DIM, wb).transpose(0, 4, 1, 2, 3)[None]
    out_gla_s = sfin_s.reshape(groups, GLA_HEADS, GLA_DV, n_seq, GLA_DK).transpose(0, 3, 1, 4, 2)
    out_gla_s = out_gla_s.reshape(1, b_s, GLA_HEADS, GLA_DK, GLA_DV).astype(state_gla.dtype)

    return (y_p.reshape(b_p, t_p, D_MODEL), y_s.reshape(b_s, t_s, D_MODEL), out_rows_p, out_win_p, out_gla_p,
            out_rows_s, out_win_s, out_gla_s)
```

```python
import functools
import math

import numpy as np
import jax
import jax.numpy as jnp
from jax import lax
from jax.experimental import pallas as pl
from jax.experimental.pallas import tpu as pltpu

F32 = jnp.float32
BF16 = jnp.bfloat16

D_MODEL = 1024
PAGE_SIZE = 128
NSA_HEADS = 8
NSA_KV_HEADS = 2
NSA_GROUP = NSA_HEADS // NSA_KV_HEADS
HEAD_DIM = 64
NSA_WIDTH = NSA_HEADS * HEAD_DIM
KV_WIDTH = NSA_KV_HEADS * HEAD_DIM
CMP_BLOCK = 32
CMP_STRIDE = 16
CMP_HIDDEN = 2 * HEAD_DIM
SEL_BLOCK = 64
SEL_TOPK = 16
WINDOW = 512
FORCE_SCORE = 1.0e4
GLA_HEADS = 4
GLA_DK = 64
GLA_DV = 128
GLA_WIDTH = GLA_HEADS * GLA_DV
GLA_QK_WIDTH = GLA_HEADS * GLA_DK
GLA_GATE_RANK = 16
GLA_GATE_TAU = 16.0
FFN_HIDDEN = -(-8 * D_MODEL // (3 * 256)) * 256
ROPE_THETA = 10000.0
EPS = 1e-6
NEG = -1.0e30

LANES = 128
MISC_WIDTH = LANES
GATE_COLS = 3 * NSA_HEADS
PROJ_WIDTH = NSA_WIDTH + 6 * KV_WIDTH + 2 * GLA_QK_WIDTH + 2 * GLA_WIDTH + MISC_WIDTH
ROW_TILE = 512
OUT_ROW_TILE = 512
Q_TILE = 128
SEL_KEY_TILE = 512
ONES_ROWS = 16
GLA_TILE = 128
GLA_GROUPS_PER_STEP = 2
FFN_CHUNK = FFN_HIDDEN // 2
VMEM_LIMIT = 56 * 1024 * 1024


def _dot(a, b):
    return jnp.dot(a, b, preferred_element_type=F32)


def _dot_nt(a, b):
    return lax.dot_general(a, b, (((1,), (1,)), ((), ())), preferred_element_type=F32)


def _split(x):
    hi = x.astype(BF16)
    lo = (x - hi.astype(F32)).astype(BF16)
    return hi, lo


def _dot2(x, w_bf16):
    hi, lo = _split(x)
    return _dot(hi, w_bf16) + _dot(lo, w_bf16)


def _silu(x):
    return x * (1.0 / (1.0 + jnp.exp(-x)))


def _head_norm(v, ones_bd, gain):
    ss = _dot((v * v).astype(BF16), ones_bd)
    return v * lax.rsqrt(ss * (1.0 / HEAD_DIM) + EPS) * gain


def _rope(v, cos, sin_signed):
    width = v.shape[-1]
    reps = width // LANES
    if reps > 1:
        cos = jnp.concatenate([cos] * reps, axis=1)
        sin_signed = jnp.concatenate([sin_signed] * reps, axis=1)
    lane = lax.broadcasted_iota(jnp.int32, v.shape, 1)
    first_half = (lane & (HEAD_DIM - 1)) < (HEAD_DIM // 2)
    rot = jnp.where(first_half, pltpu.roll(v, width - HEAD_DIM // 2, 1), pltpu.roll(v, HEAD_DIM // 2, 1))
    return v * cos + rot * sin_signed


def _softmax_rows(s):
    m = jnp.max(s, axis=-1, keepdims=True)
    m = jnp.where(m == -jnp.inf, 0.0, m)
    e = jnp.exp2(s - m)
    return e / jnp.maximum(jnp.sum(e, axis=-1, keepdims=True), 1e-30)


def _proj_body(x_ref, gmix_ref, w_ref, cos_ref, sin_ref, qg_ref, kg1_ref, kg2_ref, ones_ref,
               qn_ref, qr_ref, rows_ref, win_ref, kvb_ref, vt_ref, gq_ref, gk_ref, gv_ref, gg_ref, misc_ref, *,
               rows_minor):
    x = x_ref[...]
    ms = jnp.mean(x * x, axis=-1, keepdims=True)
    h = (x * lax.rsqrt(ms + EPS) * gmix_ref[...]).astype(BF16)
    cos = cos_ref[...]
    sin = sin_ref[...]
    ones512 = ones_ref[...]
    ones128 = ones512[:LANES, :LANES]
    scale = HEAD_DIM ** -0.5 * math.log2(math.e)

    off = 0
    q = _dot(h, w_ref[:, off:off + NSA_WIDTH])
    off += NSA_WIDTH
    qn = _head_norm(q, ones512, qg_ref[...])
    qr = _rope(qn, cos, sin)
    if rows_minor:
        qn_ref[0] = (qn * scale).T.astype(BF16)
        qr_ref[0] = (qr * scale).T.astype(BF16)
    else:
        qn_ref[...] = (qn * scale).astype(BF16)
        qr_ref[...] = (qr * scale).astype(BF16)

    kv = _dot(h, w_ref[:, off:off + 6 * KV_WIDTH])
    off += 6 * KV_WIDTH
    k_slc = _rope(_head_norm(kv[:, 2 * KV_WIDTH:3 * KV_WIDTH], ones128, kg1_ref[...]), cos, sin)
    k_win = _rope(_head_norm(kv[:, 4 * KV_WIDTH:5 * KV_WIDTH], ones128, kg2_ref[...]), cos, sin)
    v_slc = kv[:, 3 * KV_WIDTH:4 * KV_WIDTH]
    v_win = kv[:, 5 * KV_WIDTH:6 * KV_WIDTH]
    v_slc_t = v_slc.T
    v_win_t = v_win.T
    for j in range(v_slc.shape[0] // LANES):
        vt_ref[j, 0:KV_WIDTH, :] = v_slc_t[:, j * LANES:(j + 1) * LANES].astype(BF16)
        vt_ref[j, KV_WIDTH:2 * KV_WIDTH, :] = v_win_t[:, j * LANES:(j + 1) * LANES].astype(BF16)
    if rows_minor:
        rows_ref[0, 0:2 * KV_WIDTH, :] = kv[:, 0:2 * KV_WIDTH].T
        rows_ref[0, 2 * KV_WIDTH:3 * KV_WIDTH, :] = k_slc.T
        rows_ref[0, 3 * KV_WIDTH:4 * KV_WIDTH, :] = v_slc_t
        win_ref[0, 0:KV_WIDTH, :] = k_win.T
        win_ref[0, KV_WIDTH:2 * KV_WIDTH, :] = v_win_t
    else:
        rows_ref[:, 0:2 * KV_WIDTH] = kv[:, 0:2 * KV_WIDTH]
        rows_ref[:, 2 * KV_WIDTH:3 * KV_WIDTH] = k_slc
        rows_ref[:, 3 * KV_WIDTH:4 * KV_WIDTH] = v_slc
        win_ref[:, 0:KV_WIDTH] = k_win
        win_ref[:, KV_WIDTH:2 * KV_WIDTH] = v_win
    kvb_ref[:, 0:KV_WIDTH] = k_slc.astype(BF16)
    kvb_ref[:, KV_WIDTH:2 * KV_WIDTH] = v_slc.astype(BF16)
    kvb_ref[:, 2 * KV_WIDTH:3 * KV_WIDTH] = k_win.astype(BF16)
    kvb_ref[:, 3 * KV_WIDTH:4 * KV_WIDTH] = v_win.astype(BF16)

    gq_ref[...] = _dot(h, w_ref[:, off:off + GLA_QK_WIDTH])
    off += GLA_QK_WIDTH
    gk_ref[...] = _dot(h, w_ref[:, off:off + GLA_QK_WIDTH])
    off += GLA_QK_WIDTH
    gv_ref[...] = _dot(h, w_ref[:, off:off + GLA_WIDTH])
    off += GLA_WIDTH
    gg_ref[...] = _dot(h, w_ref[:, off:off + GLA_WIDTH])
    off += GLA_WIDTH
    misc_ref[...] = _dot(h, w_ref[:, off:off + MISC_WIDTH])


def _proj(x2d, gmix, w_perm, cos_t, sin_t, qg, kg1, kg2, ones512, table_blocks, seq_len=None):
    n = x2d.shape[0]
    tm = ROW_TILE
    row = lambda i: (i, 0)
    const = lambda i: (0, 0)
    tab = lambda i: (i % table_blocks, 0)
    widths = [(NSA_WIDTH, BF16), (NSA_WIDTH, BF16), (4 * KV_WIDTH, F32), (2 * KV_WIDTH, F32), (4 * KV_WIDTH, BF16), None,
              (GLA_QK_WIDTH, F32), (GLA_QK_WIDTH, F32), (GLA_WIDTH, F32), (GLA_WIDTH, F32), (MISC_WIDTH, F32)]
    out_specs = [None if w is None else pl.BlockSpec((tm, w[0]), row) for w in widths]
    out_shape = [None if w is None else jax.ShapeDtypeStruct((n, w[0]), w[1]) for w in widths]
    out_specs[5] = pl.BlockSpec((tm // LANES, 2 * KV_WIDTH, LANES), lambda i: (i, 0, 0))
    out_shape[5] = jax.ShapeDtypeStruct((n // LANES, 2 * KV_WIDTH, LANES), BF16)
    if seq_len is not None:
        tiles = seq_len // tm
        for i in (0, 1, 2, 3):
            w, dt = widths[i]
            out_specs[i] = pl.BlockSpec((1, w, tm), lambda i: (i // tiles, 0, i % tiles))
            out_shape[i] = jax.ShapeDtypeStruct((n // seq_len, w, seq_len), dt)
    return pl.pallas_call(
        functools.partial(_proj_body, rows_minor=seq_len is not None),
        grid=(n // tm,),
        in_specs=[
            pl.BlockSpec((tm, D_MODEL), row),
            pl.BlockSpec((1, D_MODEL), const),
            pl.BlockSpec((D_MODEL, PROJ_WIDTH), const),
            pl.BlockSpec((tm, LANES), tab),
            pl.BlockSpec((tm, LANES), tab),
            pl.BlockSpec((1, NSA_WIDTH), const),
            pl.BlockSpec((1, KV_WIDTH), const),
            pl.BlockSpec((1, KV_WIDTH), const),
            pl.BlockSpec((NSA_WIDTH, NSA_WIDTH), const),
        ],
        out_specs=out_specs,
        out_shape=out_shape,
        compiler_params=pltpu.CompilerParams(dimension_semantics=("parallel",), vmem_limit_bytes=VMEM_LIMIT),
        name="proj",
    )(x2d, gmix, w_perm, cos_t, sin_t, qg, kg1, kg2, ones512)


def _cmp_body(*refs, n_src, n_prefetch):
    refs = refs[n_prefetch:]
    n_refs = len(refs) - 15
    src = refs[:n_refs]
    (perm_ref, wk_ref, wv_ref, posk_ref, w1k_ref, posv_ref, w1v_ref, w2k_ref, w2v_ref, kg0_ref, ones_ref,
     kc_ref, vc_ref, xk_sc, xv_sc) = refs[n_refs:]
    chunks_per_slab = PAGE_SIZE // CMP_STRIDE

    def slab(i):
        if n_refs == 1:
            return src[0][:, :, i * PAGE_SIZE:(i + 1) * PAGE_SIZE].reshape(2 * KV_WIDTH, PAGE_SIZE)
        return src[i][...].reshape(2 * KV_WIDTH, PAGE_SIZE)

    for i in range(0, n_src, 2):
        two = jnp.concatenate([slab(i), slab(i + 1)], axis=1).astype(BF16)
        by_row = _dot_nt(perm_ref[...], two)
        for j in range(2):
            for s in range(CMP_STRIDE):
                piece = by_row[j * PAGE_SIZE + s * chunks_per_slab:j * PAGE_SIZE + (s + 1) * chunks_per_slab, :]
                rows = slice((i + j) * chunks_per_slab, (i + j + 1) * chunks_per_slab)
                xk_sc[rows, s * KV_WIDTH:(s + 1) * KV_WIDTH] = piece[:, 0:KV_WIDTH]
                xv_sc[rows, s * KV_WIDTH:(s + 1) * KV_WIDTH] = piece[:, KV_WIDTH:2 * KV_WIDTH]
    n_chunks = n_src * chunks_per_slab
    acc_k = _dot(xk_sc[...].astype(BF16), wk_ref[...])
    acc_v = _dot(xv_sc[...].astype(BF16), wv_ref[...])
    hd = CMP_HIDDEN

    def pos_term(pos_ref, w1_ref):
        p = jnp.broadcast_to(pos_ref[...], (8, pos_ref.shape[1]))
        p_hi, p_lo = _split(p)
        w_hi, w_lo = _split(w1_ref[...])
        return (_dot(p_hi, w_hi) + _dot(p_hi, w_lo) + _dot(p_lo, w_hi))[0:1, :]

    def hidden(acc, pos):
        out = []
        for h in range(NSA_KV_HEADS):
            j0 = acc[:, 2 * h * hd:(2 * h + 1) * hd]
            j1 = acc[:, (2 * h + 1) * hd:(2 * h + 2) * hd]
            out.append(j0 + pltpu.roll(j1, n_chunks - 1, 0) + pos)
        return _silu(jnp.concatenate(out, axis=1)).astype(BF16)

    kc = _dot(hidden(acc_k, pos_term(posk_ref, w1k_ref)), w2k_ref[...])
    kc_ref[0] = _head_norm(kc, ones_ref[...], kg0_ref[...])
    vc_ref[0] = _dot(hidden(acc_v, pos_term(posv_ref, w1v_ref)), w2v_ref[...])


def _cmp1_weights(w1):
    w = w1.reshape(2, CMP_STRIDE, HEAD_DIM, CMP_HIDDEN)
    eye = jnp.eye(NSA_KV_HEADS, dtype=w1.dtype)
    big = jnp.einsum("jsde,hg->shdgje", w, eye)
    return big.reshape(CMP_STRIDE * KV_WIDTH, 4 * CMP_HIDDEN).astype(BF16)


def _cmp_call(src_args, src_specs, n_src, n_prefetch, grid_spec_fn, consts, n_seq, name):
    chunks = n_src * PAGE_SIZE // CMP_STRIDE
    lhs = pltpu.VMEM((chunks, CMP_STRIDE * KV_WIDTH), F32)
    const_specs = [pl.BlockSpec(c.shape, (lambda *_, nd=c.ndim: (0,) * nd)) for c in consts]
    seq = lambda i, *_: (i, 0, 0)
    return pl.pallas_call(
        functools.partial(_cmp_body, n_src=n_src, n_prefetch=n_prefetch),
        grid_spec=grid_spec_fn(
            grid=(n_seq,),
            in_specs=src_specs + const_specs,
            out_specs=[pl.BlockSpec((1, chunks, KV_WIDTH), seq)] * 2,
            scratch_shapes=[lhs, lhs],
        ),
        out_shape=[jax.ShapeDtypeStruct((n_seq, chunks, KV_WIDTH), F32)] * 2,
        compiler_params=pltpu.CompilerParams(dimension_semantics=("parallel",), vmem_limit_bytes=VMEM_LIMIT),
        name=name,
    )(*src_args, *consts)


def _cmp_prompt(rows_t, consts):
    b, _, _, t = rows_t.shape
    spec = pl.BlockSpec((None, 2, KV_WIDTH, t), lambda i: (i, 0, 0, 0))
    grid_spec_fn = functools.partial(pltpu.PrefetchScalarGridSpec, num_scalar_prefetch=0)
    return _cmp_call([rows_t], [spec], t // PAGE_SIZE, 0, grid_spec_fn, consts, b, "cmp_prompt")


def _page_specs(kind_block, n_pages):
    def spec(i):
        return pl.BlockSpec((None, 2, KV_WIDTH, PAGE_SIZE), lambda b, pt: (pt[b, i], kind_block, 0, 0))
    return [spec(i) for i in range(n_pages)]


def _cmp_sample(cache_t, page_table, consts):
    bs, n_pages = page_table.shape
    grid_spec_fn = functools.partial(pltpu.PrefetchScalarGridSpec, num_scalar_prefetch=1)
    return _cmp_call([page_table] + [cache_t] * n_pages, _page_specs(0, n_pages), n_pages, 1, grid_spec_fn, consts, bs,
                     "cmp_sample")


def _stack_heads(q, kvh):
    base = kvh * NSA_GROUP * HEAD_DIM
    return jnp.concatenate([q[:, base + g * HEAD_DIM: base + (g + 1) * HEAD_DIM] for g in range(NSA_GROUP)], axis=0)


def _unstack_heads(o, rows):
    return jnp.concatenate([o[g * rows:(g + 1) * rows] for g in range(NSA_GROUP)], axis=1)


def _group_sum(p, rows):
    out = p[0:rows]
    for g in range(1, NSA_GROUP):
        out = out + p[g * rows:(g + 1) * rows]
    return out


def _select_blocks(imp, qpos, n_blocks):
    blk = lax.broadcasted_iota(jnp.int32, imp.shape, 1)
    visible = blk * SEL_BLOCK <= qpos
    forced = jnp.logical_or(blk == 0, blk == (qpos >> int(math.log2(SEL_BLOCK))))
    score = jnp.where(forced, FORCE_SCORE, jnp.where(visible, imp, -jnp.inf))
    beaten = jnp.zeros(imp.shape, F32)
    for i in range(n_blocks):
        ci = score[:, i:i + 1]
        earlier = jnp.where(blk > i, 1.0, 0.0)
        beaten = beaten + jnp.where(ci > score, 1.0, jnp.where(ci == score, earlier, 0.0))
    return jnp.where(beaten < float(SEL_TOPK), jnp.where(score > -jnp.inf, 1.0, 0.0), 0.0)


def _softmax_cols(s):
    m = jnp.max(s, axis=0, keepdims=True)
    m = jnp.where(m == -jnp.inf, 0.0, m)
    e = jnp.exp2(s - m)
    return e / jnp.maximum(jnp.sum(e, axis=0, keepdims=True), 1e-30)


def _select_blocks_t(imp_t, qpos, n_blocks):
    blk = lax.broadcasted_iota(jnp.int32, imp_t.shape, 0)
    visible = blk * SEL_BLOCK <= qpos
    forced = jnp.logical_or(blk == 0, blk == (qpos >> int(math.log2(SEL_BLOCK))))
    score = jnp.where(forced, FORCE_SCORE, jnp.where(visible, imp_t, -jnp.inf))
    beaten = jnp.zeros(imp_t.shape, F32)
    for i in range(n_blocks):
        ci = score[i:i + 1, :]
        earlier = jnp.where(blk > i, 1.0, 0.0)
        beaten = beaten + jnp.where(ci > score, 1.0, jnp.where(ci == score, earlier, 0.0))
    return jnp.where(beaten < float(SEL_TOPK), jnp.where(score > -jnp.inf, 1.0, 0.0), 0.0)


def _combine_branches(gate_logits, gex_ref, branches):
    gates = 1.0 / (1.0 + jnp.exp(-gate_logits))
    out = None
    for c, branch in enumerate(branches):
        term = _dot2(gates, gex_ref[c]) * branch
        out = term if out is None else out + term
    return out


def _nsa_prompt_body(qn_ref, qr_ref, kc_ref, vc_ref, kvb_ref, vt_ref, ovt_ref, ext_ref, misc_ref, gex_ref, o_ref, *,
                     seq_len):
    tq = Q_TILE
    tk = SEL_KEY_TILE
    q0 = pl.program_id(1) * tq
    n_cmp = seq_len // CMP_STRIDE - 1
    n_sel = seq_len // SEL_BLOCK
    cols = NSA_GROUP * tq
    qn_t = qn_ref[0]
    qr_t = qr_ref[0]
    kc = kc_ref[0].astype(BF16)
    vc_t = vc_ref[0].T.astype(BF16)
    qpos1 = q0 + lax.broadcasted_iota(jnp.int32, (1, tq), 1)
    qpos4 = jnp.concatenate([qpos1] * NSA_GROUP, axis=1)
    win_tiles = (WINDOW + tq) // LANES
    wstart = pl.multiple_of(jnp.maximum(q0 - WINDOW, 0), tq)
    n_tiles = (q0 + tq + tk - 1) // tk

    def group_q(q_t, kvh):
        base = kvh * NSA_GROUP * HEAD_DIM
        return jnp.concatenate([q_t[base + g * HEAD_DIM: base + (g + 1) * HEAD_DIM, :] for g in range(NSA_GROUP)],
                               axis=1)

    def per_head(o_t):
        return [o_t[:, g * tq:(g + 1) * tq] for g in range(NSA_GROUP)]

    oc, osel, ow = [], [], []
    kv_heads = range(NSA_KV_HEADS)
    dims = [slice(h * HEAD_DIM, (h + 1) * HEAD_DIM) for h in kv_heads]
    qs_ts = [group_q(qr_t, h) for h in kv_heads]

    s_c = [_dot(kc[:, dims[h]], group_q(qn_t, h)) for h in kv_heads]
    k_w = [kvb_ref[0, pl.ds(wstart, win_tiles * LANES), 2 * KV_WIDTH + h * HEAD_DIM:2 * KV_WIDTH + (h + 1) * HEAD_DIM]
           for h in kv_heads]
    s_w = [_dot(k_w[h], qs_ts[h]) for h in kv_heads]
    n_idx = lax.broadcasted_iota(jnp.int32, s_c[0].shape, 0)
    vis = jnp.logical_and(n_idx * CMP_STRIDE + (CMP_BLOCK - 1) <= qpos4, n_idx < n_cmp)
    p_c = [_softmax_cols(jnp.where(vis, s, -jnp.inf)) for s in s_c]
    rel = qpos4 - (wstart + lax.broadcasted_iota(jnp.int32, s_w[0].shape, 0))
    inside = jnp.logical_and(rel >= 0, rel < WINDOW)
    p_w = [_softmax_cols(jnp.where(inside, s, -jnp.inf)) for s in s_w]
    w0 = wstart // LANES
    for h in kv_heads:
        oc += per_head(_dot(vc_t[dims[h], :], p_c[h].astype(BF16)))
    for h in kv_heads:
        v_wt = jnp.concatenate([vt_ref[0, w0 + j, KV_WIDTH + h * HEAD_DIM:KV_WIDTH + (h + 1) * HEAD_DIM, :]
                                for j in range(win_tiles)], axis=1)
        ow += per_head(_dot(v_wt, p_w[h].astype(BF16)))

    imp_t = []
    for h in kv_heads:
        p_sum = p_c[h][:, 0:tq]
        for g in range(1, NSA_GROUP):
            p_sum = p_sum + p_c[h][:, g * tq:(g + 1) * tq]
        p_hi, p_lo = _split(p_sum)
        imp_t.append((_dot(ovt_ref[...], p_hi) + _dot(ovt_ref[...], p_lo))[0:n_sel, :])
    sel_both = _select_blocks_t(jnp.concatenate(imp_t, axis=1), jnp.concatenate([qpos1] * NSA_KV_HEADS, axis=1), n_sel)
    sel_both = jnp.concatenate([sel_both, jnp.zeros((LANES - n_sel, NSA_KV_HEADS * tq), F32)], axis=0).astype(BF16)
    sel_ts = [sel_both[:, h * tq:(h + 1) * tq] for h in kv_heads]

    ones_rows = jnp.ones((ONES_ROWS, tk), BF16)

    def sel_step(kt, carry):
        k0 = pl.multiple_of(kt * tk, tk)
        causal = (k0 + lax.broadcasted_iota(jnp.int32, (tk, tq), 0)) <= qpos1
        k_blk = [kvb_ref[0, pl.ds(k0, tk), h * HEAD_DIM:(h + 1) * HEAD_DIM] for h in kv_heads]
        v_t = [jnp.concatenate([vt_ref[0, kt * (tk // LANES) + j, h * HEAD_DIM:(h + 1) * HEAD_DIM, :]
                                for j in range(tk // LANES)] , axis=1) for h in kv_heads]
        v_t = [jnp.concatenate([v, ones_rows], axis=0) for v in v_t]
        chosen = [_dot(ext_ref[kt], sel_ts[h]) for h in kv_heads]
        bias = [jnp.where(jnp.logical_and(c > 0.5, causal), 0.0, -jnp.inf) for c in chosen]
        sc = [_dot(k_blk[h], qs_ts[h]) + jnp.concatenate([bias[h]] * NSA_GROUP, axis=1) for h in kv_heads]
        m_new = [jnp.maximum(carry[h][0], jnp.max(sc[h], axis=0, keepdims=True)) for h in kv_heads]
        pe = [jnp.exp2(sc[h] - m_new[h]).astype(BF16) for h in kv_heads]
        return tuple((m_new[h], jnp.exp2(carry[h][0] - m_new[h]) * carry[h][1] + _dot(v_t[h], pe[h])) for h in kv_heads)

    init = (jnp.full((1, cols), NEG, F32), jnp.zeros((HEAD_DIM + ONES_ROWS, cols), F32))
    for _, acc_f in lax.fori_loop(0, n_tiles, sel_step, (init,) * NSA_KV_HEADS):
        osel += per_head(acc_f[0:HEAD_DIM, :] / jnp.maximum(acc_f[HEAD_DIM:HEAD_DIM + 1, :], 1e-30))

    branches = [jnp.concatenate(o, axis=0).T for o in (oc, osel, ow)]
    o_ref[0] = _combine_branches(misc_ref[0], gex_ref, branches)


def _nsa_prompt(qn_t, qr_t, kc, vc, kvb, vt, ovt, ext, misc, gex):
    b, _, t = qn_t.shape
    chunks = kc.shape[1]
    qtile = lambda i, j: (i, 0, j)
    otile = lambda i, j: (i, j, 0)
    seq = lambda i, j: (i, 0, 0)
    return pl.pallas_call(
        functools.partial(_nsa_prompt_body, seq_len=t),
        grid=(b, t // Q_TILE),
        in_specs=[
            pl.BlockSpec((1, NSA_WIDTH, Q_TILE), qtile),
            pl.BlockSpec((1, NSA_WIDTH, Q_TILE), qtile),
            pl.BlockSpec((1, chunks, KV_WIDTH), seq),
            pl.BlockSpec((1, chunks, KV_WIDTH), seq),
            pl.BlockSpec((1, t, 4 * KV_WIDTH), seq),
            pl.BlockSpec((1,) + vt.shape[1:], lambda i, j: (i, 0, 0, 0)),
            pl.BlockSpec(ovt.shape, lambda i, j: (0, 0)),
            pl.BlockSpec(ext.shape, lambda i, j: (0, 0, 0)),
            pl.BlockSpec((1, Q_TILE, MISC_WIDTH), otile),
            pl.BlockSpec(gex.shape, lambda i, j: (0, 0, 0)),
        ],
        out_specs=pl.BlockSpec((1, Q_TILE, NSA_WIDTH), otile),
        out_shape=jax.ShapeDtypeStruct((b, t, NSA_WIDTH), F32),
        compiler_params=pltpu.CompilerParams(dimension_semantics=("parallel", "parallel"),
                                             vmem_limit_bytes=VMEM_LIMIT),
        name="nsa_prompt",
    )(qn_t, qr_t, kc, vc, kvb, vt, ovt, ext, misc, gex)


def _nsa_sample_body(*refs, n_pages, past_len, n_new):
    pages = refs[1:1 + n_pages]
    (qn_ref, qr_ref, kc_ref, vc_ref, ov_ref, ex_ref, kvb_ref, cw_ref, nw_ref, misc_ref, gex_ref,
     o_ref, neww_ref) = refs[1 + n_pages:]
    t = n_new
    rows4 = NSA_GROUP * t
    n_cmp = (past_len + t) // CMP_STRIDE - 1
    n_sel = -(-(past_len + t) // SEL_BLOCK)
    qn = qn_ref[0]
    qr = qr_ref[0]
    kc = kc_ref[0].astype(BF16)
    vc = vc_ref[0].astype(BF16)
    qpos1 = past_len + lax.broadcasted_iota(jnp.int32, (t, 1), 0)
    qpos4 = jnp.concatenate([qpos1] * NSA_GROUP, axis=0)
    qs = [_stack_heads(qr, kvh) for kvh in range(NSA_KV_HEADS)]

    kv_heads = range(NSA_KV_HEADS)
    dims = [slice(h * HEAD_DIM, (h + 1) * HEAD_DIM) for h in kv_heads]

    wb = cw_ref.shape[2]
    cw = cw_ref[0]
    nw_t = jnp.concatenate([nw_ref[0], jnp.zeros((LANES - t, 2 * KV_WIDTH), F32)], axis=0).T
    shifted = pltpu.roll(cw, wb - t, 1)
    lane = lax.broadcasted_iota(jnp.int32, (2 * KV_WIDTH, LANES), 1)
    neww_ref[0, :, 0:wb - LANES] = shifted[:, 0:wb - LANES]
    neww_ref[0, :, wb - LANES:wb] = jnp.where(lane >= LANES - t, pltpu.roll(nw_t, LANES - t, 1), shifted[:, wb - LANES:wb])
    keys_t = jnp.concatenate([cw, nw_t], axis=1).astype(BF16)

    zero = jnp.zeros((rows4, HEAD_DIM), BF16)
    q_bd = jnp.concatenate([jnp.concatenate([qs[0], zero], axis=1), jnp.concatenate([zero, qs[1]], axis=1)], axis=0)
    k_t = jnp.concatenate([p[0] for p in pages], axis=1).astype(BF16)
    v_t = jnp.concatenate([p[1] for p in pages], axis=1).astype(BF16)
    pad = jnp.zeros((LANES - t, KV_WIDTH), F32)
    k_new = jnp.concatenate([kvb_ref[0, :, 0:KV_WIDTH].astype(F32), pad], axis=0).astype(BF16)
    v_new = jnp.concatenate([kvb_ref[0, :, KV_WIDTH:2 * KV_WIDTH].astype(F32), pad], axis=0).astype(BF16)
    s_cmp = [_dot_nt(_stack_heads(qn, h), kc[:, dims[h]]) for h in kv_heads]
    s_win = [_dot(qs[h], keys_t[dims[h], :]) for h in kv_heads]
    s_past = _dot(q_bd, k_t)
    s_new = _dot_nt(q_bd, k_new)

    n_idx = lax.broadcasted_iota(jnp.int32, s_cmp[0].shape, 1)
    vis = jnp.logical_and(n_idx * CMP_STRIDE + (CMP_BLOCK - 1) <= qpos4, n_idx < n_cmp)
    p_cmp = [_softmax_rows(jnp.where(vis, s, -jnp.inf)) for s in s_cmp]
    o_cmp = jnp.concatenate([_unstack_heads(_dot(p_cmp[h].astype(BF16), vc[:, dims[h]]), t) for h in kv_heads], axis=1)
    imp = [_dot2(_group_sum(p_cmp[h], t), ov_ref[...]) for h in kv_heads]

    key_i = lax.broadcasted_iota(jnp.int32, s_win[0].shape, 1)
    q_t = lax.broadcasted_iota(jnp.int32, s_win[0].shape, 0) & (t - 1)
    rel = wb + q_t - key_i
    inside = jnp.logical_and(rel >= 0, rel < WINDOW)
    p_win = [_softmax_rows(jnp.where(inside, s, -jnp.inf)) for s in s_win]
    o_win = jnp.concatenate(
        [_unstack_heads(_dot_nt(p_win[h].astype(BF16), keys_t[KV_WIDTH + h * HEAD_DIM:KV_WIDTH + (h + 1) * HEAD_DIM, :]), t)
         for h in kv_heads], axis=1)

    sel = _select_blocks(jnp.concatenate(imp, axis=0), jnp.concatenate([qpos1] * NSA_KV_HEADS, axis=0), n_sel)
    sel_rows = jnp.concatenate(
        [jnp.concatenate([sel[h * t:(h + 1) * t, 0:ex_ref.shape[0]]] * NSA_GROUP, axis=0) for h in kv_heads],
        axis=0).astype(BF16)
    chosen = _dot(sel_rows, ex_ref[...])
    sc = jnp.where(chosen > 0.5, s_past, -jnp.inf)
    key_i = lax.broadcasted_iota(jnp.int32, s_new.shape, 1)
    q_t = lax.broadcasted_iota(jnp.int32, s_new.shape, 0) & (t - 1)
    sn = jnp.where(key_i <= q_t, s_new, -jnp.inf)
    m = jnp.maximum(jnp.max(sc, axis=-1, keepdims=True), jnp.max(sn, axis=-1, keepdims=True))
    e_past = jnp.exp2(sc - m)
    e_new = jnp.exp2(sn - m)
    denom = jnp.sum(e_past, axis=-1, keepdims=True) + jnp.sum(e_new, axis=-1, keepdims=True)
    o_both = (_dot_nt(e_past.astype(BF16), v_t) + _dot(e_new.astype(BF16), v_new)) / denom
    o_sel = jnp.concatenate([_unstack_heads(o_both[0:rows4, 0:HEAD_DIM], t),
                             _unstack_heads(o_both[rows4:2 * rows4, HEAD_DIM:KV_WIDTH], t)], axis=1)
    o_ref[0] = _combine_branches(misc_ref[0], gex_ref, [o_cmp, o_sel, o_win])


def _nsa_sample(cache_t, page_table, qn, qr, kc, vc, ov, ex, kvb, cache_win_t, new_win, misc, gex, past_len):
    bs, n_pages = page_table.shape
    t = qr.shape[1]
    chunks = kc.shape[1]
    wb = cache_win_t.shape[2]
    seq = lambda b, pt: (b, 0, 0)
    c2 = lambda b, pt: (0, 0)
    grid_spec = pltpu.PrefetchScalarGridSpec(
        num_scalar_prefetch=1,
        grid=(bs,),
        in_specs=_page_specs(1, n_pages) + [
            pl.BlockSpec((1, t, NSA_WIDTH), seq), pl.BlockSpec((1, t, NSA_WIDTH), seq),
            pl.BlockSpec((1, chunks, KV_WIDTH), seq), pl.BlockSpec((1, chunks, KV_WIDTH), seq),
            pl.BlockSpec(ov.shape, c2), pl.BlockSpec(ex.shape, c2),
            pl.BlockSpec((1, t, 4 * KV_WIDTH), seq),
            pl.BlockSpec((1, 2 * KV_WIDTH, wb), seq), pl.BlockSpec((1, t, 2 * KV_WIDTH), seq),
            pl.BlockSpec((1, t, MISC_WIDTH), seq), pl.BlockSpec(gex.shape, lambda b, pt: (0, 0, 0)),
        ],
        out_specs=[pl.BlockSpec((1, t, NSA_WIDTH), seq), pl.BlockSpec((1, 2 * KV_WIDTH, wb), seq)],
    )
    return pl.pallas_call(
        functools.partial(_nsa_sample_body, n_pages=n_pages, past_len=past_len, n_new=t),
        grid_spec=grid_spec,
        out_shape=[jax.ShapeDtypeStruct((bs, t, NSA_WIDTH), F32), jax.ShapeDtypeStruct((bs, 2 * KV_WIDTH, wb), F32)],
        compiler_params=pltpu.CompilerParams(dimension_semantics=("parallel",), vmem_limit_bytes=VMEM_LIMIT),
        name="nsa_sample",
    )(page_table, *([cache_t] * n_pages), qn, qr, kc, vc, ov, ex, kvb, cache_win_t, new_win, misc, gex)


def _gla_constants(n_seq):
    tile = GLA_TILE
    seq_len = tile // n_seq
    t = np.arange(tile)[:, None]
    r = np.arange(tile)[None, :]
    same = (t // seq_len) == (r // seq_len)
    ops = [same & (r <= t), same & (r > t)]
    masks = []
    m = 1
    while 2 * m <= seq_len:
        upper = (t % (2 * m)) >= m
        mid = (t // (2 * m)) * (2 * m) + m - 1
        ops.append(upper & (r > mid) & (r <= t))
        ops.append((~upper) & (r > t) & (r <= mid))
        same_blk = (t // (2 * m)) == (r // (2 * m))
        masks.append(same_blk & upper & ((r % (2 * m)) < m))
        m *= 2
    ops = np.concatenate([o.astype(np.float32) for o in ops], axis=0)
    masks = np.stack([k.astype(np.float32) for k in masks], axis=0)
    lane_seq = (np.arange(n_seq * GLA_DK)[None, :] // GLA_DK) == (np.arange(tile)[:, None] // seq_len)
    rep = (np.arange(n_seq * GLA_DK)[None, :] % GLA_DK) == np.arange(GLA_DK)[:, None]
    return (jnp.asarray(ops, BF16), jnp.asarray(masks, F32), jnp.asarray(lane_seq.astype(np.float32), F32),
            jnp.asarray(rep.astype(np.float32), BF16))


def _gla_body(gq_ref, gk_ref, gv_ref, gg_ref, misc_ref, s0_ref, wa_hi_ref, wa_lo_ref, ba_ref, ops_ref, masks_ref,
              lane_seq_ref, rep_ref, gnorm_ref, o_ref, sfin_ref, st_sc, *, n_levels):
    step = pl.program_id(1)

    @pl.when(step == 0)
    def _():
        st_sc[...] = s0_ref[...]

    tile = GLA_TILE
    groups = range(gq_ref.shape[0])
    chains = [(g, h) for g in groups for h in range(GLA_HEADS)]
    lane_seq = lane_seq_ref[...]
    rep = rep_ref[...]
    multi = lane_seq.shape[1] > GLA_DK

    def spread(a):
        a = a.astype(BF16)
        return (_dot(a, rep) * lane_seq).astype(BF16) if multi else a

    la, cums = [], []
    for g in groups:
        m_hi, m_lo = _split(misc_ref[g])
        x = _dot(m_hi, wa_hi_ref[...]) + _dot(m_hi, wa_lo_ref[...]) + _dot(m_lo, wa_hi_ref[...]) + ba_ref[...]
        la.append((jnp.minimum(x, 0.0) - jnp.log(1.0 + jnp.exp(-jnp.abs(x)))) * (1.0 / GLA_GATE_TAU))
    for g in groups:
        la_hi, la_lo = _split(la[g])
        cums.append(_dot(ops_ref[...], la_hi) + _dot(ops_ref[...], la_lo))

    dks = [slice(h * GLA_DK, (h + 1) * GLA_DK) for h in range(GLA_HEADS)]
    dvs = [slice(h * GLA_DV, (h + 1) * GLA_DV) for h in range(GLA_HEADS)]
    q = [gq_ref[g, :, dks[h]] * (GLA_DK ** -0.5) for g, h in chains]
    k = [gk_ref[g, :, dks[h]] for g, h in chains]
    v = [gv_ref[g, :, dvs[h]] for g, h in chains]
    st = [st_sc[g, h] for g, h in chains]
    o = [_dot_nt(spread(q[c] * jnp.exp(cums[g][0:tile, dks[h]])), st[c].astype(BF16)) for c, (g, h) in enumerate(chains)]
    a = [None] * len(chains)
    for lv in range(n_levels):
        for c, (g, h) in enumerate(chains):
            dq = cums[g][(2 + 2 * lv) * tile:(3 + 2 * lv) * tile, dks[h]]
            dkk = cums[g][(3 + 2 * lv) * tile:(4 + 2 * lv) * tile, dks[h]]
            term = _dot_nt((q[c] * jnp.exp(dq)).astype(BF16), (k[c] * jnp.exp(dkk)).astype(BF16)) * masks_ref[lv]
            a[c] = term if a[c] is None else a[c] + term
    o = [o[c] + _dot(a[c].astype(BF16), v[c].astype(BF16)) + jnp.sum(q[c] * k[c], axis=-1, keepdims=True) * v[c]
         for c in range(len(chains))]
    for c, (g, h) in enumerate(chains):
        la_h = la[g][:, dks[h]]
        if multi:
            total = jnp.sum(_dot2(la_h, rep) * lane_seq, axis=0, keepdims=True)
        else:
            total = jnp.sum(la_h, axis=0, keepdims=True)
        until_end = cums[g][tile:2 * tile, dks[h]]
        st_sc[g, h] = st[c] * jnp.exp(total) + _dot(v[c].T.astype(BF16), spread(k[c] * jnp.exp(until_end)))
    for c, (g, h) in enumerate(chains):
        ms = jnp.mean(o[c] * o[c], axis=-1, keepdims=True)
        o_ref[g, :, dvs[h]] = o[c] * lax.rsqrt(ms + EPS) * gnorm_ref[...] * _silu(gg_ref[g, :, dvs[h]])

    @pl.when(step == pl.num_programs(1) - 1)
    def _():
        sfin_ref[...] = st_sc[...]


def _gla(gq, gk, gv, gg, misc, s0_t, wa_hi, wa_lo, ba, gnorm, n_seq):
    groups, rows, _ = gq.shape
    steps = rows // GLA_TILE
    par = GLA_GROUPS_PER_STEP
    assert groups % par == 0
    ops, masks, lane_seq, rep = _gla_constants(n_seq)
    n_levels = masks.shape[0]
    tile = lambda g, s: (g, s, 0)
    c2 = lambda g, s: (0, 0)
    c3 = lambda g, s: (0, 0, 0)
    st_block = (par, GLA_HEADS, GLA_DV, n_seq * GLA_DK)
    st_map = lambda g, s: (g, 0, 0, 0)
    return pl.pallas_call(
        functools.partial(_gla_body, n_levels=n_levels),
        grid=(groups // par, steps),
        in_specs=[
            pl.BlockSpec((par, GLA_TILE, GLA_QK_WIDTH), tile), pl.BlockSpec((par, GLA_TILE, GLA_QK_WIDTH), tile),
            pl.BlockSpec((par, GLA_TILE, GLA_WIDTH), tile), pl.BlockSpec((par, GLA_TILE, GLA_WIDTH), tile),
            pl.BlockSpec((par, GLA_TILE, MISC_WIDTH), tile),
            pl.BlockSpec(st_block, st_map),
            pl.BlockSpec(wa_hi.shape, c2), pl.BlockSpec(wa_lo.shape, c2), pl.BlockSpec(ba.shape, c2),
            pl.BlockSpec(ops.shape, c2), pl.BlockSpec(masks.shape, c3),
            pl.BlockSpec(lane_seq.shape, c2), pl.BlockSpec(rep.shape, c2), pl.BlockSpec(gnorm.shape, c2),
        ],
        out_specs=[pl.BlockSpec((par, GLA_TILE, GLA_WIDTH), tile), pl.BlockSpec(st_block, st_map)],
        out_shape=[jax.ShapeDtypeStruct((groups, rows, GLA_WIDTH), F32),
                   jax.ShapeDtypeStruct((groups,) + st_block[1:], F32)],
        scratch_shapes=[pltpu.VMEM(st_block, F32)],
        compiler_params=pltpu.CompilerParams(dimension_semantics=("parallel", "arbitrary"),
                                             vmem_limit_bytes=VMEM_LIMIT),
        name="gla",
    )(gq, gk, gv, gg, misc, s0_t, wa_hi, wa_lo, ba, ops, masks, lane_seq, rep, gnorm)


def _out_body(x_ref, on_ref, og_ref, wout_ref, gffn_ref, wa_ref, wb_ref, wdown_ref, y_ref, hn_sc):
    c = pl.program_id(1)

    @pl.when(c == 0)
    def _():
        x1 = (x_ref[...] + _dot(on_ref[...].astype(BF16), wout_ref[0:NSA_WIDTH, :])
              + _dot(og_ref[...].astype(BF16), wout_ref[NSA_WIDTH:NSA_WIDTH + GLA_WIDTH, :]))
        ms = jnp.mean(x1 * x1, axis=-1, keepdims=True)
        hn_sc[...] = (x1 * lax.rsqrt(ms + EPS) * gffn_ref[...]).astype(BF16)
        y_ref[...] = x1

    hn = hn_sc[...]
    act = _silu(_dot(hn, wa_ref[...])) * _dot(hn, wb_ref[...])
    y_ref[...] += _dot(act.astype(BF16), wdown_ref[...])


def _out(x2d, o_nsa, og, wout, gffn, wup, wdown):
    n = x2d.shape[0]
    tm = min(OUT_ROW_TILE, n)
    assert n % tm == 0
    n_chunks = FFN_HIDDEN // FFN_CHUNK
    row = lambda i, c: (i, 0)
    c2 = lambda i, c: (0, 0)
    return pl.pallas_call(
        _out_body,
        grid=(n // tm, n_chunks),
        in_specs=[
            pl.BlockSpec((tm, D_MODEL), row),
            pl.BlockSpec((tm, NSA_WIDTH), row), pl.BlockSpec((tm, GLA_WIDTH), row),
            pl.BlockSpec(wout.shape, c2),
            pl.BlockSpec(gffn.shape, c2),
            pl.BlockSpec((D_MODEL, FFN_CHUNK), lambda i, c: (0, c)),
            pl.BlockSpec((D_MODEL, FFN_CHUNK), lambda i, c: (0, n_chunks + c)),
            pl.BlockSpec((FFN_CHUNK, D_MODEL), lambda i, c: (c, 0)),
        ],
        out_specs=pl.BlockSpec((tm, D_MODEL), row),
        out_shape=jax.ShapeDtypeStruct((n, D_MODEL), F32),
        scratch_shapes=[pltpu.VMEM((tm, D_MODEL), BF16)],
        compiler_params=pltpu.CompilerParams(dimension_semantics=("parallel", "arbitrary"),
                                             vmem_limit_bytes=VMEM_LIMIT),
        name="out",
    )(x2d, o_nsa, og, wout, gffn, wup, wup, wdown)


def _rope_tables(pos):
    half = HEAD_DIM // 2
    inv = ROPE_THETA ** (-jnp.arange(half, dtype=F32) / half)
    ang = pos.astype(F32)[:, None] * inv[None, :]
    cos, sin = jnp.cos(ang), jnp.sin(ang)
    reps = LANES // HEAD_DIM
    return jnp.tile(jnp.concatenate([cos, cos], axis=1), (1, reps)), jnp.tile(jnp.concatenate([-sin, sin], axis=1), (1, reps))


def _overlap_matrix(n_chunks, n_cols):
    start = np.arange(n_chunks)[:, None] * CMP_STRIDE
    j0 = np.arange(n_cols)[None, :] * SEL_BLOCK
    return jnp.asarray(((start < j0 + SEL_BLOCK) & (start + CMP_BLOCK > j0)).astype(np.float32), BF16)


def _expand_matrix(n_tiles, keys):
    key = np.arange(n_tiles)[:, None, None] * keys + np.arange(keys)[None, None, :]
    blk = np.arange(LANES)[None, :, None]
    return jnp.asarray((key // SEL_BLOCK == blk).astype(np.float32), BF16)


def _chunk_row_permutation():
    out = np.arange(2 * PAGE_SIZE)
    page, within = out // PAGE_SIZE, out % PAGE_SIZE
    chunks = PAGE_SIZE // CMP_STRIDE
    src = page * PAGE_SIZE + (within % chunks) * CMP_STRIDE + within // chunks
    return jnp.asarray((src[:, None] == np.arange(2 * PAGE_SIZE)[None, :]).astype(np.float32), BF16)


def _block_diag_ones(width):
    i = np.arange(width)
    return jnp.asarray((i[:, None] // HEAD_DIM == i[None, :] // HEAD_DIM).astype(np.float32), BF16)


def _gate_expand():
    lane = np.arange(MISC_WIDTH)[None, :, None]
    out = np.arange(NSA_WIDTH)[None, None, :]
    c = np.arange(3)[:, None, None]
    return jnp.asarray(((lane == (out // HEAD_DIM) * 3 + c) & (lane < GATE_COLS)).astype(np.float32), BF16)


def _pad_cols(w, width):
    return jnp.pad(w, ((0, 0), (0, width - w.shape[1])))


def kernel(x_prompt, x_sample, cache_nsa_kv, cache_win_kv, state_gla, page_table, w_norm_mix, w_in, nsa_q_gain,
           nsa_k_gain, cmp_pos_k, cmp_w1_k, cmp_w2_k, cmp_pos_v, cmp_w1_v, cmp_w2_v, gla_w_a2, gla_b_a,
           gla_norm_gain, w_out, w_norm_ffn, w_ffn_up, w_ffn_down):
    b_p, t_p, _ = x_prompt.shape
    b_s, t_s, _ = x_sample.shape
    depth = w_in.shape[0]
    assert depth == 1, "single layer"
    n_pages = page_table.shape[1]
    past_len = n_pages * PAGE_SIZE
    n_pool = cache_nsa_kv.shape[1]
    wb = cache_win_kv.shape[2]
    assert t_p % SEL_KEY_TILE == 0 and t_p >= WINDOW + Q_TILE and (b_p * t_p) % ROW_TILE == 0
    assert (b_s * t_s) % ROW_TILE == 0 and GLA_TILE % t_s == 0 and b_s % (GLA_TILE // t_s) == 0
    assert wb == WINDOW and ROW_TILE % t_s == 0 and t_p % PAGE_SIZE == 0 and past_len // SEL_BLOCK <= LANES
    l = 0

    q_w, kv_w, gate_w, gq_w, gk_w, gv_w, alow_w, gg_w = jnp.split(
        w_in[l], np.cumsum([NSA_WIDTH, 6 * KV_WIDTH, GATE_COLS, GLA_QK_WIDTH, GLA_QK_WIDTH, GLA_WIDTH,
                            GLA_GATE_RANK])[:].tolist(), axis=1)
    misc_w = _pad_cols(jnp.concatenate([gate_w, alow_w], axis=1), MISC_WIDTH)
    w_perm = jnp.concatenate([q_w, kv_w, gq_w, gk_w, gv_w, gg_w, misc_w], axis=1).astype(BF16)
    gmix = w_norm_mix[l][None, :]
    qg = jnp.tile(nsa_q_gain[l], NSA_HEADS)[None, :]
    kg = [jnp.tile(nsa_k_gain[l, i], NSA_KV_HEADS)[None, :] for i in range(3)]
    ones512 = _block_diag_ones(NSA_WIDTH)
    ones128 = _block_diag_ones(LANES)
    wa = jnp.zeros((MISC_WIDTH, GLA_QK_WIDTH), F32).at[GATE_COLS:GATE_COLS + GLA_GATE_RANK].set(gla_w_a2[l])
    wa_hi, wa_lo = _split(wa)
    ba = gla_b_a[l][None, :]
    gnorm = gla_norm_gain[l][None, :]
    wk1 = _cmp1_weights(cmp_w1_k[l])
    wv1 = _cmp1_weights(cmp_w1_v[l])
    eye2 = jnp.eye(NSA_KV_HEADS, dtype=F32)
    w2k_bd = jnp.einsum("ed,hg->hegd", cmp_w2_k[l], eye2).reshape(2 * CMP_HIDDEN, KV_WIDTH).astype(BF16)
    w2v_bd = jnp.einsum("ed,hg->hegd", cmp_w2_v[l], eye2).reshape(2 * CMP_HIDDEN, KV_WIDTH).astype(BF16)
    flat = CMP_BLOCK * HEAD_DIM
    posk, posv = cmp_pos_k[l].reshape(1, flat), cmp_pos_v[l].reshape(1, flat)
    w1k_flat, w1v_flat = cmp_w1_k[l].reshape(flat, CMP_HIDDEN), cmp_w1_v[l].reshape(flat, CMP_HIDDEN)
    cmp_consts = (_chunk_row_permutation(), wk1, wv1, posk, w1k_flat, posv, w1v_flat, w2k_bd, w2v_bd, kg[0], ones128)
    gex = _gate_expand()
    wout = w_out[l].astype(BF16)
    gffn = w_norm_ffn[l][None, :]
    wup = w_ffn_up[l].astype(BF16)
    wdown = w_ffn_down[l].astype(BF16)

    cos_p, sin_p = _rope_tables(jnp.arange(t_p))
    xp2 = x_prompt.reshape(b_p * t_p, D_MODEL)
    (qn_t, qr_t, rows_t, win_t, kvb, vt, gq, gk, gv, gg, misc) = _proj(
        xp2, gmix, w_perm, cos_p, sin_p, qg, kg[1], kg[2], ones512, t_p // ROW_TILE, seq_len=t_p)
    kc_p, vc_p = _cmp_prompt(rows_t.reshape(b_p, 4, KV_WIDTH, t_p), cmp_consts)
    ovt_p = _overlap_matrix(t_p // CMP_STRIDE, LANES).T
    ext_p = jnp.swapaxes(_expand_matrix(t_p // SEL_KEY_TILE, SEL_KEY_TILE), 1, 2)
    o_nsa = _nsa_prompt(qn_t, qr_t, kc_p, vc_p, kvb.reshape(b_p, t_p, -1),
                        vt.reshape(b_p, t_p // LANES, 2 * KV_WIDTH, LANES), ovt_p, ext_p,
                        misc.reshape(b_p, t_p, MISC_WIDTH), gex)
    s0_p = jnp.zeros((b_p, GLA_HEADS, GLA_DV, GLA_DK), F32)
    r3 = lambda a, g=b_p: a.reshape(g, -1, a.shape[-1])
    og, sfin_p = _gla(r3(gq), r3(gk), r3(gv), r3(gg), r3(misc), s0_p, wa_hi, wa_lo, ba, gnorm, 1)
    y_p = _out(xp2, o_nsa.reshape(b_p * t_p, -1), og.reshape(b_p * t_p, -1), wout, gffn, wup, wdown)
    out_rows_p = rows_t.reshape(b_p, 4, NSA_KV_HEADS, HEAD_DIM, t_p).transpose(0, 4, 1, 2, 3)[None]
    wlen = min(WINDOW, t_p)
    out_win_p = win_t[:, :, t_p - wlen:].reshape(b_p, 2, NSA_KV_HEADS, HEAD_DIM, wlen).transpose(0, 4, 1, 2, 3)[None]
    out_gla_p = jnp.swapaxes(sfin_p, 2, 3)[None].astype(state_gla.dtype)

    pos_s = past_len + jnp.arange(t_s)
    cos_s, sin_s = _rope_tables(pos_s)
    cos_s = jnp.tile(cos_s, (ROW_TILE // t_s, 1))
    sin_s = jnp.tile(sin_s, (ROW_TILE // t_s, 1))
    xs2 = x_sample.reshape(b_s * t_s, D_MODEL)
    (qn_s, qr_s, rows_s, win_s, kvb_s, _, gq_s, gk_s, gv_s, gg_s, misc_s) = _proj(
        xs2, gmix, w_perm, cos_s, sin_s, qg, kg[1], kg[2], ones512, 1)
    cache_t = cache_nsa_kv[l].transpose(0, 2, 3, 4, 1).reshape(n_pool, 4, KV_WIDTH, PAGE_SIZE)
    kc_s, vc_s = _cmp_sample(cache_t, page_table, cmp_consts)
    n_sel_s = -(-(past_len + t_s) // SEL_BLOCK)
    sel_cols = -(-n_sel_s // LANES) * LANES
    ov_s = _overlap_matrix(past_len // CMP_STRIDE, sel_cols)
    ex_s = _expand_matrix(1, past_len)[0]
    seq3 = lambda a: a.reshape(b_s, t_s, a.shape[-1])
    cache_win_t = cache_win_kv[l].transpose(0, 2, 3, 4, 1).reshape(b_s, 2 * KV_WIDTH, wb)
    o_nsa_s, new_win_t = _nsa_sample(cache_t, page_table, seq3(qn_s), seq3(qr_s), kc_s, vc_s, ov_s, ex_s, seq3(kvb_s),
                                     cache_win_t, seq3(win_s), seq3(misc_s), gex, past_len)
    n_seq = GLA_TILE // t_s
    groups = b_s // n_seq
    s0_s = state_gla[l].astype(F32).reshape(groups, n_seq, GLA_HEADS, GLA_DK, GLA_DV)
    s0_s = s0_s.transpose(0, 2, 4, 1, 3).reshape(groups, GLA_HEADS, GLA_DV, n_seq * GLA_DK)
    rg = lambda a: a.reshape(groups, GLA_TILE, a.shape[-1])
    og_s, sfin_s = _gla(rg(gq_s), rg(gk_s), rg(gv_s), rg(gg_s), rg(misc_s), s0_s, wa_hi, wa_lo, ba, gnorm, n_seq)
    y_s = _out(xs2, o_nsa_s.reshape(b_s * t_s, -1), og_s.reshape(b_s * t_s, -1), wout, gffn, wup, wdown)
    out_rows_s = rows_s.reshape(1, b_s, t_s, 4, NSA_KV_HEADS, HEAD_DIM)
    out_win_s = new_win_t.reshape(b_s, 2, NSA_KV_HEADS, HEAD_DIM, wb).transpose(0, 4, 1, 2, 3)[None]
    out_gla_s = sfin_s.reshape(groups, GLA_HEADS, GLA_DV, n_seq, GLA_DK).transpose(0, 3, 1, 4, 2)
    out_gla_s = out_gla_s.reshape(1, b_s, GLA_HEADS, GLA_DK, GLA_DV).astype(state_gla.dtype)

    return (y_p.reshape(b_p, t_p, D_MODEL), y_s.reshape(b_s, t_s, D_MODEL), out_rows_p, out_win_p, out_gla_p,
            out_rows_s, out_win_s, out_gla_s)
```

```python
import functools
import math

import numpy as np
import jax
import jax.numpy as jnp
from jax import lax
from jax.experimental import pallas as pl
from jax.experimental.pallas import tpu as pltpu

F32 = jnp.float32
BF16 = jnp.bfloat16

D_MODEL = 1024
PAGE_SIZE = 128
NSA_HEADS = 8
NSA_KV_HEADS = 2
NSA_GROUP = NSA_HEADS // NSA_KV_HEADS
HEAD_DIM = 64
NSA_WIDTH = NSA_HEADS * HEAD_DIM
KV_WIDTH = NSA_KV_HEADS * HEAD_DIM
CMP_BLOCK = 32
CMP_STRIDE = 16
CMP_HIDDEN = 2 * HEAD_DIM
SEL_BLOCK = 64
SEL_TOPK = 16
WINDOW = 512
FORCE_SCORE = 1.0e4
GLA_HEADS = 4
GLA_DK = 64
GLA_DV = 128
GLA_WIDTH = GLA_HEADS * GLA_DV
GLA_QK_WIDTH = GLA_HEADS * GLA_DK
GLA_GATE_RANK = 16
GLA_GATE_TAU = 16.0
FFN_HIDDEN = -(-8 * D_MODEL // (3 * 256)) * 256
ROPE_THETA = 10000.0
EPS = 1e-6
NEG = -1.0e30

LANES = 128
MISC_WIDTH = LANES
GATE_COLS = 3 * NSA_HEADS
PROJ_WIDTH = NSA_WIDTH + 6 * KV_WIDTH + 2 * GLA_QK_WIDTH + 2 * GLA_WIDTH + MISC_WIDTH
ROW_TILE = 512
OUT_ROW_TILE = 512
Q_TILE = 256
SEL_KEY_TILE = 512
ONES_ROWS = 16
GLA_TILE = 128
GLA_GROUPS_PER_STEP = 4
FFN_CHUNK = FFN_HIDDEN // 2
VMEM_LIMIT = 56 * 1024 * 1024


def _dot(a, b):
    return jnp.dot(a, b, preferred_element_type=F32)


def _dot_nt(a, b):
    return lax.dot_general(a, b, (((1,), (1,)), ((), ())), preferred_element_type=F32)


def _split(x):
    hi = x.astype(BF16)
    lo = (x - hi.astype(F32)).astype(BF16)
    return hi, lo


def _dot2(x, w_bf16):
    hi, lo = _split(x)
    return _dot(hi, w_bf16) + _dot(lo, w_bf16)


def _silu(x):
    return x * (1.0 / (1.0 + jnp.exp(-x)))


def _head_norm(v, ones_bd, gain):
    ss = _dot((v * v).astype(BF16), ones_bd)
    return v * lax.rsqrt(ss * (1.0 / HEAD_DIM) + EPS) * gain


def _rope(v, cos, sin_signed):
    width = v.shape[-1]
    reps = width // LANES
    if reps > 1:
        cos = jnp.concatenate([cos] * reps, axis=1)
        sin_signed = jnp.concatenate([sin_signed] * reps, axis=1)
    lane = lax.broadcasted_iota(jnp.int32, v.shape, 1)
    first_half = (lane & (HEAD_DIM - 1)) < (HEAD_DIM // 2)
    rot = jnp.where(first_half, pltpu.roll(v, width - HEAD_DIM // 2, 1), pltpu.roll(v, HEAD_DIM // 2, 1))
    return v * cos + rot * sin_signed


def _softmax_rows(s):
    m = jnp.max(s, axis=-1, keepdims=True)
    m = jnp.where(m == -jnp.inf, 0.0, m)
    e = jnp.exp2(s - m)
    return e / jnp.maximum(jnp.sum(e, axis=-1, keepdims=True), 1e-30)


def _proj_body(x_ref, gmix_ref, w_ref, cos_ref, sin_ref, qg_ref, kg1_ref, kg2_ref, ones_ref,
               qn_ref, qr_ref, rows_ref, win_ref, kvb_ref, vt_ref, gq_ref, gk_ref, gv_ref, gg_ref, misc_ref, *,
               rows_minor):
    x = x_ref[...]
    ms = jnp.mean(x * x, axis=-1, keepdims=True)
    h = (x * lax.rsqrt(ms + EPS) * gmix_ref[...]).astype(BF16)
    cos = cos_ref[...]
    sin = sin_ref[...]
    ones512 = ones_ref[...]
    ones128 = ones512[:LANES, :LANES]
    scale = HEAD_DIM ** -0.5 * math.log2(math.e)

    off = 0
    q = _dot(h, w_ref[:, off:off + NSA_WIDTH])
    off += NSA_WIDTH
    qn = _head_norm(q, ones512, qg_ref[...])
    qr = _rope(qn, cos, sin)
    if rows_minor:
        qn_ref[0] = (qn * scale).T.astype(BF16)
        qr_ref[0] = (qr * scale).T.astype(BF16)
    else:
        qn_ref[...] = (qn * scale).astype(BF16)
        qr_ref[...] = (qr * scale).astype(BF16)

    kv = _dot(h, w_ref[:, off:off + 6 * KV_WIDTH])
    off += 6 * KV_WIDTH
    k_slc = _rope(_head_norm(kv[:, 2 * KV_WIDTH:3 * KV_WIDTH], ones128, kg1_ref[...]), cos, sin)
    k_win = _rope(_head_norm(kv[:, 4 * KV_WIDTH:5 * KV_WIDTH], ones128, kg2_ref[...]), cos, sin)
    v_slc = kv[:, 3 * KV_WIDTH:4 * KV_WIDTH]
    v_win = kv[:, 5 * KV_WIDTH:6 * KV_WIDTH]
    v_slc_t = v_slc.T
    v_win_t = v_win.T
    for j in range(v_slc.shape[0] // LANES):
        vt_ref[j, 0:KV_WIDTH, :] = v_slc_t[:, j * LANES:(j + 1) * LANES].astype(BF16)
        vt_ref[j, KV_WIDTH:2 * KV_WIDTH, :] = v_win_t[:, j * LANES:(j + 1) * LANES].astype(BF16)
    if rows_minor:
        rows_ref[0, 0:2 * KV_WIDTH, :] = kv[:, 0:2 * KV_WIDTH].T
        rows_ref[0, 2 * KV_WIDTH:3 * KV_WIDTH, :] = k_slc.T
        rows_ref[0, 3 * KV_WIDTH:4 * KV_WIDTH, :] = v_slc_t
        win_ref[0, 0:KV_WIDTH, :] = k_win.T
        win_ref[0, KV_WIDTH:2 * KV_WIDTH, :] = v_win_t
    else:
        rows_ref[:, 0:2 * KV_WIDTH] = kv[:, 0:2 * KV_WIDTH]
        rows_ref[:, 2 * KV_WIDTH:3 * KV_WIDTH] = k_slc
        rows_ref[:, 3 * KV_WIDTH:4 * KV_WIDTH] = v_slc
        win_ref[:, 0:KV_WIDTH] = k_win
        win_ref[:, KV_WIDTH:2 * KV_WIDTH] = v_win
    kvb_ref[:, 0:KV_WIDTH] = k_slc.astype(BF16)
    kvb_ref[:, KV_WIDTH:2 * KV_WIDTH] = v_slc.astype(BF16)
    kvb_ref[:, 2 * KV_WIDTH:3 * KV_WIDTH] = k_win.astype(BF16)
    kvb_ref[:, 3 * KV_WIDTH:4 * KV_WIDTH] = v_win.astype(BF16)

    gq_ref[...] = _dot(h, w_ref[:, off:off + GLA_QK_WIDTH])
    off += GLA_QK_WIDTH
    gk_ref[...] = _dot(h, w_ref[:, off:off + GLA_QK_WIDTH])
    off += GLA_QK_WIDTH
    gv_ref[...] = _dot(h, w_ref[:, off:off + GLA_WIDTH])
    off += GLA_WIDTH
    gg_ref[...] = _dot(h, w_ref[:, off:off + GLA_WIDTH])
    off += GLA_WIDTH
    misc_ref[...] = _dot(h, w_ref[:, off:off + MISC_WIDTH])


def _proj(x2d, gmix, w_perm, cos_t, sin_t, qg, kg1, kg2, ones512, table_blocks, seq_len=None):
    n = x2d.shape[0]
    tm = ROW_TILE
    row = lambda i: (i, 0)
    const = lambda i: (0, 0)
    tab = lambda i: (i % table_blocks, 0)
    widths = [(NSA_WIDTH, BF16), (NSA_WIDTH, BF16), (4 * KV_WIDTH, F32), (2 * KV_WIDTH, F32), (4 * KV_WIDTH, BF16), None,
              (GLA_QK_WIDTH, F32), (GLA_QK_WIDTH, F32), (GLA_WIDTH, F32), (GLA_WIDTH, F32), (MISC_WIDTH, F32)]
    out_specs = [None if w is None else pl.BlockSpec((tm, w[0]), row) for w in widths]
    out_shape = [None if w is None else jax.ShapeDtypeStruct((n, w[0]), w[1]) for w in widths]
    out_specs[5] = pl.BlockSpec((tm // LANES, 2 * KV_WIDTH, LANES), lambda i: (i, 0, 0))
    out_shape[5] = jax.ShapeDtypeStruct((n // LANES, 2 * KV_WIDTH, LANES), BF16)
    if seq_len is not None:
        tiles = seq_len // tm
        for i in (0, 1, 2, 3):
            w, dt = widths[i]
            out_specs[i] = pl.BlockSpec((1, w, tm), lambda i: (i // tiles, 0, i % tiles))
            out_shape[i] = jax.ShapeDtypeStruct((n // seq_len, w, seq_len), dt)
    return pl.pallas_call(
        functools.partial(_proj_body, rows_minor=seq_len is not None),
        grid=(n // tm,),
        in_specs=[
            pl.BlockSpec((tm, D_MODEL), row),
            pl.BlockSpec((1, D_MODEL), const),
            pl.BlockSpec((D_MODEL, PROJ_WIDTH), const),
            pl.BlockSpec((tm, LANES), tab),
            pl.BlockSpec((tm, LANES), tab),
            pl.BlockSpec((1, NSA_WIDTH), const),
            pl.BlockSpec((1, KV_WIDTH), const),
            pl.BlockSpec((1, KV_WIDTH), const),
            pl.BlockSpec((NSA_WIDTH, NSA_WIDTH), const),
        ],
        out_specs=out_specs,
        out_shape=out_shape,
        compiler_params=pltpu.CompilerParams(dimension_semantics=("parallel",), vmem_limit_bytes=VMEM_LIMIT),
        name="proj",
    )(x2d, gmix, w_perm, cos_t, sin_t, qg, kg1, kg2, ones512)


def _cmp_body(*refs, n_src, n_prefetch):
    refs = refs[n_prefetch:]
    n_refs = len(refs) - 15
    src = refs[:n_refs]
    (perm_ref, wk_ref, wv_ref, posk_ref, w1k_ref, posv_ref, w1v_ref, w2k_ref, w2v_ref, kg0_ref, ones_ref,
     kc_ref, vc_ref, xk_sc, xv_sc) = refs[n_refs:]
    chunks_per_slab = PAGE_SIZE // CMP_STRIDE

    def slab(i):
        if n_refs == 1:
            return src[0][:, :, i * PAGE_SIZE:(i + 1) * PAGE_SIZE].reshape(2 * KV_WIDTH, PAGE_SIZE)
        return src[i][...].reshape(2 * KV_WIDTH, PAGE_SIZE)

    for i in range(0, n_src, 2):
        two = jnp.concatenate([slab(i), slab(i + 1)], axis=1).astype(BF16)
        by_row = _dot_nt(perm_ref[...], two).astype(BF16)
        rows = slice(i * chunks_per_slab, (i + 2) * chunks_per_slab)
        for s in range(CMP_STRIDE):
            piece = by_row[2 * s * chunks_per_slab:2 * (s + 1) * chunks_per_slab, :]
            xk_sc[rows, s * KV_WIDTH:(s + 1) * KV_WIDTH] = piece[:, 0:KV_WIDTH]
            xv_sc[rows, s * KV_WIDTH:(s + 1) * KV_WIDTH] = piece[:, KV_WIDTH:2 * KV_WIDTH]
    n_chunks = n_src * chunks_per_slab
    acc_k = _dot(xk_sc[...], wk_ref[...])
    acc_v = _dot(xv_sc[...], wv_ref[...])
    hd = CMP_HIDDEN

    def pos_term(pos_ref, w1_ref):
        p = jnp.broadcast_to(pos_ref[...], (8, pos_ref.shape[1]))
        p_hi, p_lo = _split(p)
        w_hi, w_lo = _split(w1_ref[...])
        return (_dot(p_hi, w_hi) + _dot(p_hi, w_lo) + _dot(p_lo, w_hi))[0:1, :]

    def hidden(acc, pos):
        out = []
        for h in range(NSA_KV_HEADS):
            j0 = acc[:, 2 * h * hd:(2 * h + 1) * hd]
            j1 = acc[:, (2 * h + 1) * hd:(2 * h + 2) * hd]
            out.append(j0 + pltpu.roll(j1, n_chunks - 1, 0) + pos)
        return _silu(jnp.concatenate(out, axis=1)).astype(BF16)

    kc = _dot(hidden(acc_k, pos_term(posk_ref, w1k_ref)), w2k_ref[...])
    kc_ref[0] = _head_norm(kc, ones_ref[...], kg0_ref[...])
    vc_ref[0] = _dot(hidden(acc_v, pos_term(posv_ref, w1v_ref)), w2v_ref[...])


def _cmp1_weights(w1):
    w = w1.reshape(2, CMP_STRIDE, HEAD_DIM, CMP_HIDDEN)
    eye = jnp.eye(NSA_KV_HEADS, dtype=w1.dtype)
    big = jnp.einsum("jsde,hg->shdgje", w, eye)
    return big.reshape(CMP_STRIDE * KV_WIDTH, 4 * CMP_HIDDEN).astype(BF16)


def _cmp_call(src_args, src_specs, n_src, n_prefetch, grid_spec_fn, consts, n_seq, name):
    chunks = n_src * PAGE_SIZE // CMP_STRIDE
    lhs = pltpu.VMEM((chunks, CMP_STRIDE * KV_WIDTH), BF16)
    const_specs = [pl.BlockSpec(c.shape, (lambda *_, nd=c.ndim: (0,) * nd)) for c in consts]
    seq = lambda i, *_: (i, 0, 0)
    return pl.pallas_call(
        functools.partial(_cmp_body, n_src=n_src, n_prefetch=n_prefetch),
        grid_spec=grid_spec_fn(
            grid=(n_seq,),
            in_specs=src_specs + const_specs,
            out_specs=[pl.BlockSpec((1, chunks, KV_WIDTH), seq)] * 2,
            scratch_shapes=[lhs, lhs],
        ),
        out_shape=[jax.ShapeDtypeStruct((n_seq, chunks, KV_WIDTH), F32)] * 2,
        compiler_params=pltpu.CompilerParams(dimension_semantics=("parallel",), vmem_limit_bytes=VMEM_LIMIT),
        name=name,
    )(*src_args, *consts)


def _cmp_prompt(rows_t, consts):
    b, _, _, t = rows_t.shape
    spec = pl.BlockSpec((None, 2, KV_WIDTH, t), lambda i: (i, 0, 0, 0))
    grid_spec_fn = functools.partial(pltpu.PrefetchScalarGridSpec, num_scalar_prefetch=0)
    return _cmp_call([rows_t], [spec], t // PAGE_SIZE, 0, grid_spec_fn, consts, b, "cmp_prompt")


def _page_specs(kind_block, n_pages):
    def spec(i):
        return pl.BlockSpec((None, 2, KV_WIDTH, PAGE_SIZE), lambda b, pt: (pt[b, i], kind_block, 0, 0))
    return [spec(i) for i in range(n_pages)]


def _cmp_sample(cache_t, page_table, consts):
    bs, n_pages = page_table.shape
    grid_spec_fn = functools.partial(pltpu.PrefetchScalarGridSpec, num_scalar_prefetch=1)
    return _cmp_call([page_table] + [cache_t] * n_pages, _page_specs(0, n_pages), n_pages, 1, grid_spec_fn, consts, bs,
                     "cmp_sample")


def _stack_heads(q, kvh):
    base = kvh * NSA_GROUP * HEAD_DIM
    return jnp.concatenate([q[:, base + g * HEAD_DIM: base + (g + 1) * HEAD_DIM] for g in range(NSA_GROUP)], axis=0)


def _unstack_heads(o, rows):
    return jnp.concatenate([o[g * rows:(g + 1) * rows] for g in range(NSA_GROUP)], axis=1)


def _group_sum(p, rows):
    out = p[0:rows]
    for g in range(1, NSA_GROUP):
        out = out + p[g * rows:(g + 1) * rows]
    return out


def _select_blocks(imp, qpos, n_blocks):
    blk = lax.broadcasted_iota(jnp.int32, imp.shape, 1)
    visible = blk * SEL_BLOCK <= qpos
    forced = jnp.logical_or(blk == 0, blk == (qpos >> int(math.log2(SEL_BLOCK))))
    score = jnp.where(forced, FORCE_SCORE, jnp.where(visible, imp, -jnp.inf))
    beaten = jnp.zeros(imp.shape, F32)
    for i in range(n_blocks):
        ci = score[:, i:i + 1]
        earlier = jnp.where(blk > i, 1.0, 0.0)
        beaten = beaten + jnp.where(ci > score, 1.0, jnp.where(ci == score, earlier, 0.0))
    return jnp.where(beaten < float(SEL_TOPK), jnp.where(score > -jnp.inf, 1.0, 0.0), 0.0)


def _softmax_cols(s):
    m = jnp.max(s, axis=0, keepdims=True)
    m = jnp.where(m == -jnp.inf, 0.0, m)
    e = jnp.exp2(s - m)
    return e / jnp.maximum(jnp.sum(e, axis=0, keepdims=True), 1e-30)


def _select_blocks_t(imp_t, qpos, n_blocks):
    blk = lax.broadcasted_iota(jnp.int32, imp_t.shape, 0)
    visible = blk * SEL_BLOCK <= qpos
    forced = jnp.logical_or(blk == 0, blk == (qpos >> int(math.log2(SEL_BLOCK))))
    score = jnp.where(forced, FORCE_SCORE, jnp.where(visible, imp_t, -jnp.inf))
    beaten = jnp.zeros(imp_t.shape, F32)
    for i in range(n_blocks):
        ci = score[i:i + 1, :]
        earlier = jnp.where(blk > i, 1.0, 0.0)
        beaten = beaten + jnp.where(ci > score, 1.0, jnp.where(ci == score, earlier, 0.0))
    return jnp.where(beaten < float(SEL_TOPK), jnp.where(score > -jnp.inf, 1.0, 0.0), 0.0)


def _combine_branches(gate_logits, gex_ref, branches):
    gates = 1.0 / (1.0 + jnp.exp(-gate_logits))
    out = None
    for c, branch in enumerate(branches):
        term = _dot2(gates, gex_ref[c]) * branch
        out = term if out is None else out + term
    return out


def _nsa_prompt_body(qn_ref, qr_ref, kc_ref, vc_ref, kvb_ref, vt_ref, ovt_ref, ext_ref, misc_ref, gex_ref, o_ref, *,
                     seq_len):
    tq = Q_TILE
    tk = SEL_KEY_TILE
    q0 = pl.program_id(1) * tq
    n_cmp = seq_len // CMP_STRIDE - 1
    n_sel = seq_len // SEL_BLOCK
    cols = NSA_GROUP * tq
    qn_t = qn_ref[0]
    qr_t = qr_ref[0]
    kc = kc_ref[0].astype(BF16)
    vc_t = vc_ref[0].T.astype(BF16)
    qpos1 = q0 + lax.broadcasted_iota(jnp.int32, (1, tq), 1)
    qpos4 = jnp.concatenate([qpos1] * NSA_GROUP, axis=1)
    win_tiles = (WINDOW + tq) // LANES
    wstart = pl.multiple_of(jnp.maximum(q0 - WINDOW, 0), tq)
    n_tiles = (q0 + tq + tk - 1) // tk

    def group_q(q_t, kvh):
        base = kvh * NSA_GROUP * HEAD_DIM
        return jnp.concatenate([q_t[base + g * HEAD_DIM: base + (g + 1) * HEAD_DIM, :] for g in range(NSA_GROUP)],
                               axis=1)

    def per_head(o_t):
        return [o_t[:, g * tq:(g + 1) * tq] for g in range(NSA_GROUP)]

    oc, osel, ow = [], [], []
    kv_heads = range(NSA_KV_HEADS)
    dims = [slice(h * HEAD_DIM, (h + 1) * HEAD_DIM) for h in kv_heads]
    qs_ts = [group_q(qr_t, h) for h in kv_heads]

    s_c = [_dot(kc[:, dims[h]], group_q(qn_t, h)) for h in kv_heads]
    k_w = [kvb_ref[0, pl.ds(wstart, win_tiles * LANES), 2 * KV_WIDTH + h * HEAD_DIM:2 * KV_WIDTH + (h + 1) * HEAD_DIM]
           for h in kv_heads]
    s_w = [_dot(k_w[h], qs_ts[h]) for h in kv_heads]
    n_idx = lax.broadcasted_iota(jnp.int32, s_c[0].shape, 0)
    vis = jnp.logical_and(n_idx * CMP_STRIDE + (CMP_BLOCK - 1) <= qpos4, n_idx < n_cmp)
    p_c = [_softmax_cols(jnp.where(vis, s, -jnp.inf)) for s in s_c]
    rel = qpos4 - (wstart + lax.broadcasted_iota(jnp.int32, s_w[0].shape, 0))
    inside = jnp.logical_and(rel >= 0, rel < WINDOW)
    p_w = [_softmax_cols(jnp.where(inside, s, -jnp.inf)) for s in s_w]
    w0 = wstart // LANES
    for h in kv_heads:
        oc += per_head(_dot(vc_t[dims[h], :], p_c[h].astype(BF16)))
    for h in kv_heads:
        v_wt = jnp.concatenate([vt_ref[0, w0 + j, KV_WIDTH + h * HEAD_DIM:KV_WIDTH + (h + 1) * HEAD_DIM, :]
                                for j in range(win_tiles)], axis=1)
        ow += per_head(_dot(v_wt, p_w[h].astype(BF16)))

    imp_t = []
    for h in kv_heads:
        p_sum = p_c[h][:, 0:tq]
        for g in range(1, NSA_GROUP):
            p_sum = p_sum + p_c[h][:, g * tq:(g + 1) * tq]
        p_hi, p_lo = _split(p_sum)
        imp_t.append((_dot(ovt_ref[...], p_hi) + _dot(ovt_ref[...], p_lo))[0:n_sel, :])
    sel_both = _select_blocks_t(jnp.concatenate(imp_t, axis=1), jnp.concatenate([qpos1] * NSA_KV_HEADS, axis=1), n_sel)
    sel_both = jnp.concatenate([sel_both, jnp.zeros((LANES - n_sel, NSA_KV_HEADS * tq), F32)], axis=0).astype(BF16)
    sel_ts = [sel_both[:, h * tq:(h + 1) * tq] for h in kv_heads]

    ones_rows = jnp.ones((ONES_ROWS, tk), BF16)

    def sel_step(kt, carry):
        k0 = pl.multiple_of(kt * tk, tk)
        causal = (k0 + lax.broadcasted_iota(jnp.int32, (tk, tq), 0)) <= qpos1
        k_blk = [kvb_ref[0, pl.ds(k0, tk), h * HEAD_DIM:(h + 1) * HEAD_DIM] for h in kv_heads]
        v_t = [jnp.concatenate([vt_ref[0, kt * (tk // LANES) + j, h * HEAD_DIM:(h + 1) * HEAD_DIM, :]
                                for j in range(tk // LANES)], axis=1) for h in kv_heads]
        v_t = [jnp.concatenate([v, ones_rows], axis=0) for v in v_t]
        chosen = [_dot(ext_ref[kt], sel_ts[h]) for h in kv_heads]
        bias = [jnp.where(jnp.logical_and(c > 0.5, causal), 0.0, -jnp.inf) for c in chosen]
        sc = [_dot(k_blk[h], qs_ts[h]) + jnp.concatenate([bias[h]] * NSA_GROUP, axis=1) for h in kv_heads]
        m_new = [jnp.maximum(carry[h][0], jnp.max(sc[h], axis=0, keepdims=True)) for h in kv_heads]
        pe = [jnp.exp2(sc[h] - m_new[h]).astype(BF16) for h in kv_heads]
        return tuple((m_new[h], jnp.exp2(carry[h][0] - m_new[h]) * carry[h][1] + _dot(v_t[h], pe[h])) for h in kv_heads)

    init = (jnp.full((1, cols), NEG, F32), jnp.zeros((HEAD_DIM + ONES_ROWS, cols), F32))
    for _, acc_f in lax.fori_loop(0, n_tiles, sel_step, (init,) * NSA_KV_HEADS):
        osel += per_head(acc_f[0:HEAD_DIM, :] / jnp.maximum(acc_f[HEAD_DIM:HEAD_DIM + 1, :], 1e-30))

    branches = [jnp.concatenate(o, axis=0).T for o in (oc, osel, ow)]
    o_ref[0] = _combine_branches(misc_ref[0], gex_ref, branches)


def _nsa_prompt(qn_t, qr_t, kc, vc, kvb, vt, ovt, ext, misc, gex):
    b, _, t = qn_t.shape
    chunks = kc.shape[1]
    qtile = lambda i, j: (i, 0, j)
    otile = lambda i, j: (i, j, 0)
    seq = lambda i, j: (i, 0, 0)
    return pl.pallas_call(
        functools.partial(_nsa_prompt_body, seq_len=t),
        grid=(b, t // Q_TILE),
        in_specs=[
            pl.BlockSpec((1, NSA_WIDTH, Q_TILE), qtile),
            pl.BlockSpec((1, NSA_WIDTH, Q_TILE), qtile),
            pl.BlockSpec((1, chunks, KV_WIDTH), seq),
            pl.BlockSpec((1, chunks, KV_WIDTH), seq),
            pl.BlockSpec((1, t, 4 * KV_WIDTH), seq),
            pl.BlockSpec((1,) + vt.shape[1:], lambda i, j: (i, 0, 0, 0)),
            pl.BlockSpec(ovt.shape, lambda i, j: (0, 0)),
            pl.BlockSpec(ext.shape, lambda i, j: (0, 0, 0)),
            pl.BlockSpec((1, Q_TILE, MISC_WIDTH), otile),
            pl.BlockSpec(gex.shape, lambda i, j: (0, 0, 0)),
        ],
        out_specs=pl.BlockSpec((1, Q_TILE, NSA_WIDTH), otile),
        out_shape=jax.ShapeDtypeStruct((b, t, NSA_WIDTH), F32),
        compiler_params=pltpu.CompilerParams(dimension_semantics=("parallel", "parallel"),
                                             vmem_limit_bytes=VMEM_LIMIT),
        name="nsa_prompt",
    )(qn_t, qr_t, kc, vc, kvb, vt, ovt, ext, misc, gex)


def _nsa_sample_body(*refs, n_pages, past_len, n_new):
    pages = refs[1:1 + n_pages]
    (qn_ref, qr_ref, kc_ref, vc_ref, ov_ref, ex_ref, kvb_ref, cw_ref, nw_ref, misc_ref, gex_ref,
     o_ref, neww_ref) = refs[1 + n_pages:]
    t = n_new
    rows4 = NSA_GROUP * t
    n_cmp = (past_len + t) // CMP_STRIDE - 1
    n_sel = -(-(past_len + t) // SEL_BLOCK)
    qn = qn_ref[0]
    qr = qr_ref[0]
    kc = kc_ref[0].astype(BF16)
    vc = vc_ref[0].astype(BF16)
    qpos1 = past_len + lax.broadcasted_iota(jnp.int32, (t, 1), 0)
    qpos4 = jnp.concatenate([qpos1] * NSA_GROUP, axis=0)
    qs = [_stack_heads(qr, kvh) for kvh in range(NSA_KV_HEADS)]

    kv_heads = range(NSA_KV_HEADS)
    dims = [slice(h * HEAD_DIM, (h + 1) * HEAD_DIM) for h in kv_heads]

    wb = cw_ref.shape[2]
    cw = cw_ref[0]
    nw_t = jnp.concatenate([nw_ref[0], jnp.zeros((LANES - t, 2 * KV_WIDTH), F32)], axis=0).T
    shifted = pltpu.roll(cw, wb - t, 1)
    lane = lax.broadcasted_iota(jnp.int32, (2 * KV_WIDTH, LANES), 1)
    neww_ref[0, :, 0:wb - LANES] = shifted[:, 0:wb - LANES]
    neww_ref[0, :, wb - LANES:wb] = jnp.where(lane >= LANES - t, pltpu.roll(nw_t, LANES - t, 1), shifted[:, wb - LANES:wb])
    keys_t = jnp.concatenate([cw, nw_t], axis=1).astype(BF16)

    zero = jnp.zeros((rows4, HEAD_DIM), BF16)
    q_bd = jnp.concatenate([jnp.concatenate([qs[0], zero], axis=1), jnp.concatenate([zero, qs[1]], axis=1)], axis=0)
    k_t = jnp.concatenate([p[0] for p in pages], axis=1).astype(BF16)
    v_t = jnp.concatenate([p[1] for p in pages], axis=1).astype(BF16)
    pad = jnp.zeros((LANES - t, KV_WIDTH), F32)
    k_new = jnp.concatenate([kvb_ref[0, :, 0:KV_WIDTH].astype(F32), pad], axis=0).astype(BF16)
    v_new = jnp.concatenate([kvb_ref[0, :, KV_WIDTH:2 * KV_WIDTH].astype(F32), pad], axis=0).astype(BF16)
    s_cmp = [_dot_nt(_stack_heads(qn, h), kc[:, dims[h]]) for h in kv_heads]
    s_win = [_dot(qs[h], keys_t[dims[h], :]) for h in kv_heads]
    s_past = _dot(q_bd, k_t)
    s_new = _dot_nt(q_bd, k_new)

    n_idx = lax.broadcasted_iota(jnp.int32, s_cmp[0].shape, 1)
    vis = jnp.logical_and(n_idx * CMP_STRIDE + (CMP_BLOCK - 1) <= qpos4, n_idx < n_cmp)
    p_cmp = [_softmax_rows(jnp.where(vis, s, -jnp.inf)) for s in s_cmp]
    o_cmp = jnp.concatenate([_unstack_heads(_dot(p_cmp[h].astype(BF16), vc[:, dims[h]]), t) for h in kv_heads], axis=1)
    imp = [_dot2(_group_sum(p_cmp[h], t), ov_ref[...]) for h in kv_heads]

    key_i = lax.broadcasted_iota(jnp.int32, s_win[0].shape, 1)
    q_t = lax.broadcasted_iota(jnp.int32, s_win[0].shape, 0) & (t - 1)
    rel = wb + q_t - key_i
    inside = jnp.logical_and(rel >= 0, rel < WINDOW)
    p_win = [_softmax_rows(jnp.where(inside, s, -jnp.inf)) for s in s_win]
    o_win = jnp.concatenate(
        [_unstack_heads(_dot_nt(p_win[h].astype(BF16), keys_t[KV_WIDTH + h * HEAD_DIM:KV_WIDTH + (h + 1) * HEAD_DIM, :]), t)
         for h in kv_heads], axis=1)

    sel = _select_blocks(jnp.concatenate(imp, axis=0), jnp.concatenate([qpos1] * NSA_KV_HEADS, axis=0), n_sel)
    sel_rows = jnp.concatenate(
        [jnp.concatenate([sel[h * t:(h + 1) * t, 0:ex_ref.shape[0]]] * NSA_GROUP, axis=0) for h in kv_heads],
        axis=0).astype(BF16)
    chosen = _dot(sel_rows, ex_ref[...])
    sc = jnp.where(chosen > 0.5, s_past, -jnp.inf)
    key_i = lax.broadcasted_iota(jnp.int32, s_new.shape, 1)
    q_t = lax.broadcasted_iota(jnp.int32, s_new.shape, 0) & (t - 1)
    sn = jnp.where(key_i <= q_t, s_new, -jnp.inf)
    m = jnp.maximum(jnp.max(sc, axis=-1, keepdims=True), jnp.max(sn, axis=-1, keepdims=True))
    e_past = jnp.exp2(sc - m)
    e_new = jnp.exp2(sn - m)
    denom = jnp.sum(e_past, axis=-1, keepdims=True) + jnp.sum(e_new, axis=-1, keepdims=True)
    o_both = (_dot_nt(e_past.astype(BF16), v_t) + _dot(e_new.astype(BF16), v_new)) / denom
    o_sel = jnp.concatenate([_unstack_heads(o_both[0:rows4, 0:HEAD_DIM], t),
                             _unstack_heads(o_both[rows4:2 * rows4, HEAD_DIM:KV_WIDTH], t)], axis=1)
    o_ref[0] = _combine_branches(misc_ref[0], gex_ref, [o_cmp, o_sel, o_win])


def _nsa_sample(cache_t, page_table, qn, qr, kc, vc, ov, ex, kvb, cache_win_t, new_win, misc, gex, past_len):
    bs, n_pages = page_table.shape
    t = qr.shape[1]
    chunks = kc.shape[1]
    wb = cache_win_t.shape[2]
    seq = lambda b, pt: (b, 0, 0)
    c2 = lambda b, pt: (0, 0)
    grid_spec = pltpu.PrefetchScalarGridSpec(
        num_scalar_prefetch=1,
        grid=(bs,),
        in_specs=_page_specs(1, n_pages) + [
            pl.BlockSpec((1, t, NSA_WIDTH), seq), pl.BlockSpec((1, t, NSA_WIDTH), seq),
            pl.BlockSpec((1, chunks, KV_WIDTH), seq), pl.BlockSpec((1, chunks, KV_WIDTH), seq),
            pl.BlockSpec(ov.shape, c2), pl.BlockSpec(ex.shape, c2),
            pl.BlockSpec((1, t, 4 * KV_WIDTH), seq),
            pl.BlockSpec((1, 2 * KV_WIDTH, wb), seq), pl.BlockSpec((1, t, 2 * KV_WIDTH), seq),
            pl.BlockSpec((1, t, MISC_WIDTH), seq), pl.BlockSpec(gex.shape, lambda b, pt: (0, 0, 0)),
        ],
        out_specs=[pl.BlockSpec((1, t, NSA_WIDTH), seq), pl.BlockSpec((1, 2 * KV_WIDTH, wb), seq)],
    )
    return pl.pallas_call(
        functools.partial(_nsa_sample_body, n_pages=n_pages, past_len=past_len, n_new=t),
        grid_spec=grid_spec,
        out_shape=[jax.ShapeDtypeStruct((bs, t, NSA_WIDTH), F32), jax.ShapeDtypeStruct((bs, 2 * KV_WIDTH, wb), F32)],
        compiler_params=pltpu.CompilerParams(dimension_semantics=("parallel",), vmem_limit_bytes=VMEM_LIMIT),
        name="nsa_sample",
    )(page_table, *([cache_t] * n_pages), qn, qr, kc, vc, ov, ex, kvb, cache_win_t, new_win, misc, gex)


def _gla_constants(n_seq):
    tile = GLA_TILE
    seq_len = tile // n_seq
    t = np.arange(tile)[:, None]
    r = np.arange(tile)[None, :]
    same = (t // seq_len) == (r // seq_len)
    ops = [same & (r <= t), same & (r > t)]
    masks = []
    m = 1
    while 2 * m <= seq_len:
        upper = (t % (2 * m)) >= m
        mid = (t // (2 * m)) * (2 * m) + m - 1
        ops.append(upper & (r > mid) & (r <= t))
        ops.append((~upper) & (r > t) & (r <= mid))
        same_blk = (t // (2 * m)) == (r // (2 * m))
        masks.append(same_blk & upper & ((r % (2 * m)) < m))
        m *= 2
    ops = np.concatenate([o.astype(np.float32) for o in ops], axis=0)
    masks = np.stack([k.astype(np.float32) for k in masks], axis=0)
    lane_seq = (np.arange(n_seq * GLA_DK)[None, :] // GLA_DK) == (np.arange(tile)[:, None] // seq_len)
    rep = (np.arange(n_seq * GLA_DK)[None, :] % GLA_DK) == np.arange(GLA_DK)[:, None]
    return (jnp.asarray(ops, BF16), jnp.asarray(masks, F32), jnp.asarray(lane_seq.astype(np.float32), F32),
            jnp.asarray(rep.astype(np.float32), BF16))


def _gla_body(gq_ref, gk_ref, gv_ref, gg_ref, misc_ref, s0_ref, wa_hi_ref, wa_lo_ref, ba_ref, ops_ref, masks_ref,
              lane_seq_ref, rep_ref, gnorm_ref, o_ref, sfin_ref, st_sc, *, n_levels):
    step = pl.program_id(1)

    @pl.when(step == 0)
    def _():
        st_sc[...] = s0_ref[...]

    tile = GLA_TILE
    groups = range(gq_ref.shape[0])
    chains = [(g, h) for g in groups for h in range(GLA_HEADS)]
    lane_seq = lane_seq_ref[...]
    rep = rep_ref[...]
    multi = lane_seq.shape[1] > GLA_DK

    def spread(a):
        a = a.astype(BF16)
        return (_dot(a, rep) * lane_seq).astype(BF16) if multi else a

    la, cums = [], []
    for g in groups:
        m_hi, m_lo = _split(misc_ref[g])
        x = _dot(m_hi, wa_hi_ref[...]) + _dot(m_hi, wa_lo_ref[...]) + _dot(m_lo, wa_hi_ref[...]) + ba_ref[...]
        la.append((jnp.minimum(x, 0.0) - jnp.log(1.0 + jnp.exp(-jnp.abs(x)))) * (1.0 / GLA_GATE_TAU))
    for g in groups:
        la_hi, la_lo = _split(la[g])
        cums.append(_dot(ops_ref[...], la_hi) + _dot(ops_ref[...], la_lo))

    dks = [slice(h * GLA_DK, (h + 1) * GLA_DK) for h in range(GLA_HEADS)]
    dvs = [slice(h * GLA_DV, (h + 1) * GLA_DV) for h in range(GLA_HEADS)]
    q = [gq_ref[g, :, dks[h]] * (GLA_DK ** -0.5) for g, h in chains]
    k = [gk_ref[g, :, dks[h]] for g, h in chains]
    v = [gv_ref[g, :, dvs[h]] for g, h in chains]
    st = [st_sc[g, h] for g, h in chains]
    o = [_dot_nt(spread(q[c] * jnp.exp(cums[g][0:tile, dks[h]])), st[c].astype(BF16)) for c, (g, h) in enumerate(chains)]
    a = [None] * len(chains)
    for lv in range(n_levels):
        for c, (g, h) in enumerate(chains):
            dq = cums[g][(2 + 2 * lv) * tile:(3 + 2 * lv) * tile, dks[h]]
            dkk = cums[g][(3 + 2 * lv) * tile:(4 + 2 * lv) * tile, dks[h]]
            term = _dot_nt((q[c] * jnp.exp(dq)).astype(BF16), (k[c] * jnp.exp(dkk)).astype(BF16)) * masks_ref[lv]
            a[c] = term if a[c] is None else a[c] + term
    o = [o[c] + _dot(a[c].astype(BF16), v[c].astype(BF16)) + jnp.sum(q[c] * k[c], axis=-1, keepdims=True) * v[c]
         for c in range(len(chains))]
    for c, (g, h) in enumerate(chains):
        la_h = la[g][:, dks[h]]
        if multi:
            total = jnp.sum(_dot2(la_h, rep) * lane_seq, axis=0, keepdims=True)
        else:
            total = jnp.sum(la_h, axis=0, keepdims=True)
        until_end = cums[g][tile:2 * tile, dks[h]]
        st_sc[g, h] = st[c] * jnp.exp(total) + _dot(v[c].T.astype(BF16), spread(k[c] * jnp.exp(until_end)))
    for c, (g, h) in enumerate(chains):
        ms = jnp.mean(o[c] * o[c], axis=-1, keepdims=True)
        o_ref[g, :, dvs[h]] = o[c] * lax.rsqrt(ms + EPS) * gnorm_ref[...] * _silu(gg_ref[g, :, dvs[h]])

    @pl.when(step == pl.num_programs(1) - 1)
    def _():
        sfin_ref[...] = st_sc[...]


def _gla(gq, gk, gv, gg, misc, s0_t, wa_hi, wa_lo, ba, gnorm, n_seq):
    groups, rows, _ = gq.shape
    steps = rows // GLA_TILE
    par = math.gcd(groups, GLA_GROUPS_PER_STEP)
    ops, masks, lane_seq, rep = _gla_constants(n_seq)
    n_levels = masks.shape[0]
    tile = lambda g, s: (g, s, 0)
    c2 = lambda g, s: (0, 0)
    c3 = lambda g, s: (0, 0, 0)
    st_block = (par, GLA_HEADS, GLA_DV, n_seq * GLA_DK)
    st_map = lambda g, s: (g, 0, 0, 0)
    return pl.pallas_call(
        functools.partial(_gla_body, n_levels=n_levels),
        grid=(groups // par, steps),
        in_specs=[
            pl.BlockSpec((par, GLA_TILE, GLA_QK_WIDTH), tile), pl.BlockSpec((par, GLA_TILE, GLA_QK_WIDTH), tile),
            pl.BlockSpec((par, GLA_TILE, GLA_WIDTH), tile), pl.BlockSpec((par, GLA_TILE, GLA_WIDTH), tile),
            pl.BlockSpec((par, GLA_TILE, MISC_WIDTH), tile),
            pl.BlockSpec(st_block, st_map),
            pl.BlockSpec(wa_hi.shape, c2), pl.BlockSpec(wa_lo.shape, c2), pl.BlockSpec(ba.shape, c2),
            pl.BlockSpec(ops.shape, c2), pl.BlockSpec(masks.shape, c3),
            pl.BlockSpec(lane_seq.shape, c2), pl.BlockSpec(rep.shape, c2), pl.BlockSpec(gnorm.shape, c2),
        ],
        out_specs=[pl.BlockSpec((par, GLA_TILE, GLA_WIDTH), tile), pl.BlockSpec(st_block, st_map)],
        out_shape=[jax.ShapeDtypeStruct((groups, rows, GLA_WIDTH), F32),
                   jax.ShapeDtypeStruct((groups,) + st_block[1:], F32)],
        scratch_shapes=[pltpu.VMEM(st_block, F32)],
        compiler_params=pltpu.CompilerParams(dimension_semantics=("parallel", "arbitrary"),
                                             vmem_limit_bytes=VMEM_LIMIT),
        name="gla",
    )(gq, gk, gv, gg, misc, s0_t, wa_hi, wa_lo, ba, ops, masks, lane_seq, rep, gnorm)


def _out_body(x_ref, on_ref, og_ref, wout_ref, gffn_ref, wa_ref, wb_ref, wdown_ref, y_ref, hn_sc):
    c = pl.program_id(1)

    @pl.when(c == 0)
    def _():
        x1 = (x_ref[...] + _dot(on_ref[...].astype(BF16), wout_ref[0:NSA_WIDTH, :])
              + _dot(og_ref[...].astype(BF16), wout_ref[NSA_WIDTH:NSA_WIDTH + GLA_WIDTH, :]))
        ms = jnp.mean(x1 * x1, axis=-1, keepdims=True)
        hn_sc[...] = (x1 * lax.rsqrt(ms + EPS) * gffn_ref[...]).astype(BF16)
        y_ref[...] = x1

    hn = hn_sc[...]
    act = _silu(_dot(hn, wa_ref[...])) * _dot(hn, wb_ref[...])
    y_ref[...] += _dot(act.astype(BF16), wdown_ref[...])


def _out(x2d, o_nsa, og, wout, gffn, wup, wdown):
    n = x2d.shape[0]
    tm = min(OUT_ROW_TILE, n)
    assert n % tm == 0
    n_chunks = FFN_HIDDEN // FFN_CHUNK
    row = lambda i, c: (i, 0)
    c2 = lambda i, c: (0, 0)
    return pl.pallas_call(
        _out_body,
        grid=(n // tm, n_chunks),
        in_specs=[
            pl.BlockSpec((tm, D_MODEL), row),
            pl.BlockSpec((tm, NSA_WIDTH), row), pl.BlockSpec((tm, GLA_WIDTH), row),
            pl.BlockSpec(wout.shape, c2),
            pl.BlockSpec(gffn.shape, c2),
            pl.BlockSpec((D_MODEL, FFN_CHUNK), lambda i, c: (0, c)),
            pl.BlockSpec((D_MODEL, FFN_CHUNK), lambda i, c: (0, n_chunks + c)),
            pl.BlockSpec((FFN_CHUNK, D_MODEL), lambda i, c: (c, 0)),
        ],
        out_specs=pl.BlockSpec((tm, D_MODEL), row),
        out_shape=jax.ShapeDtypeStruct((n, D_MODEL), F32),
        scratch_shapes=[pltpu.VMEM((tm, D_MODEL), BF16)],
        compiler_params=pltpu.CompilerParams(dimension_semantics=("parallel", "arbitrary"),
                                             vmem_limit_bytes=VMEM_LIMIT),
        name="out",
    )(x2d, o_nsa, og, wout, gffn, wup, wup, wdown)


def _rope_tables(pos):
    half = HEAD_DIM // 2
    inv = ROPE_THETA ** (-jnp.arange(half, dtype=F32) / half)
    ang = pos.astype(F32)[:, None] * inv[None, :]
    cos, sin = jnp.cos(ang), jnp.sin(ang)
    reps = LANES // HEAD_DIM
    return jnp.tile(jnp.concatenate([cos, cos], axis=1), (1, reps)), jnp.tile(jnp.concatenate([-sin, sin], axis=1), (1, reps))


def _overlap_matrix(n_chunks, n_cols):
    start = np.arange(n_chunks)[:, None] * CMP_STRIDE
    j0 = np.arange(n_cols)[None, :] * SEL_BLOCK
    return jnp.asarray(((start < j0 + SEL_BLOCK) & (start + CMP_BLOCK > j0)).astype(np.float32), BF16)


def _expand_matrix(n_tiles, keys):
    key = np.arange(n_tiles)[:, None, None] * keys + np.arange(keys)[None, None, :]
    blk = np.arange(LANES)[None, :, None]
    return jnp.asarray((key // SEL_BLOCK == blk).astype(np.float32), BF16)


def _chunk_row_permutation():
    out = np.arange(2 * PAGE_SIZE)
    chunks = PAGE_SIZE // CMP_STRIDE
    s, page, c = out // (2 * chunks), (out // chunks) % 2, out % chunks
    src = page * PAGE_SIZE + c * CMP_STRIDE + s
    return jnp.asarray((src[:, None] == np.arange(2 * PAGE_SIZE)[None, :]).astype(np.float32), BF16)


def _block_diag_ones(width):
    i = np.arange(width)
    return jnp.asarray((i[:, None] // HEAD_DIM == i[None, :] // HEAD_DIM).astype(np.float32), BF16)


def _gate_expand():
    lane = np.arange(MISC_WIDTH)[None, :, None]
    out = np.arange(NSA_WIDTH)[None, None, :]
    c = np.arange(3)[:, None, None]
    return jnp.asarray(((lane == (out // HEAD_DIM) * 3 + c) & (lane < GATE_COLS)).astype(np.float32), BF16)


def _pad_cols(w, width):
    return jnp.pad(w, ((0, 0), (0, width - w.shape[1])))


def kernel(x_prompt, x_sample, cache_nsa_kv, cache_win_kv, state_gla, page_table, w_norm_mix, w_in, nsa_q_gain,
           nsa_k_gain, cmp_pos_k, cmp_w1_k, cmp_w2_k, cmp_pos_v, cmp_w1_v, cmp_w2_v, gla_w_a2, gla_b_a,
           gla_norm_gain, w_out, w_norm_ffn, w_ffn_up, w_ffn_down):
    b_p, t_p, _ = x_prompt.shape
    b_s, t_s, _ = x_sample.shape
    depth = w_in.shape[0]
    assert depth == 1, "single layer"
    n_pages = page_table.shape[1]
    past_len = n_pages * PAGE_SIZE
    n_pool = cache_nsa_kv.shape[1]
    wb = cache_win_kv.shape[2]
    assert t_p % SEL_KEY_TILE == 0 and t_p >= WINDOW + Q_TILE and (b_p * t_p) % ROW_TILE == 0
    assert (b_s * t_s) % ROW_TILE == 0 and GLA_TILE % t_s == 0 and b_s % (GLA_TILE // t_s) == 0
    assert wb == WINDOW and ROW_TILE % t_s == 0 and t_p % PAGE_SIZE == 0 and past_len // SEL_BLOCK <= LANES
    l = 0

    q_w, kv_w, gate_w, gq_w, gk_w, gv_w, alow_w, gg_w = jnp.split(
        w_in[l], np.cumsum([NSA_WIDTH, 6 * KV_WIDTH, GATE_COLS, GLA_QK_WIDTH, GLA_QK_WIDTH, GLA_WIDTH,
                            GLA_GATE_RANK])[:].tolist(), axis=1)
    misc_w = _pad_cols(jnp.concatenate([gate_w, alow_w], axis=1), MISC_WIDTH)
    w_perm = jnp.concatenate([q_w, kv_w, gq_w, gk_w, gv_w, gg_w, misc_w], axis=1).astype(BF16)
    gmix = w_norm_mix[l][None, :]
    qg = jnp.tile(nsa_q_gain[l], NSA_HEADS)[None, :]
    kg = [jnp.tile(nsa_k_gain[l, i], NSA_KV_HEADS)[None, :] for i in range(3)]
    ones512 = _block_diag_ones(NSA_WIDTH)
    ones128 = _block_diag_ones(LANES)
    wa = jnp.zeros((MISC_WIDTH, GLA_QK_WIDTH), F32).at[GATE_COLS:GATE_COLS + GLA_GATE_RANK].set(gla_w_a2[l])
    wa_hi, wa_lo = _split(wa)
    ba = gla_b_a[l][None, :]
    gnorm = gla_norm_gain[l][None, :]
    wk1 = _cmp1_weights(cmp_w1_k[l])
    wv1 = _cmp1_weights(cmp_w1_v[l])
    eye2 = jnp.eye(NSA_KV_HEADS, dtype=F32)
    w2k_bd = jnp.einsum("ed,hg->hegd", cmp_w2_k[l], eye2).reshape(2 * CMP_HIDDEN, KV_WIDTH).astype(BF16)
    w2v_bd = jnp.einsum("ed,hg->hegd", cmp_w2_v[l], eye2).reshape(2 * CMP_HIDDEN, KV_WIDTH).astype(BF16)
    flat = CMP_BLOCK * HEAD_DIM
    posk, posv = cmp_pos_k[l].reshape(1, flat), cmp_pos_v[l].reshape(1, flat)
    w1k_flat, w1v_flat = cmp_w1_k[l].reshape(flat, CMP_HIDDEN), cmp_w1_v[l].reshape(flat, CMP_HIDDEN)
    cmp_consts = (_chunk_row_permutation(), wk1, wv1, posk, w1k_flat, posv, w1v_flat, w2k_bd, w2v_bd, kg[0], ones128)
    gex = _gate_expand()
    wout = w_out[l].astype(BF16)
    gffn = w_norm_ffn[l][None, :]
    wup = w_ffn_up[l].astype(BF16)
    wdown = w_ffn_down[l].astype(BF16)

    cos_p, sin_p = _rope_tables(jnp.arange(t_p))
    xp2 = x_prompt.reshape(b_p * t_p, D_MODEL)
    (qn_t, qr_t, rows_t, win_t, kvb, vt, gq, gk, gv, gg, misc) = _proj(
        xp2, gmix, w_perm, cos_p, sin_p, qg, kg[1], kg[2], ones512, t_p // ROW_TILE, seq_len=t_p)
    kc_p, vc_p = _cmp_prompt(rows_t.reshape(b_p, 4, KV_WIDTH, t_p), cmp_consts)
    ovt_p = _overlap_matrix(t_p // CMP_STRIDE, LANES).T
    ext_p = jnp.swapaxes(_expand_matrix(t_p // SEL_KEY_TILE, SEL_KEY_TILE), 1, 2)
    o_nsa = _nsa_prompt(qn_t, qr_t, kc_p, vc_p, kvb.reshape(b_p, t_p, -1),
                        vt.reshape(b_p, t_p // LANES, 2 * KV_WIDTH, LANES), ovt_p, ext_p,
                        misc.reshape(b_p, t_p, MISC_WIDTH), gex)
    s0_p = jnp.zeros((b_p, GLA_HEADS, GLA_DV, GLA_DK), F32)
    r3 = lambda a, g=b_p: a.reshape(g, -1, a.shape[-1])
    og, sfin_p = _gla(r3(gq), r3(gk), r3(gv), r3(gg), r3(misc), s0_p, wa_hi, wa_lo, ba, gnorm, 1)
    y_p = _out(xp2, o_nsa.reshape(b_p * t_p, -1), og.reshape(b_p * t_p, -1), wout, gffn, wup, wdown)
    out_rows_p = rows_t.reshape(b_p, 4, NSA_KV_HEADS, HEAD_DIM, t_p).transpose(0, 4, 1, 2, 3)[None]
    wlen = min(WINDOW, t_p)
    out_win_p = win_t[:, :, t_p - wlen:].reshape(b_p, 2, NSA_KV_HEADS, HEAD_DIM, wlen).transpose(0, 4, 1, 2, 3)[None]
    out_gla_p = jnp.swapaxes(sfin_p, 2, 3)[None].astype(state_gla.dtype)

    pos_s = past_len + jnp.arange(t_s)
    cos_s, sin_s = _rope_tables(pos_s)
    cos_s = jnp.tile(cos_s, (ROW_TILE // t_s, 1))
    sin_s = jnp.tile(sin_s, (ROW_TILE // t_s, 1))
    xs2 = x_sample.reshape(b_s * t_s, D_MODEL)
    (qn_s, qr_s, rows_s, win_s, kvb_s, _, gq_s, gk_s, gv_s, gg_s, misc_s) = _proj(
        xs2, gmix, w_perm, cos_s, sin_s, qg, kg[1], kg[2], ones512, 1)
    cache_t = cache_nsa_kv[l].transpose(0, 2, 3, 4, 1).reshape(n_pool, 4, KV_WIDTH, PAGE_SIZE)
    kc_s, vc_s = _cmp_sample(cache_t, page_table, cmp_consts)
    n_sel_s = -(-(past_len + t_s) // SEL_BLOCK)
    sel_cols = -(-n_sel_s // LANES) * LANES
    ov_s = _overlap_matrix(past_len // CMP_STRIDE, sel_cols)
    ex_s = _expand_matrix(1, past_len)[0]
    seq3 = lambda a: a.reshape(b_s, t_s, a.shape[-1])
    cache_win_t = cache_win_kv[l].transpose(0, 2, 3, 4, 1).reshape(b_s, 2 * KV_WIDTH, wb)
    o_nsa_s, new_win_t = _nsa_sample(cache_t, page_table, seq3(qn_s), seq3(qr_s), kc_s, vc_s, ov_s, ex_s, seq3(kvb_s),
                                     cache_win_t, seq3(win_s), seq3(misc_s), gex, past_len)
    n_seq = GLA_TILE // t_s
    groups = b_s // n_seq
    s0_s = state_gla[l].astype(F32).reshape(groups, n_seq, GLA_HEADS, GLA_DK, GLA_DV)
    s0_s = s0_s.transpose(0, 2, 4, 1, 3).reshape(groups, GLA_HEADS, GLA_DV, n_seq * GLA_DK)
    rg = lambda a: a.reshape(groups, GLA_TILE, a.shape[-1])
    og_s, sfin_s = _gla(rg(gq_s), rg(gk_s), rg(gv_s), rg(gg_s), rg(misc_s), s0_s, wa_hi, wa_lo, ba, gnorm, n_seq)
    y_s = _out(xs2, o_nsa_s.reshape(b_s * t_s, -1), og_s.reshape(b_s * t_s, -1), wout, gffn, wup, wdown)
    out_rows_s = rows_s.reshape(1, b_s, t_s, 4, NSA_KV_HEADS, HEAD_DIM)
    out_win_s = new_win_t.reshape(b_s, 2, NSA_KV_HEADS, HEAD_DIM, wb).transpose(0, 4, 1, 2, 3)[None]
    out_gla_s = sfin_s.reshape(groups, GLA_HEADS, GLA_DV, n_seq, GLA_DK).transpose(0, 3, 1, 4, 2)
    out_gla_s = out_gla_s.reshape(1, b_s, GLA_HEADS, GLA_DK, GLA_DV).astype(state_gla.dtype)

    return (y_p.reshape(b_p, t_p, D_MODEL), y_s.reshape(b_s, t_s, D_MODEL), out_rows_p, out_win_p, out_gla_p,
            out_rows_s, out_win_s, out_gla_s)
```
